```python
import math
import jax, jax.numpy as jnp
from jax import lax
import numpy as np

D_MODEL = 1024
BATCH = 2
SEQ = 16384
DEPTH = 2

LRU_WIDTH = 256
LRU_BLOCKS = 4
LRU_BLOCK = LRU_WIDTH // LRU_BLOCKS
CONV_WIDTH = 4
LRU_C = 8.0
GLA_HEADS = 6
GLA_DK = 64
GLA_DV = 64
GLA_GATE_RANK = 16
GLA_GATE_TAU = 16.0
GLA_CHUNK = 64
DSA_HEADS = 6
DSA_HEAD_DIM = 64
IDX_HEADS = 4
IDX_DIM = 32
IDX_TOPK_MAX = 256
Q_BLOCK = 128
REL_BUCKETS = 32
REL_MAX_DIST = 128
D_FF = -(-8 * D_MODEL // (3 * 256)) * 256
DN_ALPHA = (2.0 * DEPTH) ** 0.25
DN_BETA = (8.0 * DEPTH) ** -0.25
LN_EPS = 1e-5

GLA_WIDTH = GLA_HEADS * GLA_DV
DSA_WIDTH = DSA_HEADS * DSA_HEAD_DIM
MIX_WIDTH = LRU_WIDTH + GLA_WIDTH + DSA_WIDTH
IN_SIZES = (
    LRU_WIDTH, LRU_WIDTH,
    GLA_HEADS * GLA_DK, GLA_HEADS * GLA_DK, GLA_WIDTH, GLA_WIDTH, GLA_GATE_RANK,
    DSA_WIDTH, DSA_WIDTH, DSA_WIDTH,
    IDX_HEADS * IDX_DIM, IDX_DIM, IDX_HEADS,
)
IN_COLS = sum(IN_SIZES)
IN_OFFSETS = [int(o) for o in np.cumsum(IN_SIZES)[:-1]]

kernel_name = "hymba_style_lru_gla_dsa_deepnorm"


def layer_norm(x, g, b):
    xf = x.astype(jnp.float32)
    mu = jnp.mean(xf, axis=-1, keepdims=True)
    var = jnp.mean(jnp.square(xf - mu), axis=-1, keepdims=True)
    return ((xf - mu) * lax.rsqrt(var + LN_EPS) * g + b).astype(x.dtype)


def t5_bucket(dist):
    n = jnp.maximum(dist, 0)
    max_exact = REL_BUCKETS // 2
    nf = jnp.maximum(n, 1).astype(jnp.float32)
    large = max_exact + (jnp.log(nf / max_exact) / math.log(REL_MAX_DIST / max_exact)
                         * (REL_BUCKETS - max_exact)).astype(jnp.int32)
    large = jnp.minimum(large, REL_BUCKETS - 1)
    return jnp.where(n < max_exact, n, large)


def rg_lru_group(xb, gb, conv_w, conv_b, w_a, b_a, w_x, b_x, lam):
    B, S, C = xb.shape
    xc = lax.conv_general_dilated(xb, conv_w[:, None, :], window_strides=(1,),
                                  padding=[(CONV_WIDTH - 1, 0)],
                                  dimension_numbers=('NWC', 'WIO', 'NWC'),
                                  feature_group_count=C) + conv_b
    xr = xc.reshape(B, S, LRU_BLOCKS, LRU_BLOCK)
    r = jax.nn.sigmoid(jnp.einsum('bsnk,nkj->bsnj', xr, w_a).reshape(B, S, C) + b_a)
    i = jax.nn.sigmoid(jnp.einsum('bsnk,nkj->bsnj', xr, w_x).reshape(B, S, C) + b_x)
    log_a = -LRU_C * r.astype(jnp.float32) * jax.nn.softplus(-lam.astype(jnp.float32))
    a = jnp.exp(log_a)
    u = jnp.sqrt(-jnp.expm1(2.0 * log_a)) * (i * xc).astype(jnp.float32)

    def combine(left, right):
        a1, b1 = left
        a2, b2 = right
        return a1 * a2, a2 * b1 + b2

    _, h = lax.associative_scan(combine, (a, u), axis=1)
    return h.astype(xb.dtype) * jax.nn.gelu(gb)


def gla_group(q, k, v, g, g_lr, w_gate2, b_gate, norm_g):
    B, S, _ = q.shape
    N, C = S // GLA_CHUNK, GLA_CHUNK
    f32 = jnp.float32
    qf = q.astype(f32).reshape(B, N, C, GLA_HEADS, GLA_DK) * (GLA_DK ** -0.5)
    kf = k.astype(f32).reshape(B, N, C, GLA_HEADS, GLA_DK)
    vf = v.astype(f32).reshape(B, N, C, GLA_HEADS, GLA_DV)
    log_alpha = jax.nn.log_sigmoid((g_lr @ w_gate2 + b_gate).astype(f32)) / GLA_GATE_TAU
    bcum = jnp.cumsum(log_alpha.reshape(B, N, C, GLA_HEADS, GLA_DK), axis=2)
    b_last = bcum[:, :, -1:]
    q_dec = qf * jnp.exp(bcum)
    k_inv = kf * jnp.exp(-bcum)
    k_end = kf * jnp.exp(b_last - bcum)
    causal = jnp.tril(jnp.ones((C, C), dtype=bool))
    att = jnp.einsum('bnchd,bnshd->bnhcs', q_dec, k_inv)
    att = jnp.where(causal, att, 0.0)
    o_intra = jnp.einsum('bnhcs,bnshv->bnchv', att, vf)
    u = jnp.einsum('bnchd,bnchv->bnhdv', k_end, vf)
    decay = jnp.exp(b_last[:, :, 0])

    def step(state, inp):
        dec, un = inp
        return dec[..., None] * state + un, state

    s0 = jnp.zeros((B, GLA_HEADS, GLA_DK, GLA_DV), f32)
    _, s_start = lax.scan(step, s0, (decay.transpose(1, 0, 2, 3), u.transpose(1, 0, 2, 3, 4)))
    s_start = s_start.transpose(1, 0, 2, 3, 4)
    o_inter = jnp.einsum('bnchd,bnhdv->bnchv', q_dec, s_start)
    o = (o_intra + o_inter).reshape(B, S, GLA_HEADS, GLA_DV)
    o = o * lax.rsqrt(jnp.mean(jnp.square(o), axis=-1, keepdims=True) + 1e-6)
    o = o.reshape(B, S, GLA_WIDTH) * norm_g
    return (o * jax.nn.silu(g.astype(f32))).astype(q.dtype)


def dsa_group(q, k, v, q_idx, k_idx, w_idx, rel_bias):
    B, S, W = q.shape
    top_k = min(IDX_TOPK_MAX, S // 4)
    nblk = S // Q_BLOCK
    qb_all = q.reshape(B, nblk, Q_BLOCK, DSA_HEADS, DSA_HEAD_DIM).transpose(1, 0, 2, 3, 4)
    qi_all = q_idx.reshape(B, nblk, Q_BLOCK, IDX_HEADS, IDX_DIM).transpose(1, 0, 2, 3, 4)
    wi_all = w_idx.reshape(B, nblk, Q_BLOCK, IDX_HEADS).transpose(1, 0, 2, 3)
    k_idx_f = k_idx.astype(jnp.float32)
    key_pos = jnp.arange(S, dtype=jnp.int32)
    starts = jnp.arange(nblk, dtype=jnp.int32) * Q_BLOCK

    def block(inp):
        qb, qib, wb, start = inp
        t = start + jnp.arange(Q_BLOCK, dtype=jnp.int32)
        s_h = jax.nn.relu(jnp.einsum('btid,bsd->btis', qib.astype(jnp.float32), k_idx_f)) * (IDX_DIM ** -0.5)
        score = jnp.einsum('btis,bti->bts', s_h, wb.astype(jnp.float32) * (IDX_HEADS ** -0.5))
        score = jnp.where(key_pos[None, None, :] <= t[None, :, None], score, -jnp.inf)
        _, sel = lax.top_k(score, top_k)
        k_sel = jax.vmap(lambda kk, ss: kk[ss])(k, sel).reshape(B, Q_BLOCK, top_k, DSA_HEADS, DSA_HEAD_DIM)
        v_sel = jax.vmap(lambda vv, ss: vv[ss])(v, sel).reshape(B, Q_BLOCK, top_k, DSA_HEADS, DSA_HEAD_DIM)
        dist = t[None, :, None] - sel
        bias = rel_bias[t5_bucket(dist)]
        logits = jnp.einsum('bthd,btkhd->bthk', qb, k_sel).astype(jnp.float32) * (DSA_HEAD_DIM ** -0.5)
        logits = logits + bias.transpose(0, 1, 3, 2).astype(jnp.float32)
        logits = jnp.where((dist >= 0)[:, :, None, :], logits, -jnp.inf)
        p = jax.nn.softmax(logits, axis=-1).astype(v.dtype)
        o = jnp.einsum('bthk,btkhd->bthd', p, v_sel)
        return o.reshape(B, Q_BLOCK, W)

    out = lax.map(block, (qb_all, qi_all, wi_all, starts))
    return out.transpose(1, 0, 2, 3).reshape(B, S, W)


def setup_inputs(seed: int = 0) -> dict:
    key = jax.random.key(seed)
    ks = jax.random.split(key, 24)
    f32 = jnp.float32

    def nrm(k, shape, scale):
        return jax.random.normal(k, shape, f32) * scale

    u = jax.random.uniform(ks[8], (DEPTH, LRU_WIDTH), f32, 0.9, 0.999)
    a = u ** (1.0 / LRU_C)
    lru_lambda = jnp.log(a) - jnp.log1p(-a)
    return {
        "x": nrm(ks[0], (BATCH, SEQ, D_MODEL), 1.0),
        "w_in": nrm(ks[1], (DEPTH, D_MODEL, IN_COLS), D_MODEL ** -0.5),
        "conv_w": nrm(ks[2], (DEPTH, CONV_WIDTH, LRU_WIDTH), CONV_WIDTH ** -0.5),
        "conv_b": nrm(ks[3], (DEPTH, LRU_WIDTH), 0.01),
        "lru_wa": nrm(ks[4], (DEPTH, LRU_BLOCKS, LRU_BLOCK, LRU_BLOCK), LRU_BLOCK ** -0.5),
        "lru_ba": nrm(ks[5], (DEPTH, LRU_WIDTH), 0.01),
        "lru_wx": nrm(ks[6], (DEPTH, LRU_BLOCKS, LRU_BLOCK, LRU_BLOCK), LRU_BLOCK ** -0.5),
        "lru_bx": nrm(ks[7], (DEPTH, LRU_WIDTH), 0.01),
        "lru_lambda": lru_lambda,
        "gla_w_gate2": nrm(ks[9], (DEPTH, GLA_GATE_RANK, GLA_HEADS * GLA_DK), GLA_GATE_RANK ** -0.5),
        "gla_b_gate": nrm(ks[10], (DEPTH, GLA_HEADS * GLA_DK), 0.01),
        "gla_norm_g": 1.0 + nrm(ks[11], (DEPTH, GLA_WIDTH), 0.01),
        "rel_bias": nrm(ks[12], (REL_BUCKETS, DSA_HEADS), 0.5),
        "w_out": nrm(ks[13], (DEPTH, MIX_WIDTH, D_MODEL), MIX_WIDTH ** -0.5 * DN_BETA),
        "ln1_g": 1.0 + nrm(ks[14], (DEPTH, D_MODEL), 0.01),
        "ln1_b": nrm(ks[15], (DEPTH, D_MODEL), 0.01),
        "w_ffn_gate": nrm(ks[16], (DEPTH, D_MODEL, D_FF), D_MODEL ** -0.5),
        "w_ffn_up": nrm(ks[17], (DEPTH, D_MODEL, D_FF), D_MODEL ** -0.5),
        "w_ffn_down": nrm(ks[18], (DEPTH, D_FF, D_MODEL), D_FF ** -0.5 * DN_BETA),
        "ln2_g": 1.0 + nrm(ks[19], (DEPTH, D_MODEL), 0.01),
        "ln2_b": nrm(ks[20], (DEPTH, D_MODEL), 0.01),
    }


def reference(x, w_in, conv_w, conv_b, lru_wa, lru_ba, lru_wx, lru_bx, lru_lambda,
              gla_w_gate2, gla_b_gate, gla_norm_g, rel_bias, w_out, ln1_g, ln1_b,
              w_ffn_gate, w_ffn_up, w_ffn_down, ln2_g, ln2_b):
    for l in range(DEPTH):
        proj = x @ w_in[l]
        (lru_x, lru_g, gq, gk, gv, gg, g_lr, dq, dk, dv, iq, ik, iw) = jnp.split(proj, IN_OFFSETS, axis=-1)
        y_lru = rg_lru_group(lru_x, lru_g, conv_w[l], conv_b[l], lru_wa[l], lru_ba[l],
                             lru_wx[l], lru_bx[l], lru_lambda[l])
        y_gla = gla_group(gq, gk, gv, gg, g_lr, gla_w_gate2[l], gla_b_gate[l], gla_norm_g[l])
        y_dsa = dsa_group(dq, dk, dv, iq, ik, iw, rel_bias)
        mix = jnp.concatenate([y_lru, y_gla, y_dsa], axis=-1) @ w_out[l]
        x = layer_norm(DN_ALPHA * x + mix, ln1_g[l], ln1_b[l])
        h = jax.nn.silu(x @ w_ffn_gate[l]) * (x @ w_ffn_up[l])
        x = layer_norm(DN_ALPHA * x + h @ w_ffn_down[l], ln2_g[l], ln2_b[l])
    return x
```

```python
import functools
import math

import jax
import jax.numpy as jnp
import numpy as np
from jax import lax
from jax.experimental import pallas as pl
from jax.experimental.pallas import tpu as pltpu

F32 = jnp.float32
BF16 = jnp.bfloat16
I32 = jnp.int32

D_MODEL = 1024
DEPTH = 2
LRU_WIDTH = 256
LRU_BLOCKS = 4
LRU_BLOCK = LRU_WIDTH // LRU_BLOCKS
CONV_WIDTH = 4
LRU_C = 8.0
GLA_HEADS = 6
GLA_DK = 64
GLA_DV = 64
GLA_GATE_RANK = 16
GLA_GATE_TAU = 16.0
GLA_CHUNK = 64
GLA_WIDTH = GLA_HEADS * GLA_DV
DSA_HEADS = 6
DSA_HEAD_DIM = 64
DSA_WIDTH = DSA_HEADS * DSA_HEAD_DIM
IDX_HEADS = 4
IDX_DIM = 32
IDX_TOPK_MAX = 256
REL_BUCKETS = 32
REL_MAX_DIST = 128
D_FF = 2816
DN_ALPHA = (2.0 * DEPTH) ** 0.25
LN_EPS = 1e-5

IN_SIZES = (LRU_WIDTH, LRU_WIDTH, GLA_WIDTH, GLA_WIDTH, GLA_WIDTH, GLA_WIDTH, GLA_GATE_RANK,
            DSA_WIDTH, DSA_WIDTH, DSA_WIDTH, IDX_HEADS * IDX_DIM, IDX_DIM, IDX_HEADS)
IN_OFFSETS = [0] + [int(o) for o in np.cumsum(IN_SIZES)]

LANES = 128
SUBLANES = 8
VMEM_LIMIT_BYTES = 56 * 1024 * 1024

LRU_COLS = 2 * LRU_WIDTH
GLA_COLS = 4 * GLA_WIDTH + LANES
PA_LRU = 0
PA_GLA = PA_LRU + LRU_COLS
PA_DQ = PA_GLA + GLA_COLS
PA_DV = PA_DQ + DSA_WIDTH
PA_IQ = PA_DV + DSA_WIDTH
PA_IW = PA_IQ + LANES
PA_END = PA_IW + LANES
PT_ROWS = DSA_WIDTH + IDX_DIM

Q_BLOCK = 128
KEY_CHUNK = 512
INT_MIN = -2147483648
NEG_BIG = -1e30


def _nt_dot(a, b):
    return lax.dot_general(a, b, (((1,), (1,)), ((), ())), preferred_element_type=F32)


def _tn_dot(a, b):
    return lax.dot_general(a, b, (((0,), (0,)), ((), ())), preferred_element_type=F32)


def _softplus(z):
    return jnp.maximum(z, 0.0) + jnp.log(1.0 + jnp.exp(-jnp.abs(z)))


def _sigmoid(z):
    return 1.0 / (1.0 + jnp.exp(-z))


def _proj_kernel(x_ref, wa_ref, wt_ref, lru_ref, gla_ref, q_ref, v_ref, iq_ref, iw_ref, kt_ref, kit_ref):
    xb = x_ref[...].astype(BF16)

    def mm(c0, c1):
        return jnp.dot(xb, wa_ref[:, c0:c1], preferred_element_type=F32)

    lru_ref[...] = mm(PA_LRU, PA_GLA)
    gla_ref[...] = mm(PA_GLA, PA_DQ)
    q_ref[...] = mm(PA_DQ, PA_DV).astype(BF16)
    v_ref[...] = mm(PA_DV, PA_IQ).astype(BF16)
    iq_ref[...] = mm(PA_IQ, PA_IW).astype(BF16)
    iw_ref[...] = mm(PA_IW, PA_END)
    t = _nt_dot(wt_ref[...], xb)
    kt_ref[...] = t[:DSA_WIDTH].astype(BF16)
    kit_ref[...] = t[DSA_WIDTH:PT_ROWS].astype(BF16)


def _proj(x, wa, wt, *, tm):
    B, S, D = x.shape
    grid = (B, S // tm)
    row = lambda b, m: (b, m, 0)
    col = lambda b, m: (b, 0, m)
    const = lambda b, m: (0, 0)
    out_shape = (
        jax.ShapeDtypeStruct((B, S, LRU_COLS), F32),
        jax.ShapeDtypeStruct((B, S, GLA_COLS), F32),
        jax.ShapeDtypeStruct((B, S, DSA_WIDTH), BF16),
        jax.ShapeDtypeStruct((B, S, DSA_WIDTH), BF16),
        jax.ShapeDtypeStruct((B, S, LANES), BF16),
        jax.ShapeDtypeStruct((B, S, LANES), F32),
        jax.ShapeDtypeStruct((B, DSA_WIDTH, S), BF16),
        jax.ShapeDtypeStruct((B, IDX_DIM, S), BF16),
    )
    out_specs = (
        pl.BlockSpec((None, tm, LRU_COLS), row),
        pl.BlockSpec((None, tm, GLA_COLS), row),
        pl.BlockSpec((None, tm, DSA_WIDTH), row),
        pl.BlockSpec((None, tm, DSA_WIDTH), row),
        pl.BlockSpec((None, tm, LANES), row),
        pl.BlockSpec((None, tm, LANES), row),
        pl.BlockSpec((None, DSA_WIDTH, tm), col),
        pl.BlockSpec((None, IDX_DIM, tm), col),
    )
    return pl.pallas_call(
        _proj_kernel,
        grid=grid,
        in_specs=[pl.BlockSpec((None, tm, D), row),
                  pl.BlockSpec((D, PA_END), const),
                  pl.BlockSpec((PT_ROWS, D), const)],
        out_specs=out_specs,
        out_shape=out_shape,
        compiler_params=pltpu.CompilerParams(
            dimension_semantics=("parallel", "parallel"), vmem_limit_bytes=VMEM_LIMIT_BYTES),
        name="in_proj",
    )(x, wa, wt)


def _shift_rows(x, d, fill, row):
    return jnp.where(row >= d, pltpu.roll(x, d, 0), fill)


def _lru_kernel(p_ref, cw_ref, cb_ref, wa_ref, ba_ref, wx_ref, bx_ref, lam_ref, y_ref, xprev_ref, h_ref):
    tl = p_ref.shape[0]

    @pl.when(pl.program_id(1) == 0)
    def _():
        xprev_ref[...] = jnp.zeros_like(xprev_ref)
        h_ref[...] = jnp.zeros_like(h_ref)

    xb = p_ref[:, 0:LRU_WIDTH]
    gb = p_ref[:, LRU_WIDTH:2 * LRU_WIDTH]
    prev = xprev_ref[...]
    row8 = lax.broadcasted_iota(I32, (SUBLANES, LRU_WIDTH), 0)
    cw = cw_ref[...]
    xc = cb_ref[...] + cw[CONV_WIDTH - 1:CONV_WIDTH] * xb
    for d in range(1, CONV_WIDTH):
        r = pltpu.roll(xb, d, 0)
        top = jnp.where(row8 < d, pltpu.roll(prev, d, 0), r[0:SUBLANES])
        r = jnp.concatenate([top, r[SUBLANES:]], axis=0)
        xc = xc + cw[CONV_WIDTH - 1 - d:CONV_WIDTH - d] * r
    xprev_ref[...] = xb[tl - SUBLANES:tl]

    xcb = xc.astype(BF16)
    r_gate = _sigmoid(jnp.dot(xcb, wa_ref[...], preferred_element_type=F32) + ba_ref[...])
    i_gate = _sigmoid(jnp.dot(xcb, wx_ref[...], preferred_element_type=F32) + bx_ref[...])
    log_a = (-LRU_C) * r_gate * _softplus(-lam_ref[...])
    a = jnp.exp(log_a)
    u = jnp.sqrt(1.0 - a * a) * (i_gate * xc)

    row = lax.broadcasted_iota(I32, (tl, LRU_WIDTH), 0)
    d = 1
    while d < tl:
        a_s = _shift_rows(a, d, 1.0, row)
        u_s = _shift_rows(u, d, 0.0, row)
        u = u + a * u_s
        a = a * a_s
        d *= 2
    h = u + a * h_ref[SUBLANES - 1:SUBLANES, :]
    h_ref[...] = h[tl - SUBLANES:tl]
    y_ref[...] = h * jax.nn.gelu(gb)


def _lru(p_lru, cw, cb, wa, ba, wx, bx, lam, *, tl):
    B, S, _ = p_lru.shape
    const = lambda b, j: (0, 0)
    vec = pl.BlockSpec((1, LRU_WIDTH), const)
    mat = pl.BlockSpec((LRU_WIDTH, LRU_WIDTH), const)
    return pl.pallas_call(
        _lru_kernel,
        grid=(B, S // tl),
        in_specs=[pl.BlockSpec((None, tl, LRU_COLS), lambda b, j: (b, j, 0)),
                  pl.BlockSpec((CONV_WIDTH, LRU_WIDTH), const), vec, mat, vec, mat, vec, vec],
        out_specs=pl.BlockSpec((None, tl, LRU_WIDTH), lambda b, j: (b, j, 0)),
        out_shape=jax.ShapeDtypeStruct((B, S, LRU_WIDTH), F32),
        scratch_shapes=[pltpu.VMEM((SUBLANES, LRU_WIDTH), F32), pltpu.VMEM((SUBLANES, LRU_WIDTH), F32)],
        compiler_params=pltpu.CompilerParams(
            dimension_semantics=("parallel", "arbitrary"), vmem_limit_bytes=VMEM_LIMIT_BYTES),
        name="rg_lru",
    )(p_lru, cw, cb, wa, ba, wx, bx, lam)


def _gla_kernel(q_ref, k_ref, v_ref, g_ref, glr_ref, w2_ref, bg_ref, ng_ref, tri_ref, mean_ref,
                y_ref, st_ref):
    tg = q_ref.shape[0]
    C = GLA_CHUNK
    hi = lax.Precision.HIGHEST

    @pl.when(pl.program_id(1) == 0)
    def _():
        st_ref[...] = jnp.zeros_like(st_ref)

    z = jnp.dot(glr_ref[...], w2_ref[...], preferred_element_type=F32, precision=hi) + bg_ref[...]
    log_alpha = -_softplus(-z) * (1.0 / GLA_GATE_TAU)

    lane = lax.broadcasted_iota(I32, (C, LANES), 1)
    first = lane < GLA_DK
    rr = lax.broadcasted_iota(I32, (C, C), 0)
    cc = lax.broadcasted_iota(I32, (C, C), 1)
    causal = cc <= rr
    r2 = lax.broadcasted_iota(I32, (LANES, LANES), 0)
    c2 = lax.broadcasted_iota(I32, (LANES, LANES), 1)
    same_head = (r2 < GLA_DV) == (c2 < GLA_DK)
    tri = tri_ref[...]

    for c in range(tg // C):
        rows = slice(c * C, (c + 1) * C)
        bcum = jnp.dot(tri, log_alpha[rows], preferred_element_type=F32, precision=hi)
        blast = bcum[C - 1:C]
        kf = k_ref[rows, :]
        q_dec = q_ref[rows, :] * (GLA_DK ** -0.5) * jnp.exp(bcum)
        k_inv = kf * jnp.exp(-bcum)
        k_end = kf * jnp.exp(blast - bcum)
        decay = jnp.exp(blast)
        vf = v_ref[rows, :]
        outs = []
        for p in range(GLA_HEADS // 2):
            cs = slice(p * LANES, (p + 1) * LANES)
            qd, ki, ke, vp = q_dec[:, cs], k_inv[:, cs], k_end[:, cs], vf[:, cs]
            kib = ki.astype(BF16)
            vpb = vp.astype(BF16)
            att0 = _nt_dot(jnp.where(first, qd, 0.0).astype(BF16), kib)
            att1 = _nt_dot(jnp.where(first, 0.0, qd).astype(BF16), kib)
            att0 = jnp.where(causal, att0, 0.0).astype(BF16)
            att1 = jnp.where(causal, att1, 0.0).astype(BF16)
            o_intra = jnp.where(first,
                                jnp.dot(att0, vpb, preferred_element_type=F32),
                                jnp.dot(att1, vpb, preferred_element_type=F32))
            st = st_ref[p]
            o_inter = _nt_dot(qd.astype(BF16), st.astype(BF16))
            u_t = _tn_dot(vpb, ke.astype(BF16))
            st_ref[p] = st * decay[:, cs] + jnp.where(same_head, u_t, 0.0)
            outs.append(o_intra + o_inter)
        o = jnp.concatenate(outs, axis=1)
        ms = jnp.dot(o * o, mean_ref[...], preferred_element_type=F32, precision=hi)
        o = o * lax.rsqrt(ms + 1e-6) * ng_ref[...]
        gf = g_ref[rows, :]
        y_ref[rows, :] = o * (gf * _sigmoid(gf))


def _gla(p_gla, w2, bg, ng, *, tg):
    B, S, _ = p_gla.shape
    W = GLA_WIDTH
    const = lambda b, j: (0, 0)
    tri = jnp.tril(jnp.ones((GLA_CHUNK, GLA_CHUNK), F32))
    head = jnp.arange(W) // GLA_DV
    mean_blk = (head[:, None] == head[None, :]).astype(F32) / GLA_DV

    def colblk(c):
        return pl.BlockSpec((None, tg, W), lambda b, j: (b, j, c))

    return pl.pallas_call(
        _gla_kernel,
        grid=(B, S // tg),
        in_specs=[colblk(0), colblk(1), colblk(2), colblk(3),
                  pl.BlockSpec((None, tg, LANES), lambda b, j: (b, j, 4 * W // LANES)),
                  pl.BlockSpec((LANES, W), const),
                  pl.BlockSpec((1, W), const), pl.BlockSpec((1, W), const),
                  pl.BlockSpec((GLA_CHUNK, GLA_CHUNK), const),
                  pl.BlockSpec((W, W), const)],
        out_specs=pl.BlockSpec((None, tg, W), lambda b, j: (b, j, 0)),
        out_shape=jax.ShapeDtypeStruct((B, S, W), F32),
        scratch_shapes=[pltpu.VMEM((GLA_HEADS // 2, LANES, LANES), F32)],
        compiler_params=pltpu.CompilerParams(
            dimension_semantics=("parallel", "arbitrary"), vmem_limit_bytes=VMEM_LIMIT_BYTES),
        name="gla",
    )(p_gla, p_gla, p_gla, p_gla, p_gla, w2, bg, ng, tri, mean_blk)


def _relbias_kernel(rb_ref, tb_ref):
    r = lax.broadcasted_iota(I32, (Q_BLOCK, LANES), 0)
    c = lax.broadcasted_iota(I32, (Q_BLOCK, LANES), 1)
    max_exact = REL_BUCKETS // 2
    for delta in range(2):
        n = jnp.maximum(delta * LANES + r - c, 0)
        nf = jnp.maximum(n, 1).astype(F32)
        large = max_exact + (jnp.log(nf / max_exact) / math.log(REL_MAX_DIST / max_exact)
                             * (REL_BUCKETS - max_exact)).astype(I32)
        large = jnp.minimum(large, REL_BUCKETS - 1)
        bucket = jnp.where(n < max_exact, n, large)
        for h in range(DSA_HEADS):
            acc = jnp.zeros((Q_BLOCK, LANES), F32)
            for b in range(REL_BUCKETS):
                acc = jnp.where(bucket == b, rb_ref[b, h], acc)
            tb_ref[h, delta] = acc - rb_ref[REL_BUCKETS - 1, h]


def _relbias_tiles(rel_bias):
    return pl.pallas_call(
        _relbias_kernel,
        in_specs=[pl.BlockSpec(memory_space=pltpu.SMEM)],
        out_specs=pl.BlockSpec(memory_space=pltpu.VMEM),
        out_shape=jax.ShapeDtypeStruct((DSA_HEADS, 2, Q_BLOCK, LANES), F32),
        name="rel_bias_tiles",
    )(rel_bias)


def _dsa_kernel(iq_ref, iw_ref, q_ref, kit_ref, kt_ref, v_ref, tb_ref, utri_ref, y_ref,
                key_ref, m_ref, l_ref, acc_ref, *, top_k):
    i = pl.program_id(1)
    TQ, CK, L = Q_BLOCK, KEY_CHUNK, LANES
    n_chunks = (i * TQ + TQ + CK - 1) // CK
    row_t = i * TQ + lax.broadcasted_iota(I32, (TQ, CK), 0)
    lane_ck = lax.broadcasted_iota(I32, (TQ, CK), 1)
    neg_inf_key = I32(-2139095041)

    iq = iq_ref[...]
    iq_h = [iq[:, h * IDX_DIM:(h + 1) * IDX_DIM] for h in range(IDX_HEADS)]
    scale = (IDX_DIM ** -0.5) * (IDX_HEADS ** -0.5)
    w_h = [jnp.broadcast_to(iw_ref[:, h:h + 1] * scale, (TQ, CK)) for h in range(IDX_HEADS)]

    def score_chunk(c, carry):
        off = pl.multiple_of(c * CK, CK)
        kc = kit_ref[:, pl.ds(off, CK)]
        sc = jnp.zeros((TQ, CK), F32)
        for h in range(IDX_HEADS):
            z = jnp.dot(iq_h[h], kc, preferred_element_type=F32)
            sc = sc + jnp.maximum(z, 0.0) * w_h[h]
        sc = jnp.where(off + lane_ck <= row_t, sc + 0.0, -jnp.inf)
        bits = pltpu.bitcast(sc, I32)
        key_ref[:, pl.ds(off, CK)] = bits ^ ((bits >> 31) & I32(0x7FFFFFFF))
        return carry

    lax.fori_loop(0, n_chunks, score_chunk, 0)

    def count(pred):
        def body(c, acc):
            off = pl.multiple_of(c * CK, CK)
            m = pred(key_ref[:, pl.ds(off, CK)]).astype(I32)
            for s in range(CK // L):
                acc = acc + m[:, s * L:(s + 1) * L]
            return acc
        acc = lax.fori_loop(0, n_chunks, body, jnp.zeros((TQ, L), I32))
        return jnp.sum(acc, axis=1, keepdims=True)

    def bit_step(b, thr):
        cand = thr + lax.shift_left(I32(1), I32(31) - b)
        cnt = count(lambda kk: kk >= cand)
        return jnp.where(cnt >= top_k, cand, thr)

    thr = lax.fori_loop(0, 32, bit_step, jnp.full((TQ, 1), INT_MIN, I32))
    need = (top_k - count(lambda kk: kk > thr)).astype(F32)

    utri = utri_ref[...]

    def mask_chunk(c, run):
        off = pl.multiple_of(c * CK, CK)
        for s in range(CK // L):
            kk = key_ref[:, pl.ds(off + s * L, L)]
            eq = kk == thr
            pref = jnp.dot(jnp.where(eq, 1.0, 0.0).astype(BF16), utri, preferred_element_type=F32)
            sel = (kk > thr) | (eq & (run + pref <= need))
            sel = sel & (kk > neg_inf_key)
            key_ref[:, pl.ds(off + s * L, L)] = pltpu.bitcast(jnp.where(sel, 0.0, -jnp.inf).astype(F32), I32)
            run = run + pref[:, L - 1:L]
        return run

    lax.fori_loop(0, n_chunks, mask_chunk, jnp.zeros((TQ, 1), F32))

    m_ref[...] = jnp.full(m_ref.shape, NEG_BIG, F32)
    l_ref[...] = jnp.zeros(l_ref.shape, F32)
    acc_ref[...] = jnp.zeros(acc_ref.shape, F32)
    q = q_ref[...]
    q_h = [q[:, h * DSA_HEAD_DIM:(h + 1) * DSA_HEAD_DIM] for h in range(DSA_HEADS)]

    def attend(off, width, delta):
        mb = pltpu.bitcast(key_ref[:, pl.ds(off, width)], F32)
        for h in range(DSA_HEADS):
            kt = kt_ref[h * DSA_HEAD_DIM:(h + 1) * DSA_HEAD_DIM, pl.ds(off, width)]
            s = jnp.dot(q_h[h], kt, preferred_element_type=F32) + mb
            if delta is not None:
                s = s + tb_ref[h, delta]
            m_old = m_ref[h]
            m_new = jnp.maximum(m_old, jnp.max(s, axis=1, keepdims=True))
            alpha = jnp.exp(m_old - m_new)
            ps = [jnp.exp(s[:, j * L:(j + 1) * L] - m_new) for j in range(width // L)]
            p = ps[0] if len(ps) == 1 else jnp.concatenate(ps, axis=1)
            pair = (h // 2) * L
            pv = jnp.dot(p.astype(BF16), v_ref[pl.ds(off, width), pair:pair + L],
                         preferred_element_type=F32)
            l_ref[h] = l_ref[h] * alpha + functools.reduce(lambda x, y: x + y, ps)
            acc_ref[h] = acc_ref[h] * alpha + pv
            m_ref[h] = m_new

    def far_chunk(c, carry):
        attend(pl.multiple_of(c * 2 * L, 2 * L), 2 * L, None)
        return carry

    lax.fori_loop(0, (i - 1) >> 1, far_chunk, 0)

    @pl.when((i >= 2) & ((i & 1) == 0))
    def _():
        attend(pl.multiple_of((i - 2) * L, L), L, None)

    @pl.when(i >= 1)
    def _():
        attend(pl.multiple_of((i - 1) * L, L), L, 1)

    attend(pl.multiple_of(i * L, L), L, 0)

    lane = lax.broadcasted_iota(I32, (TQ, L), 1)
    for p in range(DSA_HEADS // 2):
        o0 = acc_ref[2 * p] / jnp.sum(l_ref[2 * p], axis=1, keepdims=True)
        o1 = acc_ref[2 * p + 1] / jnp.sum(l_ref[2 * p + 1], axis=1, keepdims=True)
        y_ref[:, p * L:(p + 1) * L] = jnp.where(lane < DSA_HEAD_DIM, o0, o1)


def _dsa(iq, iw, q, kit, kt, v, tb, *, top_k):
    B, S, W = q.shape
    utri = jnp.triu(jnp.ones((LANES, LANES), F32)).astype(BF16)
    qrow = lambda b, i: (b, i, 0)
    whole = lambda b, i: (b, 0, 0)
    one = pl.Buffered(1)
    return pl.pallas_call(
        functools.partial(_dsa_kernel, top_k=top_k),
        grid=(B, S // Q_BLOCK),
        in_specs=[pl.BlockSpec((None, Q_BLOCK, LANES), qrow),
                  pl.BlockSpec((None, Q_BLOCK, LANES), qrow),
                  pl.BlockSpec((None, Q_BLOCK, W), qrow),
                  pl.BlockSpec((None, IDX_DIM, S), whole, pipeline_mode=one),
                  pl.BlockSpec((None, W, S), whole, pipeline_mode=one),
                  pl.BlockSpec((None, S, W), whole, pipeline_mode=one),
                  pl.BlockSpec((DSA_HEADS, 2, Q_BLOCK, LANES), lambda b, i: (0, 0, 0, 0)),
                  pl.BlockSpec((LANES, LANES), lambda b, i: (0, 0))],
        out_specs=pl.BlockSpec((None, Q_BLOCK, W), qrow),
        out_shape=jax.ShapeDtypeStruct((B, S, W), F32),
        scratch_shapes=[pltpu.VMEM((Q_BLOCK, S), I32),
                        pltpu.VMEM((DSA_HEADS, Q_BLOCK, LANES), F32),
                        pltpu.VMEM((DSA_HEADS, Q_BLOCK, LANES), F32),
                        pltpu.VMEM((DSA_HEADS, Q_BLOCK, LANES), F32)],
        compiler_params=pltpu.CompilerParams(
            dimension_semantics=("parallel", "arbitrary"), vmem_limit_bytes=VMEM_LIMIT_BYTES),
        name="dsa",
    )(iq, iw, q, kit, kt, v, tb, utri)


def _layer_norm(z, g, b):
    mu = jnp.mean(z, axis=-1, keepdims=True)
    zc = z - mu
    var = jnp.mean(zc * zc, axis=-1, keepdims=True)
    return zc * lax.rsqrt(var + LN_EPS) * g + b


def _mix_kernel(x_ref, ya_ref, yb_ref, yc_ref, wa_ref, wb_ref, wc_ref, g_ref, b_ref, o_ref):
    mix = jnp.dot(ya_ref[...].astype(BF16), wa_ref[...], preferred_element_type=F32)
    mix = mix + jnp.dot(yb_ref[...].astype(BF16), wb_ref[...], preferred_element_type=F32)
    mix = mix + jnp.dot(yc_ref[...].astype(BF16), wc_ref[...], preferred_element_type=F32)
    o_ref[...] = _layer_norm(DN_ALPHA * x_ref[...] + mix, g_ref[...], b_ref[...])


def _mix(x, ya, yb, yc, wa, wb, wc, g, b, *, tm):
    T, D = x.shape
    row = lambda m: (m, 0)
    const = lambda m: (0, 0)
    return pl.pallas_call(
        _mix_kernel,
        grid=(T // tm,),
        in_specs=[pl.BlockSpec((tm, D), row),
                  pl.BlockSpec((tm, ya.shape[1]), row), pl.BlockSpec((tm, yb.shape[1]), row),
                  pl.BlockSpec((tm, yc.shape[1]), row),
                  pl.BlockSpec(wa.shape, const), pl.BlockSpec(wb.shape, const), pl.BlockSpec(wc.shape, const),
                  pl.BlockSpec((1, D), const), pl.BlockSpec((1, D), const)],
        out_specs=pl.BlockSpec((tm, D), row),
        out_shape=jax.ShapeDtypeStruct((T, D), F32),
        compiler_params=pltpu.CompilerParams(
            dimension_semantics=("parallel",), vmem_limit_bytes=VMEM_LIMIT_BYTES),
        name="out_proj_ln",
    )(x, ya, yb, yc, wa, wb, wc, g, b)


def _ffn_kernel(x_ref, wg_ref, wu_ref, wd_ref, g_ref, b_ref, o_ref, xb_ref, acc_ref):
    f = pl.program_id(1)

    @pl.when(f == 0)
    def _():
        xb_ref[...] = x_ref[...].astype(BF16)
        acc_ref[...] = jnp.zeros_like(acc_ref)

    xb = xb_ref[...]
    gate = jnp.dot(xb, wg_ref[...], preferred_element_type=F32)
    up = jnp.dot(xb, wu_ref[...], preferred_element_type=F32)
    h = (gate * _sigmoid(gate)) * up
    acc_ref[...] += jnp.dot(h.astype(BF16), wd_ref[...], preferred_element_type=F32)

    @pl.when(f == pl.num_programs(1) - 1)
    def _():
        o_ref[...] = _layer_norm(DN_ALPHA * x_ref[...] + acc_ref[...], g_ref[...], b_ref[...])


def _ffn(x, wg, wu, wd, g, b, *, tm, tf):
    T, D = x.shape
    FF = wg.shape[1]
    row = lambda m, f: (m, 0)
    const = lambda m, f: (0, 0)
    return pl.pallas_call(
        _ffn_kernel,
        grid=(T // tm, FF // tf),
        in_specs=[pl.BlockSpec((tm, D), row),
                  pl.BlockSpec((D, tf), lambda m, f: (0, f)),
                  pl.BlockSpec((D, tf), lambda m, f: (0, f)),
                  pl.BlockSpec((tf, D), lambda m, f: (f, 0)),
                  pl.BlockSpec((1, D), const), pl.BlockSpec((1, D), const)],
        out_specs=pl.BlockSpec((tm, D), row),
        out_shape=jax.ShapeDtypeStruct((T, D), F32),
        scratch_shapes=[pltpu.VMEM((tm, D), BF16), pltpu.VMEM((tm, D), F32)],
        compiler_params=pltpu.CompilerParams(
            dimension_semantics=("parallel", "arbitrary"), vmem_limit_bytes=VMEM_LIMIT_BYTES),
        name="ffn_ln",
    )(x, wg, wu, wd, g, b)


def _split_w_in(w):
    parts = [w[:, IN_OFFSETS[j]:IN_OFFSETS[j + 1]] for j in range(len(IN_SIZES))]
    lru_x, lru_g, gq, gk, gv, gg, g_lr, dq, dk, dv, iq, ik, iw = parts
    D = w.shape[0]
    pad = lambda a, n: jnp.pad(a, ((0, 0), (0, n - a.shape[1])))
    wa = jnp.concatenate([lru_x, lru_g, gq, gk, gv, gg, pad(g_lr, LANES),
                          dq * (DSA_HEAD_DIM ** -0.5), dv, iq, pad(iw, LANES)], axis=1)
    wt = jnp.concatenate([dk, ik], axis=1).T
    assert wa.shape == (D, PA_END) and wt.shape == (PT_ROWS, D)
    return wa.astype(BF16), wt.astype(BF16)


def _block_diag(w):
    n, k, _ = w.shape
    eye = jnp.eye(n, dtype=w.dtype)
    return (eye[:, None, :, None] * w[:, :, None, :]).reshape(n * k, n * k)


def kernel(x, w_in, conv_w, conv_b, lru_wa, lru_ba, lru_wx, lru_bx, lru_lambda, gla_w_gate2, gla_b_gate,
           gla_norm_g, rel_bias, w_out, ln1_g, ln1_b, w_ffn_gate, w_ffn_up, w_ffn_down, ln2_g, ln2_b):
    B, S, D = x.shape
    T = B * S
    top_k = min(IDX_TOPK_MAX, S // 4)
    assert S % KEY_CHUNK == 0 and D == D_MODEL
    tm = min(512, S)
    tb = _relbias_tiles(rel_bias)
    row = lambda a: a.reshape(1, -1)
    for l in range(w_in.shape[0]):
        wa, wt = _split_w_in(w_in[l])
        p_lru, p_gla, dq, dv, iq, iw, kt, kit = _proj(x, wa, wt, tm=tm)
        y_lru = _lru(p_lru, conv_w[l], row(conv_b[l]),
                     _block_diag(lru_wa[l]).astype(BF16), row(lru_ba[l]),
                     _block_diag(lru_wx[l]).astype(BF16), row(lru_bx[l]), row(lru_lambda[l]), tl=tm)
        w2 = jnp.pad(gla_w_gate2[l], ((0, LANES - GLA_GATE_RANK), (0, 0)))
        y_gla = _gla(p_gla, w2, row(gla_b_gate[l]), row(gla_norm_g[l]), tg=min(256, S))
        y_dsa = _dsa(iq, iw, dq, kit, kt, dv, tb, top_k=top_k)
        wo = w_out[l].astype(BF16)
        x1 = _mix(x.reshape(T, D), y_lru.reshape(T, -1), y_gla.reshape(T, -1), y_dsa.reshape(T, -1),
                  wo[:LRU_WIDTH], wo[LRU_WIDTH:LRU_WIDTH + GLA_WIDTH], wo[LRU_WIDTH + GLA_WIDTH:],
                  row(ln1_g[l]), row(ln1_b[l]), tm=tm)
        x2 = _ffn(x1, w_ffn_gate[l].astype(BF16), w_ffn_up[l].astype(BF16), w_ffn_down[l].astype(BF16),
                  row(ln2_g[l]), row(ln2_b[l]), tm=tm, tf=D_FF // 2)
        x = x2.reshape(B, S, D)
    return x
```

```python
import functools
import math

import jax
import jax.numpy as jnp
import numpy as np
from jax import lax
from jax.experimental import pallas as pl
from jax.experimental.pallas import tpu as pltpu

F32 = jnp.float32
BF16 = jnp.bfloat16
I32 = jnp.int32

D_MODEL = 1024
DEPTH = 2
LRU_WIDTH = 256
LRU_BLOCKS = 4
LRU_BLOCK = LRU_WIDTH // LRU_BLOCKS
CONV_WIDTH = 4
LRU_C = 8.0
GLA_HEADS = 6
GLA_DK = 64
GLA_DV = 64
GLA_GATE_RANK = 16
GLA_GATE_TAU = 16.0
GLA_CHUNK = 64
GLA_WIDTH = GLA_HEADS * GLA_DV
DSA_HEADS = 6
DSA_HEAD_DIM = 64
DSA_WIDTH = DSA_HEADS * DSA_HEAD_DIM
IDX_HEADS = 4
IDX_DIM = 32
IDX_TOPK_MAX = 256
REL_BUCKETS = 32
REL_MAX_DIST = 128
D_FF = 2816
DN_ALPHA = (2.0 * DEPTH) ** 0.25
LN_EPS = 1e-5

IN_SIZES = (LRU_WIDTH, LRU_WIDTH, GLA_WIDTH, GLA_WIDTH, GLA_WIDTH, GLA_WIDTH, GLA_GATE_RANK,
            DSA_WIDTH, DSA_WIDTH, DSA_WIDTH, IDX_HEADS * IDX_DIM, IDX_DIM, IDX_HEADS)
IN_OFFSETS = [0] + [int(o) for o in np.cumsum(IN_SIZES)]

LANES = 128
SUBLANES = 8
VMEM_LIMIT_BYTES = 56 * 1024 * 1024

LRU_COLS = 2 * LRU_WIDTH
GLA_COLS = 4 * GLA_WIDTH + LANES
PA_LRU = 0
PA_GLA = PA_LRU + LRU_COLS
PA_DQ = PA_GLA + GLA_COLS
PA_DV = PA_DQ + DSA_WIDTH
PA_IQ = PA_DV + DSA_WIDTH
PA_IW = PA_IQ + LANES
PA_END = PA_IW + LANES
PT_ROWS = DSA_WIDTH + IDX_DIM

Q_BLOCK = 128
KEY_CHUNK = 512
HALF_BITS = 16
INT_MIN = -2147483648
NEG_INF_KEY = -2139095041
NEG_BIG = -1e30
LOG2E = 1.4426950408889634
REL_SLABS = 3
FAST_SOFTMAX_MAX_SHIFT = 120.0


def _nt_dot(a, b):
    return lax.dot_general(a, b, (((1,), (1,)), ((), ())), preferred_element_type=F32)


def _tn_dot(a, b):
    return lax.dot_general(a, b, (((0,), (0,)), ((), ())), preferred_element_type=F32)


def _softplus(z):
    return jnp.maximum(z, 0.0) + jnp.log(1.0 + jnp.exp(-jnp.abs(z)))


def _sigmoid(z):
    return 1.0 / (1.0 + jnp.exp(-z))


def _proj_kernel(x_ref, wa_ref, wt_ref, lru_ref, gla_ref, q_ref, v_ref, iq_ref, iw_ref, kt_ref, kit_ref):
    xb = x_ref[...].astype(BF16)

    def mm(c0, c1):
        return jnp.dot(xb, wa_ref[:, c0:c1], preferred_element_type=F32)

    lru_ref[...] = mm(PA_LRU, PA_GLA)
    gla_ref[...] = mm(PA_GLA, PA_DQ)
    q_ref[...] = mm(PA_DQ, PA_DV)
    v_ref[...] = mm(PA_DV, PA_IQ).astype(BF16)
    iq_ref[...] = mm(PA_IQ, PA_IW).astype(BF16)
    iw_ref[...] = mm(PA_IW, PA_END)
    t = _nt_dot(wt_ref[...], xb)
    kt_ref[...] = t[:DSA_WIDTH].astype(BF16)
    kit_ref[...] = t[DSA_WIDTH:PT_ROWS].astype(BF16)


def _proj(x, wa, wt, *, tm):
    B, S, D = x.shape
    grid = (B, S // tm)
    row = lambda b, m: (b, m, 0)
    col = lambda b, m: (b, 0, m)
    const = lambda b, m: (0, 0)
    out_shape = (
        jax.ShapeDtypeStruct((B, S, LRU_COLS), F32),
        jax.ShapeDtypeStruct((B, S, GLA_COLS), F32),
        jax.ShapeDtypeStruct((B, S, DSA_WIDTH), F32),
        jax.ShapeDtypeStruct((B, S, DSA_WIDTH), BF16),
        jax.ShapeDtypeStruct((B, S, LANES), BF16),
        jax.ShapeDtypeStruct((B, S, LANES), F32),
        jax.ShapeDtypeStruct((B, DSA_WIDTH, S), BF16),
        jax.ShapeDtypeStruct((B, IDX_DIM, S), BF16),
    )
    out_specs = (
        pl.BlockSpec((None, tm, LRU_COLS), row),
        pl.BlockSpec((None, tm, GLA_COLS), row),
        pl.BlockSpec((None, tm, DSA_WIDTH), row),
        pl.BlockSpec((None, tm, DSA_WIDTH), row),
        pl.BlockSpec((None, tm, LANES), row),
        pl.BlockSpec((None, tm, LANES), row),
        pl.BlockSpec((None, DSA_WIDTH, tm), col),
        pl.BlockSpec((None, IDX_DIM, tm), col),
    )
    return pl.pallas_call(
        _proj_kernel,
        grid=grid,
        in_specs=[pl.BlockSpec((None, tm, D), row),
                  pl.BlockSpec((D, PA_END), const),
                  pl.BlockSpec((PT_ROWS, D), const)],
        out_specs=out_specs,
        out_shape=out_shape,
        compiler_params=pltpu.CompilerParams(
            dimension_semantics=("parallel", "parallel"), vmem_limit_bytes=VMEM_LIMIT_BYTES),
        name="in_proj",
    )(x, wa, wt)


def _shift_rows(x, d, fill, row):
    return jnp.where(row >= d, pltpu.roll(x, d, 0), fill)


def _lru_kernel(p_ref, cw_ref, cb_ref, wa_ref, ba_ref, wx_ref, bx_ref, lam_ref, y_ref, xprev_ref, h_ref):
    tl = p_ref.shape[0]

    @pl.when(pl.program_id(1) == 0)
    def _():
        xprev_ref[...] = jnp.zeros_like(xprev_ref)
        h_ref[...] = jnp.zeros_like(h_ref)

    xb = p_ref[:, 0:LRU_WIDTH]
    gb = p_ref[:, LRU_WIDTH:2 * LRU_WIDTH]
    prev = xprev_ref[...]
    row8 = lax.broadcasted_iota(I32, (SUBLANES, LRU_WIDTH), 0)
    cw = cw_ref[...]
    xc = cb_ref[...] + cw[CONV_WIDTH - 1:CONV_WIDTH] * xb
    for d in range(1, CONV_WIDTH):
        r = pltpu.roll(xb, d, 0)
        top = jnp.where(row8 < d, pltpu.roll(prev, d, 0), r[0:SUBLANES])
        r = jnp.concatenate([top, r[SUBLANES:]], axis=0)
        xc = xc + cw[CONV_WIDTH - 1 - d:CONV_WIDTH - d] * r
    xprev_ref[...] = xb[tl - SUBLANES:tl]

    xcb = xc.astype(BF16)
    r_gate = _sigmoid(jnp.dot(xcb, wa_ref[...], preferred_element_type=F32) + ba_ref[...])
    i_gate = _sigmoid(jnp.dot(xcb, wx_ref[...], preferred_element_type=F32) + bx_ref[...])
    log_a = (-LRU_C) * r_gate * _softplus(-lam_ref[...])
    a = jnp.exp(log_a)
    u = jnp.sqrt(1.0 - a * a) * (i_gate * xc)

    row = lax.broadcasted_iota(I32, (tl, LRU_WIDTH), 0)
    d = 1
    while d < tl:
        a_s = _shift_rows(a, d, 1.0, row)
        u_s = _shift_rows(u, d, 0.0, row)
        u = u + a * u_s
        a = a * a_s
        d *= 2
    h = u + a * h_ref[SUBLANES - 1:SUBLANES, :]
    h_ref[...] = h[tl - SUBLANES:tl]
    y_ref[...] = h * jax.nn.gelu(gb)


def _lru(p_lru, cw, cb, wa, ba, wx, bx, lam, *, tl):
    B, S, _ = p_lru.shape
    const = lambda b, j: (0, 0)
    vec = pl.BlockSpec((1, LRU_WIDTH), const)
    mat = pl.BlockSpec((LRU_WIDTH, LRU_WIDTH), const)
    return pl.pallas_call(
        _lru_kernel,
        grid=(B, S // tl),
        in_specs=[pl.BlockSpec((None, tl, LRU_COLS), lambda b, j: (b, j, 0)),
                  pl.BlockSpec((CONV_WIDTH, LRU_WIDTH), const), vec, mat, vec, mat, vec, vec],
        out_specs=pl.BlockSpec((None, tl, LRU_WIDTH), lambda b, j: (b, j, 0)),
        out_shape=jax.ShapeDtypeStruct((B, S, LRU_WIDTH), F32),
        scratch_shapes=[pltpu.VMEM((SUBLANES, LRU_WIDTH), F32), pltpu.VMEM((SUBLANES, LRU_WIDTH), F32)],
        compiler_params=pltpu.CompilerParams(
            dimension_semantics=("parallel", "arbitrary"), vmem_limit_bytes=VMEM_LIMIT_BYTES),
        name="rg_lru",
    )(p_lru, cw, cb, wa, ba, wx, bx, lam)


def _gla_kernel(q_ref, k_ref, v_ref, g_ref, glr_ref, w2_ref, bg_ref, ng_ref, tri_ref, mean_ref,
                y_ref, st_ref):
    tg = q_ref.shape[0]
    C = GLA_CHUNK
    hi = lax.Precision.HIGHEST

    @pl.when(pl.program_id(1) == 0)
    def _():
        st_ref[...] = jnp.zeros_like(st_ref)

    z = jnp.dot(glr_ref[...], w2_ref[...], preferred_element_type=F32, precision=hi) + bg_ref[...]
    log_alpha = -_softplus(-z) * (1.0 / GLA_GATE_TAU)

    lane = lax.broadcasted_iota(I32, (C, LANES), 1)
    first = lane < GLA_DK
    rr = lax.broadcasted_iota(I32, (C, C), 0)
    cc = lax.broadcasted_iota(I32, (C, C), 1)
    causal = cc <= rr
    r2 = lax.broadcasted_iota(I32, (LANES, LANES), 0)
    c2 = lax.broadcasted_iota(I32, (LANES, LANES), 1)
    same_head = (r2 < GLA_DV) == (c2 < GLA_DK)
    tri = tri_ref[...]

    for c in range(tg // C):
        rows = slice(c * C, (c + 1) * C)
        bcum = jnp.dot(tri, log_alpha[rows], preferred_element_type=F32, precision=hi)
        blast = bcum[C - 1:C]
        kf = k_ref[rows, :]
        q_dec = q_ref[rows, :] * (GLA_DK ** -0.5) * jnp.exp(bcum)
        k_inv = kf * jnp.exp(-bcum)
        k_end = kf * jnp.exp(blast - bcum)
        decay = jnp.exp(blast)
        vf = v_ref[rows, :]
        outs = []
        for p in range(GLA_HEADS // 2):
            cs = slice(p * LANES, (p + 1) * LANES)
            qd, ki, ke, vp = q_dec[:, cs], k_inv[:, cs], k_end[:, cs], vf[:, cs]
            kib = ki.astype(BF16)
            vpb = vp.astype(BF16)
            att0 = _nt_dot(jnp.where(first, qd, 0.0).astype(BF16), kib)
            att1 = _nt_dot(jnp.where(first, 0.0, qd).astype(BF16), kib)
            att0 = jnp.where(causal, att0, 0.0).astype(BF16)
            att1 = jnp.where(causal, att1, 0.0).astype(BF16)
            o_intra = jnp.where(first,
                                jnp.dot(att0, vpb, preferred_element_type=F32),
                                jnp.dot(att1, vpb, preferred_element_type=F32))
            st = st_ref[p]
            o_inter = _nt_dot(qd.astype(BF16), st.astype(BF16))
            u_t = _tn_dot(vpb, ke.astype(BF16))
            st_ref[p] = st * decay[:, cs] + jnp.where(same_head, u_t, 0.0)
            outs.append(o_intra + o_inter)
        o = jnp.concatenate(outs, axis=1)
        ms = jnp.dot(o * o, mean_ref[...], preferred_element_type=F32, precision=hi)
        o = o * lax.rsqrt(ms + 1e-6) * ng_ref[...]
        gf = g_ref[rows, :]
        y_ref[rows, :] = o * (gf * _sigmoid(gf))


def _gla(p_gla, w2, bg, ng, *, tg):
    B, S, _ = p_gla.shape
    W = GLA_WIDTH
    const = lambda b, j: (0, 0)
    tri = jnp.tril(jnp.ones((GLA_CHUNK, GLA_CHUNK), F32))
    head = jnp.arange(W) // GLA_DV
    mean_blk = (head[:, None] == head[None, :]).astype(F32) / GLA_DV

    def colblk(c):
        return pl.BlockSpec((None, tg, W), lambda b, j: (b, j, c))

    return pl.pallas_call(
        _gla_kernel,
        grid=(B, S // tg),
        in_specs=[colblk(0), colblk(1), colblk(2), colblk(3),
                  pl.BlockSpec((None, tg, LANES), lambda b, j: (b, j, 4 * W // LANES)),
                  pl.BlockSpec((LANES, W), const),
                  pl.BlockSpec((1, W), const), pl.BlockSpec((1, W), const),
                  pl.BlockSpec((GLA_CHUNK, GLA_CHUNK), const),
                  pl.BlockSpec((W, W), const)],
        out_specs=pl.BlockSpec((None, tg, W), lambda b, j: (b, j, 0)),
        out_shape=jax.ShapeDtypeStruct((B, S, W), F32),
        scratch_shapes=[pltpu.VMEM((GLA_HEADS // 2, LANES, LANES), F32)],
        compiler_params=pltpu.CompilerParams(
            dimension_semantics=("parallel", "arbitrary"), vmem_limit_bytes=VMEM_LIMIT_BYTES),
        name="gla",
    )(p_gla, p_gla, p_gla, p_gla, p_gla, w2, bg, ng, tri, mean_blk)


def _relbias_kernel(rb_ref, tb_ref):
    r = lax.broadcasted_iota(I32, (Q_BLOCK, LANES), 0)
    c = lax.broadcasted_iota(I32, (Q_BLOCK, LANES), 1)
    max_exact = REL_BUCKETS // 2
    for delta in range(REL_SLABS):
        n = jnp.maximum(delta * LANES + r - c, 0)
        nf = jnp.maximum(n, 1).astype(F32)
        large = max_exact + (jnp.log(nf / max_exact) / math.log(REL_MAX_DIST / max_exact)
                             * (REL_BUCKETS - max_exact)).astype(I32)
        large = jnp.minimum(large, REL_BUCKETS - 1)
        bucket = jnp.where(n < max_exact, n, large)
        for h in range(DSA_HEADS):
            acc = jnp.zeros((Q_BLOCK, LANES), F32)
            for b in range(REL_BUCKETS):
                acc = jnp.where(bucket == b, rb_ref[b, h], acc)
            tb_ref[h, delta] = (acc - rb_ref[REL_BUCKETS - 1, h]) * LOG2E


def _relbias_tiles(rel_bias):
    return pl.pallas_call(
        _relbias_kernel,
        in_specs=[pl.BlockSpec(memory_space=pltpu.SMEM)],
        out_specs=pl.BlockSpec(memory_space=pltpu.VMEM),
        out_shape=jax.ShapeDtypeStruct((DSA_HEADS, REL_SLABS, Q_BLOCK, LANES), F32),
        name="rel_bias_tiles",
    )(rel_bias)


def _dsa_kernel(iq_ref, iw_ref, q_ref, kit_ref, kt_ref, v_ref, tb_ref, uo_ref, y_ref,
                key_ref, kmx_ref, m_ref, acc_ref, *, top_k):
    i = pl.program_id(1)
    TQ, CK, L = Q_BLOCK, KEY_CHUNK, LANES
    S = key_ref.shape[1]
    n_chunks = (i + CK // L) // (CK // L)
    lane = lax.broadcasted_iota(I32, (TQ, L), 1)

    @pl.when(i == 0)
    def _():
        nb = 2 * CK
        for h in range(DSA_HEADS):
            def norm_chunk(c, mx):
                kk = kt_ref[h * DSA_HEAD_DIM:(h + 1) * DSA_HEAD_DIM, pl.ds(pl.multiple_of(c * nb, nb), nb)]
                kk = kk.astype(F32)
                return jnp.maximum(mx, jnp.sum(kk * kk, axis=0, keepdims=True))
            mx = lax.fori_loop(0, S // nb, norm_chunk, jnp.zeros((1, nb), F32))
            kmx_ref[h:h + 1, :] = jnp.broadcast_to(jnp.max(mx, axis=1, keepdims=True), (1, L))

    iq = iq_ref[...]
    iq_h = [iq[:, h * IDX_DIM:(h + 1) * IDX_DIM] for h in range(IDX_HEADS)]
    scale = (IDX_DIM ** -0.5) * (IDX_HEADS ** -0.5)
    w_h = [jnp.broadcast_to(iw_ref[:, h:h + 1] * scale, (TQ, CK)) for h in range(IDX_HEADS)]

    def score_chunk(c, masked):
        off = pl.multiple_of(c * CK, CK)
        kc = kit_ref[:, pl.ds(off, CK)]
        sc = jnp.zeros((TQ, CK), F32)
        for h in range(IDX_HEADS):
            z = jnp.dot(iq_h[h], kc, preferred_element_type=F32)
            sc = sc + jnp.maximum(z, 0.0) * w_h[h]
        sc = sc + 0.0
        if masked:
            row_t = i * TQ + lax.broadcasted_iota(I32, (TQ, CK), 0)
            pos = off + lax.broadcasted_iota(I32, (TQ, CK), 1)
            sc = jnp.where(pos <= row_t, sc, -jnp.inf)
        bits = pltpu.bitcast(sc, I32)
        key_ref[:, pl.ds(off, CK)] = bits ^ ((bits >> 31) & I32(0x7FFFFFFF))

    def score_body(c, carry):
        score_chunk(c, False)
        return carry

    lax.fori_loop(0, n_chunks - 1, score_body, 0)
    score_chunk(n_chunks - 1, True)

    def count_ge(cand):
        cand_b = jnp.broadcast_to(cand, (TQ, L))

        def body(c, acc):
            kk = key_ref[:, pl.ds(pl.multiple_of(c * CK, CK), CK)]
            for s in range(CK // L):
                acc = acc + (kk[:, s * L:(s + 1) * L] >= cand_b).astype(I32)
            return acc
        acc = lax.fori_loop(0, n_chunks, body, jnp.zeros((TQ, L), I32))
        return jnp.sum(acc, axis=1, keepdims=True)

    def bit_step(thr, cnt_thr, bit):
        cand = thr + bit
        cnt = count_ge(cand)
        ok = cnt >= top_k
        return jnp.where(ok, cand, thr), jnp.where(ok, cnt, cnt_thr)

    def high_step(b, st):
        return bit_step(st[0], st[1], lax.shift_left(I32(1), I32(31) - b))

    thr, cnt_thr = lax.fori_loop(
        0, 32 - HALF_BITS, high_step,
        (jnp.full((TQ, 1), INT_MIN, I32), jnp.full((TQ, 1), S, I32)))

    cnt_next = count_ge(thr + 1)
    exact = cnt_next < top_k

    def low_cond(st):
        b, _, _, active = st
        return (b < HALF_BITS) & (active > 0)

    def low_step(st):
        b, thr, cnt_thr, _ = st
        new_thr, new_cnt = bit_step(thr, cnt_thr, lax.shift_left(I32(1), I32(HALF_BITS - 1) - b))
        thr = jnp.where(exact, thr, new_thr)
        cnt_thr = jnp.where(exact, cnt_thr, new_cnt)
        active = jnp.max(jnp.where(exact | (cnt_thr == top_k), 0, 1))
        return b + 1, thr, cnt_thr, active

    active0 = jnp.max(jnp.where(exact | (cnt_thr == top_k), 0, 1))
    _, thr, cnt_thr, _ = lax.while_loop(low_cond, low_step, (I32(0), thr, cnt_thr, active0))

    t_sel = jnp.where(cnt_thr == top_k, thr - 1, thr)
    need = (top_k - count_ge(t_sel + 1)).astype(F32)
    t_sel_b = jnp.broadcast_to(t_sel, (TQ, L))
    need_b = jnp.broadcast_to(need, (TQ, L))
    uo = uo_ref[...]

    def mask_chunk(c, run):
        off = pl.multiple_of(c * CK, CK)
        kc = key_ref[:, pl.ds(off, CK)]
        out = []
        for s in range(CK // L):
            kk = kc[:, s * L:(s + 1) * L]
            eq = kk == t_sel_b
            pr = jnp.dot(jnp.where(eq, 1.0, 0.0).astype(BF16), uo, preferred_element_type=F32)
            sel = (kk > t_sel_b) | (eq & (run + pr[:, :L] <= need_b))
            sel = sel & (kk > NEG_INF_KEY)
            out.append(jnp.where(sel, 0.0, -jnp.inf).astype(F32))
            run = run + pr[:, L:]
        key_ref[:, pl.ds(off, CK)] = pltpu.bitcast(jnp.concatenate(out, axis=1), I32)
        return run

    lax.fori_loop(0, n_chunks, mask_chunk, jnp.zeros((TQ, L), F32))

    q2 = q_ref[...] * LOG2E
    q_ext = []
    shift_max = jnp.zeros((TQ, L), F32)
    for h in range(DSA_HEADS):
        blk = q2[:, (h // 2) * L:(h // 2 + 1) * L]
        if h % 2:
            blk = pltpu.roll(blk, DSA_HEAD_DIM, 1)
        qf = jnp.where(lane < DSA_HEAD_DIM, blk, 0.0).astype(BF16).astype(F32)
        bound = jnp.sqrt(jnp.sum(qf * qf, axis=1, keepdims=True) * kmx_ref[h:h + 1, :]) * (1.0 + 2.0 ** -7)
        shift_max = jnp.maximum(shift_max, bound)
        q_ext.append(jnp.where(lane == DSA_HEAD_DIM, bound, qf).astype(BF16))
    minus_one_row = jnp.where(lax.broadcasted_iota(I32, (DSA_HEAD_DIM, CK), 0) == 0, -1.0, 0.0).astype(BF16)
    ones_cols = jnp.ones((CK, L), BF16)

    acc_ref[...] = jnp.zeros(acc_ref.shape, F32)

    def attend(c, biased, fast):
        off = pl.multiple_of(c * CK, CK)
        width = CK
        mb = pltpu.bitcast(key_ref[:, pl.ds(off, width)], F32)
        for h in range(DSA_HEADS):
            k_ext = jnp.concatenate(
                [kt_ref[h * DSA_HEAD_DIM:(h + 1) * DSA_HEAD_DIM, pl.ds(off, width)], minus_one_row], axis=0)
            s = jnp.dot(q_ext[h], k_ext, preferred_element_type=F32) + mb
            if biased:
                back = [jnp.clip(i - (c * (CK // L) + j), 0, REL_SLABS - 1) for j in range(CK // L)]
                s = s + jnp.concatenate([tb_ref[h, d] for d in back], axis=1)
            pair = (h // 2) * L
            v_ext = jnp.concatenate([v_ref[pl.ds(off, width), pair:pair + L], ones_cols], axis=1)
            if fast:
                p = jnp.exp2(s).astype(BF16)
                acc_ref[h] += jnp.dot(p, v_ext, preferred_element_type=F32)
            else:
                m_old = m_ref[h]
                m_new = jnp.maximum(m_old, jnp.max(s, axis=1, keepdims=True))
                alpha = jnp.exp2(m_old - m_new)
                p = jnp.concatenate(
                    [jnp.exp2(s[:, j * L:(j + 1) * L] - m_new) for j in range(width // L)], axis=1)
                pv = jnp.dot(p.astype(BF16), v_ext, preferred_element_type=F32)
                acc_ref[h] = acc_ref[h] * jnp.concatenate([alpha, alpha], axis=1) + pv
                m_ref[h] = m_new

    def attend_all(fast):
        first_biased = jnp.maximum(i - 1, 0) // (CK // L)

        def far_chunk(c, carry):
            attend(c, False, fast)
            return carry

        def near_chunk(c, carry):
            attend(c, True, fast)
            return carry

        lax.fori_loop(0, first_biased, far_chunk, 0)
        lax.fori_loop(first_biased, n_chunks, near_chunk, 0)

    use_fast = 2.0 * jnp.max(shift_max) + jnp.max(jnp.abs(tb_ref[...])) <= FAST_SOFTMAX_MAX_SHIFT

    @pl.when(use_fast)
    def _():
        attend_all(True)

    @pl.when(jnp.logical_not(use_fast))
    def _():
        m_ref[...] = jnp.full(m_ref.shape, NEG_BIG, F32)
        attend_all(False)

    for p in range(DSA_HEADS // 2):
        a0, a1 = acc_ref[2 * p], acc_ref[2 * p + 1]
        y_ref[:, p * L:(p + 1) * L] = jnp.where(lane < DSA_HEAD_DIM, a0[:, :L] / a0[:, L:], a1[:, :L] / a1[:, L:])


def _dsa(iq, iw, q, kit, kt, v, tb, *, top_k):
    B, S, W = q.shape
    upper = jnp.triu(jnp.ones((LANES, LANES), F32))
    uo = jnp.concatenate([upper, jnp.ones((LANES, LANES), F32)], axis=1).astype(BF16)
    qrow = lambda b, i: (b, i, 0)
    whole = lambda b, i: (b, 0, 0)
    one = pl.Buffered(1)
    return pl.pallas_call(
        functools.partial(_dsa_kernel, top_k=top_k),
        grid=(B, S // Q_BLOCK),
        in_specs=[pl.BlockSpec((None, Q_BLOCK, LANES), qrow),
                  pl.BlockSpec((None, Q_BLOCK, LANES), qrow),
                  pl.BlockSpec((None, Q_BLOCK, W), qrow),
                  pl.BlockSpec((None, IDX_DIM, S), whole, pipeline_mode=one),
                  pl.BlockSpec((None, W, S), whole, pipeline_mode=one),
                  pl.BlockSpec((None, S, W), whole, pipeline_mode=one),
                  pl.BlockSpec((DSA_HEADS, REL_SLABS, Q_BLOCK, LANES), lambda b, i: (0, 0, 0, 0)),
                  pl.BlockSpec((LANES, 2 * LANES), lambda b, i: (0, 0))],
        out_specs=pl.BlockSpec((None, Q_BLOCK, W), qrow),
        out_shape=jax.ShapeDtypeStruct((B, S, W), F32),
        scratch_shapes=[pltpu.VMEM((Q_BLOCK, S), I32),
                        pltpu.VMEM((SUBLANES, LANES), F32),
                        pltpu.VMEM((DSA_HEADS, Q_BLOCK, LANES), F32),
                        pltpu.VMEM((DSA_HEADS, Q_BLOCK, 2 * LANES), F32)],
        compiler_params=pltpu.CompilerParams(
            dimension_semantics=("parallel", "arbitrary"), vmem_limit_bytes=VMEM_LIMIT_BYTES),
        name="dsa",
    )(iq, iw, q, kit, kt, v, tb, uo)


def _layer_norm(z, g, b):
    mu = jnp.mean(z, axis=-1, keepdims=True)
    zc = z - mu
    var = jnp.mean(zc * zc, axis=-1, keepdims=True)
    return zc * lax.rsqrt(var + LN_EPS) * g + b


def _mix_kernel(x_ref, ya_ref, yb_ref, yc_ref, wa_ref, wb_ref, wc_ref, g_ref, b_ref, o_ref):
    mix = jnp.dot(ya_ref[...].astype(BF16), wa_ref[...], preferred_element_type=F32)
    mix = mix + jnp.dot(yb_ref[...].astype(BF16), wb_ref[...], preferred_element_type=F32)
    mix = mix + jnp.dot(yc_ref[...].astype(BF16), wc_ref[...], preferred_element_type=F32)
    o_ref[...] = _layer_norm(DN_ALPHA * x_ref[...] + mix, g_ref[...], b_ref[...])


def _mix(x, ya, yb, yc, wa, wb, wc, g, b, *, tm):
    T, D = x.shape
    row = lambda m: (m, 0)
    const = lambda m: (0, 0)
    return pl.pallas_call(
        _mix_kernel,
        grid=(T // tm,),
        in_specs=[pl.BlockSpec((tm, D), row),
                  pl.BlockSpec((tm, ya.shape[1]), row), pl.BlockSpec((tm, yb.shape[1]), row),
                  pl.BlockSpec((tm, yc.shape[1]), row),
                  pl.BlockSpec(wa.shape, const), pl.BlockSpec(wb.shape, const), pl.BlockSpec(wc.shape, const),
                  pl.BlockSpec((1, D), const), pl.BlockSpec((1, D), const)],
        out_specs=pl.BlockSpec((tm, D), row),
        out_shape=jax.ShapeDtypeStruct((T, D), F32),
        compiler_params=pltpu.CompilerParams(
            dimension_semantics=("parallel",), vmem_limit_bytes=VMEM_LIMIT_BYTES),
        name="out_proj_ln",
    )(x, ya, yb, yc, wa, wb, wc, g, b)


def _ffn_kernel(x_ref, wg_ref, wu_ref, wd_ref, g_ref, b_ref, o_ref, xb_ref, acc_ref):
    f = pl.program_id(1)

    @pl.when(f == 0)
    def _():
        xb_ref[...] = x_ref[...].astype(BF16)
        acc_ref[...] = jnp.zeros_like(acc_ref)

    xb = xb_ref[...]
    gate = jnp.dot(xb, wg_ref[...], preferred_element_type=F32)
    up = jnp.dot(xb, wu_ref[...], preferred_element_type=F32)
    h = (gate * _sigmoid(gate)) * up
    acc_ref[...] += jnp.dot(h.astype(BF16), wd_ref[...], preferred_element_type=F32)

    @pl.when(f == pl.num_programs(1) - 1)
    def _():
        o_ref[...] = _layer_norm(DN_ALPHA * x_ref[...] + acc_ref[...], g_ref[...], b_ref[...])


def _ffn(x, wg, wu, wd, g, b, *, tm, tf):
    T, D = x.shape
    FF = wg.shape[1]
    row = lambda m, f: (m, 0)
    const = lambda m, f: (0, 0)
    return pl.pallas_call(
        _ffn_kernel,
        grid=(T // tm, FF // tf),
        in_specs=[pl.BlockSpec((tm, D), row),
                  pl.BlockSpec((D, tf), lambda m, f: (0, f)),
                  pl.BlockSpec((D, tf), lambda m, f: (0, f)),
                  pl.BlockSpec((tf, D), lambda m, f: (f, 0)),
                  pl.BlockSpec((1, D), const), pl.BlockSpec((1, D), const)],
        out_specs=pl.BlockSpec((tm, D), row),
        out_shape=jax.ShapeDtypeStruct((T, D), F32),
        scratch_shapes=[pltpu.VMEM((tm, D), BF16), pltpu.VMEM((tm, D), F32)],
        compiler_params=pltpu.CompilerParams(
            dimension_semantics=("parallel", "arbitrary"), vmem_limit_bytes=VMEM_LIMIT_BYTES),
        name="ffn_ln",
    )(x, wg, wu, wd, g, b)


def _split_w_in(w):
    parts = [w[:, IN_OFFSETS[j]:IN_OFFSETS[j + 1]] for j in range(len(IN_SIZES))]
    lru_x, lru_g, gq, gk, gv, gg, g_lr, dq, dk, dv, iq, ik, iw = parts
    D = w.shape[0]
    pad = lambda a, n: jnp.pad(a, ((0, 0), (0, n - a.shape[1])))
    wa = jnp.concatenate([lru_x, lru_g, gq, gk, gv, gg, pad(g_lr, LANES),
                          dq * (DSA_HEAD_DIM ** -0.5), dv, iq, pad(iw, LANES)], axis=1)
    wt = jnp.concatenate([dk, ik], axis=1).T
    assert wa.shape == (D, PA_END) and wt.shape == (PT_ROWS, D)
    return wa.astype(BF16), wt.astype(BF16)


def _block_diag(w):
    n, k, _ = w.shape
    eye = jnp.eye(n, dtype=w.dtype)
    return (eye[:, None, :, None] * w[:, :, None, :]).reshape(n * k, n * k)


def kernel(x, w_in, conv_w, conv_b, lru_wa, lru_ba, lru_wx, lru_bx, lru_lambda, gla_w_gate2, gla_b_gate,
           gla_norm_g, rel_bias, w_out, ln1_g, ln1_b, w_ffn_gate, w_ffn_up, w_ffn_down, ln2_g, ln2_b):
    B, S, D = x.shape
    T = B * S
    top_k = min(IDX_TOPK_MAX, S // 4)
    assert S % (2 * KEY_CHUNK) == 0 and D == D_MODEL
    tm = min(512, S)
    tb = _relbias_tiles(rel_bias)
    row = lambda a: a.reshape(1, -1)
    for l in range(w_in.shape[0]):
        wa, wt = _split_w_in(w_in[l])
        p_lru, p_gla, dq, dv, iq, iw, kt, kit = _proj(x, wa, wt, tm=tm)
        y_lru = _lru(p_lru, conv_w[l], row(conv_b[l]),
                     _block_diag(lru_wa[l]).astype(BF16), row(lru_ba[l]),
                     _block_diag(lru_wx[l]).astype(BF16), row(lru_bx[l]), row(lru_lambda[l]), tl=tm)
        w2 = jnp.pad(gla_w_gate2[l], ((0, LANES - GLA_GATE_RANK), (0, 0)))
        y_gla = _gla(p_gla, w2, row(gla_b_gate[l]), row(gla_norm_g[l]), tg=min(256, S))
        y_dsa = _dsa(iq, iw, dq, kit, kt, dv, tb, top_k=top_k)
        wo = w_out[l].astype(BF16)
        x1 = _mix(x.reshape(T, D), y_lru.reshape(T, -1), y_gla.reshape(T, -1), y_dsa.reshape(T, -1),
                  wo[:LRU_WIDTH], wo[LRU_WIDTH:LRU_WIDTH + GLA_WIDTH], wo[LRU_WIDTH + GLA_WIDTH:],
                  row(ln1_g[l]), row(ln1_b[l]), tm=tm)
        x2 = _ffn(x1, w_ffn_gate[l].astype(BF16), w_ffn_up[l].astype(BF16), w_ffn_down[l].astype(BF16),
                  row(ln2_g[l]), row(ln2_b[l]), tm=tm, tf=D_FF // 2)
        x = x2.reshape(B, S, D)
    return x
```

```python
import functools
import math

import jax
import jax.numpy as jnp
import numpy as np
from jax import lax
from jax.experimental import pallas as pl
from jax.experimental.pallas import tpu as pltpu

F32 = jnp.float32
BF16 = jnp.bfloat16
I32 = jnp.int32

D_MODEL = 1024
DEPTH = 2
LRU_WIDTH = 256
LRU_BLOCKS = 4
LRU_BLOCK = LRU_WIDTH // LRU_BLOCKS
CONV_WIDTH = 4
LRU_C = 8.0
GLA_HEADS = 6
GLA_DK = 64
GLA_DV = 64
GLA_GATE_RANK = 16
GLA_GATE_TAU = 16.0
GLA_CHUNK = 64
GLA_WIDTH = GLA_HEADS * GLA_DV
DSA_HEADS = 6
DSA_HEAD_DIM = 64
DSA_WIDTH = DSA_HEADS * DSA_HEAD_DIM
IDX_HEADS = 4
IDX_DIM = 32
IDX_TOPK_MAX = 256
REL_BUCKETS = 32
REL_MAX_DIST = 128
D_FF = 2816
DN_ALPHA = (2.0 * DEPTH) ** 0.25
LN_EPS = 1e-5

IN_SIZES = (LRU_WIDTH, LRU_WIDTH, GLA_WIDTH, GLA_WIDTH, GLA_WIDTH, GLA_WIDTH, GLA_GATE_RANK,
            DSA_WIDTH, DSA_WIDTH, DSA_WIDTH, IDX_HEADS * IDX_DIM, IDX_DIM, IDX_HEADS)
IN_OFFSETS = [0] + [int(o) for o in np.cumsum(IN_SIZES)]

LANES = 128
SUBLANES = 8
VMEM_LIMIT_BYTES = 56 * 1024 * 1024

LRU_COLS = 2 * LRU_WIDTH
GLA_COLS = 4 * GLA_WIDTH + LANES
PA_LRU = 0
PA_GLA = PA_LRU + LRU_COLS
PA_DQ = PA_GLA + GLA_COLS
PA_DV = PA_DQ + DSA_WIDTH
PA_IQ = PA_DV + DSA_WIDTH
PA_IW = PA_IQ + LANES
PA_END = PA_IW + LANES
PT_ROWS = DSA_WIDTH + IDX_DIM

Q_BLOCK = 128
KEY_CHUNK = 512
SAMPLE_SHIFT = 4
SAMPLE_STRIDE = 1 << SAMPLE_SHIFT
GUESS_BITS = 20
FIRST_STEP_LOG2 = 18.0
MAX_SEARCH_PASSES = 128
INT_MIN = -2147483648
INT_MAX = 2147483647
NEG_INF_KEY = -2139095041
NEG_BIG = -1e30
LOG2E = 1.4426950408889634
REL_SLABS = 3
FAST_SOFTMAX_MAX_SHIFT = 120.0


def _nt_dot(a, b):
    return lax.dot_general(a, b, (((1,), (1,)), ((), ())), preferred_element_type=F32)


def _tn_dot(a, b):
    return lax.dot_general(a, b, (((0,), (0,)), ((), ())), preferred_element_type=F32)


def _softplus(z):
    return jnp.maximum(z, 0.0) + jnp.log(1.0 + jnp.exp(-jnp.abs(z)))


def _sigmoid(z):
    return 1.0 / (1.0 + jnp.exp(-z))


def _proj_kernel(x_ref, wa_ref, wt_ref, lru_ref, gla_ref, q_ref, v_ref, iq_ref, iw_ref, kt_ref, kit_ref):
    xb = x_ref[...].astype(BF16)

    def mm(c0, c1):
        return jnp.dot(xb, wa_ref[:, c0:c1], preferred_element_type=F32)

    lru_ref[...] = mm(PA_LRU, PA_GLA)
    gla_ref[...] = mm(PA_GLA, PA_DQ)
    q_ref[...] = mm(PA_DQ, PA_DV)
    v_ref[...] = mm(PA_DV, PA_IQ).astype(BF16)
    iq_ref[...] = mm(PA_IQ, PA_IW).astype(BF16)
    iw_ref[...] = mm(PA_IW, PA_END)
    t = _nt_dot(wt_ref[...], xb)
    kt_ref[...] = t[:DSA_WIDTH].astype(BF16)
    kit_ref[...] = t[DSA_WIDTH:PT_ROWS].astype(BF16)


def _proj(x, wa, wt, *, tm):
    B, S, D = x.shape
    grid = (B, S // tm)
    row = lambda b, m: (b, m, 0)
    col = lambda b, m: (b, 0, m)
    const = lambda b, m: (0, 0)
    out_shape = (
        jax.ShapeDtypeStruct((B, S, LRU_COLS), F32),
        jax.ShapeDtypeStruct((B, S, GLA_COLS), F32),
        jax.ShapeDtypeStruct((B, S, DSA_WIDTH), F32),
        jax.ShapeDtypeStruct((B, S, DSA_WIDTH), BF16),
        jax.ShapeDtypeStruct((B, S, LANES), BF16),
        jax.ShapeDtypeStruct((B, S, LANES), F32),
        jax.ShapeDtypeStruct((B, DSA_WIDTH, S), BF16),
        jax.ShapeDtypeStruct((B, IDX_DIM, S), BF16),
    )
    out_specs = (
        pl.BlockSpec((None, tm, LRU_COLS), row),
        pl.BlockSpec((None, tm, GLA_COLS), row),
        pl.BlockSpec((None, tm, DSA_WIDTH), row),
        pl.BlockSpec((None, tm, DSA_WIDTH), row),
        pl.BlockSpec((None, tm, LANES), row),
        pl.BlockSpec((None, tm, LANES), row),
        pl.BlockSpec((None, DSA_WIDTH, tm), col),
        pl.BlockSpec((None, IDX_DIM, tm), col),
    )
    return pl.pallas_call(
        _proj_kernel,
        grid=grid,
        in_specs=[pl.BlockSpec((None, tm, D), row),
                  pl.BlockSpec((D, PA_END), const),
                  pl.BlockSpec((PT_ROWS, D), const)],
        out_specs=out_specs,
        out_shape=out_shape,
        compiler_params=pltpu.CompilerParams(
            dimension_semantics=("parallel", "parallel"), vmem_limit_bytes=VMEM_LIMIT_BYTES),
        name="in_proj",
    )(x, wa, wt)


def _shift_rows(x, d, fill, row):
    return jnp.where(row >= d, pltpu.roll(x, d, 0), fill)


def _lru_kernel(p_ref, cw_ref, cb_ref, wa_ref, ba_ref, wx_ref, bx_ref, lam_ref, y_ref, xprev_ref, h_ref):
    tl = p_ref.shape[0]

    @pl.when(pl.program_id(1) == 0)
    def _():
        xprev_ref[...] = jnp.zeros_like(xprev_ref)
        h_ref[...] = jnp.zeros_like(h_ref)

    xb = p_ref[:, 0:LRU_WIDTH]
    gb = p_ref[:, LRU_WIDTH:2 * LRU_WIDTH]
    prev = xprev_ref[...]
    row8 = lax.broadcasted_iota(I32, (SUBLANES, LRU_WIDTH), 0)
    cw = cw_ref[...]
    xc = cb_ref[...] + cw[CONV_WIDTH - 1:CONV_WIDTH] * xb
    for d in range(1, CONV_WIDTH):
        r = pltpu.roll(xb, d, 0)
        top = jnp.where(row8 < d, pltpu.roll(prev, d, 0), r[0:SUBLANES])
        r = jnp.concatenate([top, r[SUBLANES:]], axis=0)
        xc = xc + cw[CONV_WIDTH - 1 - d:CONV_WIDTH - d] * r
    xprev_ref[...] = xb[tl - SUBLANES:tl]

    xcb = xc.astype(BF16)
    r_gate = _sigmoid(jnp.dot(xcb, wa_ref[...], preferred_element_type=F32) + ba_ref[...])
    i_gate = _sigmoid(jnp.dot(xcb, wx_ref[...], preferred_element_type=F32) + bx_ref[...])
    log_a = (-LRU_C) * r_gate * _softplus(-lam_ref[...])
    a = jnp.exp(log_a)
    u = jnp.sqrt(1.0 - a * a) * (i_gate * xc)

    row = lax.broadcasted_iota(I32, (tl, LRU_WIDTH), 0)
    d = 1
    while d < tl:
        a_s = _shift_rows(a, d, 1.0, row)
        u_s = _shift_rows(u, d, 0.0, row)
        u = u + a * u_s
        a = a * a_s
        d *= 2
    h = u + a * h_ref[SUBLANES - 1:SUBLANES, :]
    h_ref[...] = h[tl - SUBLANES:tl]
    y_ref[...] = h * jax.nn.gelu(gb)


def _lru(p_lru, cw, cb, wa, ba, wx, bx, lam, *, tl):
    B, S, _ = p_lru.shape
    const = lambda b, j: (0, 0)
    vec = pl.BlockSpec((1, LRU_WIDTH), const)
    mat = pl.BlockSpec((LRU_WIDTH, LRU_WIDTH), const)
    return pl.pallas_call(
        _lru_kernel,
        grid=(B, S // tl),
        in_specs=[pl.BlockSpec((None, tl, LRU_COLS), lambda b, j: (b, j, 0)),
                  pl.BlockSpec((CONV_WIDTH, LRU_WIDTH), const), vec, mat, vec, mat, vec, vec],
        out_specs=pl.BlockSpec((None, tl, LRU_WIDTH), lambda b, j: (b, j, 0)),
        out_shape=jax.ShapeDtypeStruct((B, S, LRU_WIDTH), F32),
        scratch_shapes=[pltpu.VMEM((SUBLANES, LRU_WIDTH), F32), pltpu.VMEM((SUBLANES, LRU_WIDTH), F32)],
        compiler_params=pltpu.CompilerParams(
            dimension_semantics=("parallel", "arbitrary"), vmem_limit_bytes=VMEM_LIMIT_BYTES),
        name="rg_lru",
    )(p_lru, cw, cb, wa, ba, wx, bx, lam)


def _gla_kernel(q_ref, k_ref, v_ref, g_ref, glr_ref, w2_ref, bg_ref, ng_ref, tri_ref, mean_ref,
                y_ref, st_ref):
    tg = q_ref.shape[0]
    C = GLA_CHUNK
    hi = lax.Precision.HIGHEST

    @pl.when(pl.program_id(1) == 0)
    def _():
        st_ref[...] = jnp.zeros_like(st_ref)

    z = jnp.dot(glr_ref[...], w2_ref[...], preferred_element_type=F32, precision=hi) + bg_ref[...]
    log_alpha = -_softplus(-z) * (1.0 / GLA_GATE_TAU)

    lane = lax.broadcasted_iota(I32, (C, LANES), 1)
    first = lane < GLA_DK
    rr = lax.broadcasted_iota(I32, (C, C), 0)
    cc = lax.broadcasted_iota(I32, (C, C), 1)
    causal = cc <= rr
    r2 = lax.broadcasted_iota(I32, (LANES, LANES), 0)
    c2 = lax.broadcasted_iota(I32, (LANES, LANES), 1)
    same_head = (r2 < GLA_DV) == (c2 < GLA_DK)
    tri = tri_ref[...]

    for c in range(tg // C):
        rows = slice(c * C, (c + 1) * C)
        bcum = jnp.dot(tri, log_alpha[rows], preferred_element_type=F32, precision=hi)
        blast = bcum[C - 1:C]
        kf = k_ref[rows, :]
        q_dec = q_ref[rows, :] * (GLA_DK ** -0.5) * jnp.exp(bcum)
        k_inv = kf * jnp.exp(-bcum)
        k_end = kf * jnp.exp(blast - bcum)
        decay = jnp.exp(blast)
        vf = v_ref[rows, :]
        outs = []
        for p in range(GLA_HEADS // 2):
            cs = slice(p * LANES, (p + 1) * LANES)
            qd, ki, ke, vp = q_dec[:, cs], k_inv[:, cs], k_end[:, cs], vf[:, cs]
            kib = ki.astype(BF16)
            vpb = vp.astype(BF16)
            att0 = _nt_dot(jnp.where(first, qd, 0.0).astype(BF16), kib)
            att1 = _nt_dot(jnp.where(first, 0.0, qd).astype(BF16), kib)
            att0 = jnp.where(causal, att0, 0.0).astype(BF16)
            att1 = jnp.where(causal, att1, 0.0).astype(BF16)
            o_intra = jnp.where(first,
                                jnp.dot(att0, vpb, preferred_element_type=F32),
                                jnp.dot(att1, vpb, preferred_element_type=F32))
            st = st_ref[p]
            o_inter = _nt_dot(qd.astype(BF16), st.astype(BF16))
            u_t = _tn_dot(vpb, ke.astype(BF16))
            st_ref[p] = st * decay[:, cs] + jnp.where(same_head, u_t, 0.0)
            outs.append(o_intra + o_inter)
        o = jnp.concatenate(outs, axis=1)
        ms = jnp.dot(o * o, mean_ref[...], preferred_element_type=F32, precision=hi)
        o = o * lax.rsqrt(ms + 1e-6) * ng_ref[...]
        gf = g_ref[rows, :]
        y_ref[rows, :] = o * (gf * _sigmoid(gf))


def _gla(p_gla, w2, bg, ng, *, tg):
    B, S, _ = p_gla.shape
    W = GLA_WIDTH
    const = lambda b, j: (0, 0)
    tri = jnp.tril(jnp.ones((GLA_CHUNK, GLA_CHUNK), F32))
    head = jnp.arange(W) // GLA_DV
    mean_blk = (head[:, None] == head[None, :]).astype(F32) / GLA_DV

    def colblk(c):
        return pl.BlockSpec((None, tg, W), lambda b, j: (b, j, c))

    return pl.pallas_call(
        _gla_kernel,
        grid=(B, S // tg),
        in_specs=[colblk(0), colblk(1), colblk(2), colblk(3),
                  pl.BlockSpec((None, tg, LANES), lambda b, j: (b, j, 4 * W // LANES)),
                  pl.BlockSpec((LANES, W), const),
                  pl.BlockSpec((1, W), const), pl.BlockSpec((1, W), const),
                  pl.BlockSpec((GLA_CHUNK, GLA_CHUNK), const),
                  pl.BlockSpec((W, W), const)],
        out_specs=pl.BlockSpec((None, tg, W), lambda b, j: (b, j, 0)),
        out_shape=jax.ShapeDtypeStruct((B, S, W), F32),
        scratch_shapes=[pltpu.VMEM((GLA_HEADS // 2, LANES, LANES), F32)],
        compiler_params=pltpu.CompilerParams(
            dimension_semantics=("parallel", "arbitrary"), vmem_limit_bytes=VMEM_LIMIT_BYTES),
        name="gla",
    )(p_gla, p_gla, p_gla, p_gla, p_gla, w2, bg, ng, tri, mean_blk)


def _relbias_kernel(rb_ref, tb_ref):
    r = lax.broadcasted_iota(I32, (Q_BLOCK, LANES), 0)
    c = lax.broadcasted_iota(I32, (Q_BLOCK, LANES), 1)
    max_exact = REL_BUCKETS // 2
    for delta in range(REL_SLABS):
        n = jnp.maximum(delta * LANES + r - c, 0)
        nf = jnp.maximum(n, 1).astype(F32)
        large = max_exact + (jnp.log(nf / max_exact) / math.log(REL_MAX_DIST / max_exact)
                             * (REL_BUCKETS - max_exact)).astype(I32)
        large = jnp.minimum(large, REL_BUCKETS - 1)
        bucket = jnp.where(n < max_exact, n, large)
        for h in range(DSA_HEADS):
            acc = jnp.zeros((Q_BLOCK, LANES), F32)
            for b in range(REL_BUCKETS):
                acc = jnp.where(bucket == b, rb_ref[b, h], acc)
            tb_ref[h, delta] = (acc - rb_ref[REL_BUCKETS - 1, h]) * LOG2E


def _relbias_tiles(rel_bias):
    return pl.pallas_call(
        _relbias_kernel,
        in_specs=[pl.BlockSpec(memory_space=pltpu.SMEM)],
        out_specs=pl.BlockSpec(memory_space=pltpu.VMEM),
        out_shape=jax.ShapeDtypeStruct((DSA_HEADS, REL_SLABS, Q_BLOCK, LANES), F32),
        name="rel_bias_tiles",
    )(rel_bias)


def _dsa_kernel(iq_ref, iw_ref, q_ref, kit_ref, kt_ref, v_ref, tb_ref, uo_ref, y_ref,
                key_ref, kmx_ref, m_ref, acc_ref, *, top_k):
    i = pl.program_id(1)
    TQ, CK, L = Q_BLOCK, KEY_CHUNK, LANES
    S = key_ref.shape[1]
    n_chunks = (i + CK // L) // (CK // L)
    lane = lax.broadcasted_iota(I32, (TQ, L), 1)

    @pl.when(i == 0)
    def _():
        nb = 2 * CK
        for h in range(DSA_HEADS):
            def norm_chunk(c, mx):
                kk = kt_ref[h * DSA_HEAD_DIM:(h + 1) * DSA_HEAD_DIM, pl.ds(pl.multiple_of(c * nb, nb), nb)]
                kk = kk.astype(F32)
                return jnp.maximum(mx, jnp.sum(kk * kk, axis=0, keepdims=True))
            mx = lax.fori_loop(0, S // nb, norm_chunk, jnp.zeros((1, nb), F32))
            kmx_ref[h:h + 1, :] = jnp.broadcast_to(jnp.max(mx, axis=1, keepdims=True), (1, L))

    iq = iq_ref[...]
    iq_h = [iq[:, h * IDX_DIM:(h + 1) * IDX_DIM] for h in range(IDX_HEADS)]
    scale = (IDX_DIM ** -0.5) * (IDX_HEADS ** -0.5)
    w_h = [jnp.broadcast_to(iw_ref[:, h:h + 1] * scale, (TQ, CK)) for h in range(IDX_HEADS)]

    def score_chunk(c, masked):
        off = pl.multiple_of(c * CK, CK)
        kc = kit_ref[:, pl.ds(off, CK)]
        sc = jnp.zeros((TQ, CK), F32)
        for h in range(IDX_HEADS):
            z = jnp.dot(iq_h[h], kc, preferred_element_type=F32)
            sc = sc + jnp.maximum(z, 0.0) * w_h[h]
        sc = sc + 0.0
        if masked:
            row_t = i * TQ + lax.broadcasted_iota(I32, (TQ, CK), 0)
            pos = off + lax.broadcasted_iota(I32, (TQ, CK), 1)
            sc = jnp.where(pos <= row_t, sc, -jnp.inf)
        bits = pltpu.bitcast(sc, I32)
        key_ref[:, pl.ds(off, CK)] = bits ^ ((bits >> 31) & I32(0x7FFFFFFF))

    def score_body(c, carry):
        score_chunk(c, False)
        return carry

    lax.fori_loop(0, n_chunks - 1, score_body, 0)
    score_chunk(n_chunks - 1, True)

    def count_ge(cand):
        cand_b = jnp.broadcast_to(cand, (TQ, L))

        def body(c, acc):
            kk = key_ref[:, pl.ds(pl.multiple_of(c * CK, CK), CK)]
            for s in range(CK // L):
                acc = acc + (kk[:, s * L:(s + 1) * L] >= cand_b).astype(I32)
            return acc
        acc = lax.fori_loop(0, n_chunks, body, jnp.zeros((TQ, L), I32))
        return jnp.sum(acc, axis=1, keepdims=True)

    n_samp = (i >> SAMPLE_SHIFT) + 1

    def count_sample(cand):
        cand_b = jnp.broadcast_to(cand, (TQ, L))

        def body(c, acc):
            off = pl.multiple_of(c * (SAMPLE_STRIDE * L), SAMPLE_STRIDE * L)
            return acc + (key_ref[:, pl.ds(off, L)] >= cand_b).astype(I32)
        acc = lax.fori_loop(0, n_samp, body, jnp.zeros((TQ, L), I32))
        return jnp.sum(acc, axis=1, keepdims=True)

    def guess_step(b, g):
        cand = g + lax.shift_left(I32(1), I32(31) - b)
        ok = count_sample(cand) * (i + 1) >= top_k * n_samp
        return jnp.where(ok, cand, g)

    guess = lax.fori_loop(0, GUESS_BITS, guess_step, jnp.full((TQ, 1), INT_MIN, I32))
    guess = jnp.maximum(guess, INT_MIN + 1)

    def search_cond(st):
        return (st[6] > 0) & (st[0] < MAX_SEARCH_PASSES)

    def search_step(st):
        n, lo, c_lo, hi, c_hi, cand, _ = st
        cnt = count_ge(cand)
        ok = cnt >= top_k
        lo, c_lo = jnp.where(ok, cand, lo), jnp.where(ok, cnt, c_lo)
        hi, c_hi = jnp.where(ok, hi, cand), jnp.where(ok, c_hi, cnt)
        settled = (c_lo == top_k) | (hi == lo + 1)
        have_lo, have_hi = lo != INT_MIN, hi != INT_MAX
        lo_f, hi_f = lo.astype(F32), hi.astype(F32)
        step = jnp.exp2(jnp.minimum(n, 12).astype(F32) + FIRST_STEP_LOG2)
        gallop = jnp.clip(jnp.where(have_lo, lo_f + step, hi_f - step), -2.0 ** 31 + 256.0, 2.0 ** 31 - 256.0)
        width = hi_f - lo_f
        frac = (c_lo - top_k).astype(F32) + 0.5
        frac = frac / jnp.maximum((c_lo - c_hi).astype(F32), 1.0)
        narrow = width < 2.0 ** 30
        interp = lo + jnp.maximum((jnp.where(narrow, width, 0.0) * frac).astype(I32), 1)
        mid = (lo >> 1) + (hi >> 1) + (lo & hi & 1)
        nxt = jnp.where(have_lo & have_hi, jnp.where(narrow & ((n & 1) == 0), interp, mid), gallop.astype(I32))
        nxt = jnp.where((n == 0) & ok, lo + 1, nxt)
        nxt = jnp.minimum(jnp.maximum(nxt, lo + 1), hi - 1)
        nxt = jnp.where(settled, lo, nxt)
        return n + 1, lo, c_lo, hi, c_hi, nxt, jnp.max(jnp.where(settled, 0, 1))

    col = lambda v: jnp.full((TQ, 1), v, I32)
    _, lo, c_lo, _, c_hi, _, _ = lax.while_loop(
        search_cond, search_step, (I32(0), col(INT_MIN), col(S), col(INT_MAX), col(0), guess, I32(1)))

    t_sel = jnp.where(c_lo == top_k, lo - 1, lo)
    need = jnp.where(c_lo == top_k, 0, top_k - c_hi).astype(F32)
    t_sel_b = jnp.broadcast_to(t_sel, (TQ, L))
    need_b = jnp.broadcast_to(need, (TQ, L))
    uo = uo_ref[...]

    def mask_chunk(c, run):
        off = pl.multiple_of(c * CK, CK)
        kc = key_ref[:, pl.ds(off, CK)]
        out = []
        for s in range(CK // L):
            kk = kc[:, s * L:(s + 1) * L]
            eq = kk == t_sel_b
            pr = jnp.dot(jnp.where(eq, 1.0, 0.0).astype(BF16), uo, preferred_element_type=F32)
            sel = (kk > t_sel_b) | (eq & (run + pr[:, :L] <= need_b))
            sel = sel & (kk > NEG_INF_KEY)
            out.append(jnp.where(sel, 0.0, -jnp.inf).astype(F32))
            run = run + pr[:, L:]
        key_ref[:, pl.ds(off, CK)] = pltpu.bitcast(jnp.concatenate(out, axis=1), I32)
        return run

    lax.fori_loop(0, n_chunks, mask_chunk, jnp.zeros((TQ, L), F32))

    q2 = q_ref[...] * LOG2E
    q_ext = []
    shift_max = jnp.zeros((TQ, L), F32)
    for h in range(DSA_HEADS):
        blk = q2[:, (h // 2) * L:(h // 2 + 1) * L]
        if h % 2:
            blk = pltpu.roll(blk, DSA_HEAD_DIM, 1)
        qf = jnp.where(lane < DSA_HEAD_DIM, blk, 0.0).astype(BF16).astype(F32)
        bound = jnp.sqrt(jnp.sum(qf * qf, axis=1, keepdims=True) * kmx_ref[h:h + 1, :]) * (1.0 + 2.0 ** -7)
        shift_max = jnp.maximum(shift_max, bound)
        q_ext.append(jnp.where(lane == DSA_HEAD_DIM, bound, qf).astype(BF16))
    minus_one_row = jnp.where(lax.broadcasted_iota(I32, (DSA_HEAD_DIM, CK), 0) == 0, -1.0, 0.0).astype(BF16)
    ones_cols = jnp.ones((CK, L), BF16)

    acc_ref[...] = jnp.zeros(acc_ref.shape, F32)

    def attend(c, biased, fast):
        off = pl.multiple_of(c * CK, CK)
        width = CK
        mb = pltpu.bitcast(key_ref[:, pl.ds(off, width)], F32)
        for h in range(DSA_HEADS):
            k_ext = jnp.concatenate(
                [kt_ref[h * DSA_HEAD_DIM:(h + 1) * DSA_HEAD_DIM, pl.ds(off, width)], minus_one_row], axis=0)
            s = jnp.dot(q_ext[h], k_ext, preferred_element_type=F32) + mb
            if biased:
                back = [jnp.clip(i - (c * (CK // L) + j), 0, REL_SLABS - 1) for j in range(CK // L)]
                s = s + jnp.concatenate([tb_ref[h, d] for d in back], axis=1)
            pair = (h // 2) * L
            v_ext = jnp.concatenate([v_ref[pl.ds(off, width), pair:pair + L], ones_cols], axis=1)
            if fast:
                p = jnp.exp2(s).astype(BF16)
                acc_ref[h] += jnp.dot(p, v_ext, preferred_element_type=F32)
            else:
                m_old = m_ref[h]
                m_new = jnp.maximum(m_old, jnp.max(s, axis=1, keepdims=True))
                alpha = jnp.exp2(m_old - m_new)
                p = jnp.concatenate(
                    [jnp.exp2(s[:, j * L:(j + 1) * L] - m_new) for j in range(width // L)], axis=1)
                pv = jnp.dot(p.astype(BF16), v_ext, preferred_element_type=F32)
                acc_ref[h] = acc_ref[h] * jnp.concatenate([alpha, alpha], axis=1) + pv
                m_ref[h] = m_new

    def attend_all(fast):
        first_biased = jnp.maximum(i - 1, 0) // (CK // L)

        def far_chunk(c, carry):
            attend(c, False, fast)
            return carry

        def near_chunk(c, carry):
            attend(c, True, fast)
            return carry

        lax.fori_loop(0, first_biased, far_chunk, 0)
        lax.fori_loop(first_biased, n_chunks, near_chunk, 0)

    use_fast = 2.0 * jnp.max(shift_max) + jnp.max(jnp.abs(tb_ref[...])) <= FAST_SOFTMAX_MAX_SHIFT

    @pl.when(use_fast)
    def _():
        attend_all(True)

    @pl.when(jnp.logical_not(use_fast))
    def _():
        m_ref[...] = jnp.full(m_ref.shape, NEG_BIG, F32)
        attend_all(False)

    for p in range(DSA_HEADS // 2):
        a0, a1 = acc_ref[2 * p], acc_ref[2 * p + 1]
        y_ref[:, p * L:(p + 1) * L] = jnp.where(lane < DSA_HEAD_DIM, a0[:, :L] / a0[:, L:], a1[:, :L] / a1[:, L:])


def _dsa(iq, iw, q, kit, kt, v, tb, *, top_k):
    B, S, W = q.shape
    upper = jnp.triu(jnp.ones((LANES, LANES), F32))
    uo = jnp.concatenate([upper, jnp.ones((LANES, LANES), F32)], axis=1).astype(BF16)
    qrow = lambda b, i: (b, i, 0)
    whole = lambda b, i: (b, 0, 0)
    one = pl.Buffered(1)
    return pl.pallas_call(
        functools.partial(_dsa_kernel, top_k=top_k),
        grid=(B, S // Q_BLOCK),
        in_specs=[pl.BlockSpec((None, Q_BLOCK, LANES), qrow),
                  pl.BlockSpec((None, Q_BLOCK, LANES), qrow),
                  pl.BlockSpec((None, Q_BLOCK, W), qrow),
                  pl.BlockSpec((None, IDX_DIM, S), whole, pipeline_mode=one),
                  pl.BlockSpec((None, W, S), whole, pipeline_mode=one),
                  pl.BlockSpec((None, S, W), whole, pipeline_mode=one),
                  pl.BlockSpec((DSA_HEADS, REL_SLABS, Q_BLOCK, LANES), lambda b, i: (0, 0, 0, 0)),
                  pl.BlockSpec((LANES, 2 * LANES), lambda b, i: (0, 0))],
        out_specs=pl.BlockSpec((None, Q_BLOCK, W), qrow),
        out_shape=jax.ShapeDtypeStruct((B, S, W), F32),
        scratch_shapes=[pltpu.VMEM((Q_BLOCK, S), I32),
                        pltpu.VMEM((SUBLANES, LANES), F32),
                        pltpu.VMEM((DSA_HEADS, Q_BLOCK, LANES), F32),
                        pltpu.VMEM((DSA_HEADS, Q_BLOCK, 2 * LANES), F32)],
        compiler_params=pltpu.CompilerParams(
            dimension_semantics=("parallel", "arbitrary"), vmem_limit_bytes=VMEM_LIMIT_BYTES),
        name="dsa",
    )(iq, iw, q, kit, kt, v, tb, uo)


def _layer_norm(z, g, b):
    mu = jnp.mean(z, axis=-1, keepdims=True)
    zc = z - mu
    var = jnp.mean(zc * zc, axis=-1, keepdims=True)
    return zc * lax.rsqrt(var + LN_EPS) * g + b


def _mix_kernel(x_ref, ya_ref, yb_ref, yc_ref, wa_ref, wb_ref, wc_ref, g_ref, b_ref, o_ref):
    mix = jnp.dot(ya_ref[...].astype(BF16), wa_ref[...], preferred_element_type=F32)
    mix = mix + jnp.dot(yb_ref[...].astype(BF16), wb_ref[...], preferred_element_type=F32)
    mix = mix + jnp.dot(yc_ref[...].astype(BF16), wc_ref[...], preferred_element_type=F32)
    o_ref[...] = _layer_norm(DN_ALPHA * x_ref[...] + mix, g_ref[...], b_ref[...])


def _mix(x, ya, yb, yc, wa, wb, wc, g, b, *, tm):
    T, D = x.shape
    row = lambda m: (m, 0)
    const = lambda m: (0, 0)
    return pl.pallas_call(
        _mix_kernel,
        grid=(T // tm,),
        in_specs=[pl.BlockSpec((tm, D), row),
                  pl.BlockSpec((tm, ya.shape[1]), row), pl.BlockSpec((tm, yb.shape[1]), row),
                  pl.BlockSpec((tm, yc.shape[1]), row),
                  pl.BlockSpec(wa.shape, const), pl.BlockSpec(wb.shape, const), pl.BlockSpec(wc.shape, const),
                  pl.BlockSpec((1, D), const), pl.BlockSpec((1, D), const)],
        out_specs=pl.BlockSpec((tm, D), row),
        out_shape=jax.ShapeDtypeStruct((T, D), F32),
        compiler_params=pltpu.CompilerParams(
            dimension_semantics=("parallel",), vmem_limit_bytes=VMEM_LIMIT_BYTES),
        name="out_proj_ln",
    )(x, ya, yb, yc, wa, wb, wc, g, b)


def _ffn_kernel(x_ref, wg_ref, wu_ref, wd_ref, g_ref, b_ref, o_ref, xb_ref, acc_ref):
    f = pl.program_id(1)

    @pl.when(f == 0)
    def _():
        xb_ref[...] = x_ref[...].astype(BF16)
        acc_ref[...] = jnp.zeros_like(acc_ref)

    xb = xb_ref[...]
    gate = jnp.dot(xb, wg_ref[...], preferred_element_type=F32)
    up = jnp.dot(xb, wu_ref[...], preferred_element_type=F32)
    h = (gate * _sigmoid(gate)) * up
    acc_ref[...] += jnp.dot(h.astype(BF16), wd_ref[...], preferred_element_type=F32)

    @pl.when(f == pl.num_programs(1) - 1)
    def _():
        o_ref[...] = _layer_norm(DN_ALPHA * x_ref[...] + acc_ref[...], g_ref[...], b_ref[...])


def _ffn(x, wg, wu, wd, g, b, *, tm, tf):
    T, D = x.shape
    FF = wg.shape[1]
    row = lambda m, f: (m, 0)
    const = lambda m, f: (0, 0)
    return pl.pallas_call(
        _ffn_kernel,
        grid=(T // tm, FF // tf),
        in_specs=[pl.BlockSpec((tm, D), row),
                  pl.BlockSpec((D, tf), lambda m, f: (0, f)),
                  pl.BlockSpec((D, tf), lambda m, f: (0, f)),
                  pl.BlockSpec((tf, D), lambda m, f: (f, 0)),
                  pl.BlockSpec((1, D), const), pl.BlockSpec((1, D), const)],
        out_specs=pl.BlockSpec((tm, D), row),
        out_shape=jax.ShapeDtypeStruct((T, D), F32),
        scratch_shapes=[pltpu.VMEM((tm, D), BF16), pltpu.VMEM((tm, D), F32)],
        compiler_params=pltpu.CompilerParams(
            dimension_semantics=("parallel", "arbitrary"), vmem_limit_bytes=VMEM_LIMIT_BYTES),
        name="ffn_ln",
    )(x, wg, wu, wd, g, b)


def _split_w_in(w):
    parts = [w[:, IN_OFFSETS[j]:IN_OFFSETS[j + 1]] for j in range(len(IN_SIZES))]
    lru_x, lru_g, gq, gk, gv, gg, g_lr, dq, dk, dv, iq, ik, iw = parts
    D = w.shape[0]
    pad = lambda a, n: jnp.pad(a, ((0, 0), (0, n - a.shape[1])))
    wa = jnp.concatenate([lru_x, lru_g, gq, gk, gv, gg, pad(g_lr, LANES),
                          dq * (DSA_HEAD_DIM ** -0.5), dv, iq, pad(iw, LANES)], axis=1)
    wt = jnp.concatenate([dk, ik], axis=1).T
    assert wa.shape == (D, PA_END) and wt.shape == (PT_ROWS, D)
    return wa.astype(BF16), wt.astype(BF16)


def _block_diag(w):
    n, k, _ = w.shape
    eye = jnp.eye(n, dtype=w.dtype)
    return (eye[:, None, :, None] * w[:, :, None, :]).reshape(n * k, n * k)


def kernel(x, w_in, conv_w, conv_b, lru_wa, lru_ba, lru_wx, lru_bx, lru_lambda, gla_w_gate2, gla_b_gate,
           gla_norm_g, rel_bias, w_out, ln1_g, ln1_b, w_ffn_gate, w_ffn_up, w_ffn_down, ln2_g, ln2_b):
    B, S, D = x.shape
    T = B * S
    top_k = min(IDX_TOPK_MAX, S // 4)
    assert S % (2 * KEY_CHUNK) == 0 and D == D_MODEL
    tm = min(512, S)
    tb = _relbias_tiles(rel_bias)
    row = lambda a: a.reshape(1, -1)
    for l in range(w_in.shape[0]):
        wa, wt = _split_w_in(w_in[l])
        p_lru, p_gla, dq, dv, iq, iw, kt, kit = _proj(x, wa, wt, tm=tm)
        y_lru = _lru(p_lru, conv_w[l], row(conv_b[l]),
                     _block_diag(lru_wa[l]).astype(BF16), row(lru_ba[l]),
                     _block_diag(lru_wx[l]).astype(BF16), row(lru_bx[l]), row(lru_lambda[l]), tl=tm)
        w2 = jnp.pad(gla_w_gate2[l], ((0, LANES - GLA_GATE_RANK), (0, 0)))
        y_gla = _gla(p_gla, w2, row(gla_b_gate[l]), row(gla_norm_g[l]), tg=min(256, S))
        y_dsa = _dsa(iq, iw, dq, kit, kt, dv, tb, top_k=top_k)
        wo = w_out[l].astype(BF16)
        x1 = _mix(x.reshape(T, D), y_lru.reshape(T, -1), y_gla.reshape(T, -1), y_dsa.reshape(T, -1),
                  wo[:LRU_WIDTH], wo[LRU_WIDTH:LRU_WIDTH + GLA_WIDTH], wo[LRU_WIDTH + GLA_WIDTH:],
                  row(ln1_g[l]), row(ln1_b[l]), tm=tm)
        x2 = _ffn(x1, w_ffn_gate[l].astype(BF16), w_ffn_up[l].astype(BF16), w_ffn_down[l].astype(BF16),
                  row(ln2_g[l]), row(ln2_b[l]), tm=tm, tf=D_FF // 2)
        x = x2.reshape(B, S, D)
    return x
```

```python
import functools
import math

import jax
import jax.numpy as jnp
import numpy as np
from jax import lax
from jax.experimental import pallas as pl
from jax.experimental.pallas import tpu as pltpu

F32 = jnp.float32
BF16 = jnp.bfloat16
I32 = jnp.int32

D_MODEL = 1024
DEPTH = 2
LRU_WIDTH = 256
LRU_BLOCKS = 4
LRU_BLOCK = LRU_WIDTH // LRU_BLOCKS
CONV_WIDTH = 4
LRU_C = 8.0
GLA_HEADS = 6
GLA_DK = 64
GLA_DV = 64
GLA_GATE_RANK = 16
GLA_GATE_TAU = 16.0
GLA_CHUNK = 64
GLA_WIDTH = GLA_HEADS * GLA_DV
DSA_HEADS = 6
DSA_HEAD_DIM = 64
DSA_WIDTH = DSA_HEADS * DSA_HEAD_DIM
IDX_HEADS = 4
IDX_DIM = 32
IDX_TOPK_MAX = 256
REL_BUCKETS = 32
REL_MAX_DIST = 128
D_FF = 2816
DN_ALPHA = (2.0 * DEPTH) ** 0.25
LN_EPS = 1e-5

IN_SIZES = (LRU_WIDTH, LRU_WIDTH, GLA_WIDTH, GLA_WIDTH, GLA_WIDTH, GLA_WIDTH, GLA_GATE_RANK,
            DSA_WIDTH, DSA_WIDTH, DSA_WIDTH, IDX_HEADS * IDX_DIM, IDX_DIM, IDX_HEADS)
IN_OFFSETS = [0] + [int(o) for o in np.cumsum(IN_SIZES)]

LANES = 128
SUBLANES = 8
VMEM_LIMIT_BYTES = 56 * 1024 * 1024

LRU_COLS = 2 * LRU_WIDTH
GLA_COLS = 4 * GLA_WIDTH + LANES
PA_LRU = 0
PA_GLA = PA_LRU + LRU_COLS
PA_DQ = PA_GLA + GLA_COLS
PA_DV = PA_DQ + DSA_WIDTH
PA_IQ = PA_DV + DSA_WIDTH
PA_IW = PA_IQ + LANES
PA_END = PA_IW + LANES
PT_ROWS = DSA_WIDTH + IDX_DIM

Q_BLOCK = 256
Q_SUB = Q_BLOCK // LANES
KEY_CHUNK = 512
HALF_BITS = 16
INT_MIN = -2147483648
NEG_INF_KEY = -2139095041
NEG_BIG = -1e30
LOG2E = 1.4426950408889634
REL_SLABS = 3
FAST_SOFTMAX_MAX_SHIFT = 120.0


def _nt_dot(a, b):
    return lax.dot_general(a, b, (((1,), (1,)), ((), ())), preferred_element_type=F32)


def _tn_dot(a, b):
    return lax.dot_general(a, b, (((0,), (0,)), ((), ())), preferred_element_type=F32)


def _softplus(z):
    return jnp.maximum(z, 0.0) + jnp.log(1.0 + jnp.exp(-jnp.abs(z)))


def _sigmoid(z):
    return 1.0 / (1.0 + jnp.exp(-z))


def _proj_kernel(x_ref, wa_ref, wt_ref, lru_ref, gla_ref, q_ref, v_ref, iq_ref, iw_ref, kt_ref, kit_ref):
    xb = x_ref[...].astype(BF16)

    def mm(c0, c1):
        return jnp.dot(xb, wa_ref[:, c0:c1], preferred_element_type=F32)

    lru_ref[...] = mm(PA_LRU, PA_GLA)
    gla_ref[...] = mm(PA_GLA, PA_DQ)
    q_ref[...] = mm(PA_DQ, PA_DV)
    v_ref[...] = mm(PA_DV, PA_IQ).astype(BF16)
    iq_ref[...] = mm(PA_IQ, PA_IW).astype(BF16)
    iw_ref[...] = mm(PA_IW, PA_END)
    t = _nt_dot(wt_ref[...], xb)
    kt_ref[...] = t[:DSA_WIDTH].astype(BF16)
    kit_ref[...] = t[DSA_WIDTH:PT_ROWS].astype(BF16)


def _proj(x, wa, wt, *, tm):
    B, S, D = x.shape
    grid = (B, S // tm)
    row = lambda b, m: (b, m, 0)
    col = lambda b, m: (b, 0, m)
    const = lambda b, m: (0, 0)
    out_shape = (
        jax.ShapeDtypeStruct((B, S, LRU_COLS), F32),
        jax.ShapeDtypeStruct((B, S, GLA_COLS), F32),
        jax.ShapeDtypeStruct((B, S, DSA_WIDTH), F32),
        jax.ShapeDtypeStruct((B, S, DSA_WIDTH), BF16),
        jax.ShapeDtypeStruct((B, S, LANES), BF16),
        jax.ShapeDtypeStruct((B, S, LANES), F32),
        jax.ShapeDtypeStruct((B, DSA_WIDTH, S), BF16),
        jax.ShapeDtypeStruct((B, IDX_DIM, S), BF16),
    )
    out_specs = (
        pl.BlockSpec((None, tm, LRU_COLS), row),
        pl.BlockSpec((None, tm, GLA_COLS), row),
        pl.BlockSpec((None, tm, DSA_WIDTH), row),
        pl.BlockSpec((None, tm, DSA_WIDTH), row),
        pl.BlockSpec((None, tm, LANES), row),
        pl.BlockSpec((None, tm, LANES), row),
        pl.BlockSpec((None, DSA_WIDTH, tm), col),
        pl.BlockSpec((None, IDX_DIM, tm), col),
    )
    return pl.pallas_call(
        _proj_kernel,
        grid=grid,
        in_specs=[pl.BlockSpec((None, tm, D), row),
                  pl.BlockSpec((D, PA_END), const),
                  pl.BlockSpec((PT_ROWS, D), const)],
        out_specs=out_specs,
        out_shape=out_shape,
        compiler_params=pltpu.CompilerParams(
            dimension_semantics=("parallel", "parallel"), vmem_limit_bytes=VMEM_LIMIT_BYTES),
        name="in_proj",
    )(x, wa, wt)


def _shift_rows(x, d, fill, row):
    return jnp.where(row >= d, pltpu.roll(x, d, 0), fill)


def _lru_kernel(p_ref, cw_ref, cb_ref, wa_ref, ba_ref, wx_ref, bx_ref, lam_ref, y_ref, xprev_ref, h_ref):
    tl = p_ref.shape[0]

    @pl.when(pl.program_id(1) == 0)
    def _():
        xprev_ref[...] = jnp.zeros_like(xprev_ref)
        h_ref[...] = jnp.zeros_like(h_ref)

    xb = p_ref[:, 0:LRU_WIDTH]
    gb = p_ref[:, LRU_WIDTH:2 * LRU_WIDTH]
    prev = xprev_ref[...]
    row8 = lax.broadcasted_iota(I32, (SUBLANES, LRU_WIDTH), 0)
    cw = cw_ref[...]
    xc = cb_ref[...] + cw[CONV_WIDTH - 1:CONV_WIDTH] * xb
    for d in range(1, CONV_WIDTH):
        r = pltpu.roll(xb, d, 0)
        top = jnp.where(row8 < d, pltpu.roll(prev, d, 0), r[0:SUBLANES])
        r = jnp.concatenate([top, r[SUBLANES:]], axis=0)
        xc = xc + cw[CONV_WIDTH - 1 - d:CONV_WIDTH - d] * r
    xprev_ref[...] = xb[tl - SUBLANES:tl]

    xcb = xc.astype(BF16)
    r_gate = _sigmoid(jnp.dot(xcb, wa_ref[...], preferred_element_type=F32) + ba_ref[...])
    i_gate = _sigmoid(jnp.dot(xcb, wx_ref[...], preferred_element_type=F32) + bx_ref[...])
    log_a = (-LRU_C) * r_gate * _softplus(-lam_ref[...])
    a = jnp.exp(log_a)
    u = jnp.sqrt(1.0 - a * a) * (i_gate * xc)

    row = lax.broadcasted_iota(I32, (tl, LRU_WIDTH), 0)
    d = 1
    while d < tl:
        a_s = _shift_rows(a, d, 1.0, row)
        u_s = _shift_rows(u, d, 0.0, row)
        u = u + a * u_s
        a = a * a_s
        d *= 2
    h = u + a * h_ref[SUBLANES - 1:SUBLANES, :]
    h_ref[...] = h[tl - SUBLANES:tl]
    y_ref[...] = h * jax.nn.gelu(gb)


def _lru(p_lru, cw, cb, wa, ba, wx, bx, lam, *, tl):
    B, S, _ = p_lru.shape
    const = lambda b, j: (0, 0)
    vec = pl.BlockSpec((1, LRU_WIDTH), const)
    mat = pl.BlockSpec((LRU_WIDTH, LRU_WIDTH), const)
    return pl.pallas_call(
        _lru_kernel,
        grid=(B, S // tl),
        in_specs=[pl.BlockSpec((None, tl, LRU_COLS), lambda b, j: (b, j, 0)),
                  pl.BlockSpec((CONV_WIDTH, LRU_WIDTH), const), vec, mat, vec, mat, vec, vec],
        out_specs=pl.BlockSpec((None, tl, LRU_WIDTH), lambda b, j: (b, j, 0)),
        out_shape=jax.ShapeDtypeStruct((B, S, LRU_WIDTH), F32),
        scratch_shapes=[pltpu.VMEM((SUBLANES, LRU_WIDTH), F32), pltpu.VMEM((SUBLANES, LRU_WIDTH), F32)],
        compiler_params=pltpu.CompilerParams(
            dimension_semantics=("parallel", "arbitrary"), vmem_limit_bytes=VMEM_LIMIT_BYTES),
        name="rg_lru",
    )(p_lru, cw, cb, wa, ba, wx, bx, lam)


def _gla_kernel(q_ref, k_ref, v_ref, g_ref, glr_ref, w2_ref, bg_ref, ng_ref, tri_ref, mean_ref,
                y_ref, st_ref):
    tg = q_ref.shape[0]
    C = GLA_CHUNK
    hi = lax.Precision.HIGHEST

    @pl.when(pl.program_id(1) == 0)
    def _():
        st_ref[...] = jnp.zeros_like(st_ref)

    z = jnp.dot(glr_ref[...], w2_ref[...], preferred_element_type=F32, precision=hi) + bg_ref[...]
    log_alpha = -_softplus(-z) * (1.0 / GLA_GATE_TAU)

    lane = lax.broadcasted_iota(I32, (C, LANES), 1)
    first = lane < GLA_DK
    rr = lax.broadcasted_iota(I32, (C, C), 0)
    cc = lax.broadcasted_iota(I32, (C, C), 1)
    causal = cc <= rr
    r2 = lax.broadcasted_iota(I32, (LANES, LANES), 0)
    c2 = lax.broadcasted_iota(I32, (LANES, LANES), 1)
    same_head = (r2 < GLA_DV) == (c2 < GLA_DK)
    tri = tri_ref[...]

    for c in range(tg // C):
        rows = slice(c * C, (c + 1) * C)
        bcum = jnp.dot(tri, log_alpha[rows], preferred_element_type=F32, precision=hi)
        blast = bcum[C - 1:C]
        kf = k_ref[rows, :]
        q_dec = q_ref[rows, :] * (GLA_DK ** -0.5) * jnp.exp(bcum)
        k_inv = kf * jnp.exp(-bcum)
        k_end = kf * jnp.exp(blast - bcum)
        decay = jnp.exp(blast)
        vf = v_ref[rows, :]
        outs = []
        for p in range(GLA_HEADS // 2):
            cs = slice(p * LANES, (p + 1) * LANES)
            qd, ki, ke, vp = q_dec[:, cs], k_inv[:, cs], k_end[:, cs], vf[:, cs]
            kib = ki.astype(BF16)
            vpb = vp.astype(BF16)
            att0 = _nt_dot(jnp.where(first, qd, 0.0).astype(BF16), kib)
            att1 = _nt_dot(jnp.where(first, 0.0, qd).astype(BF16), kib)
            att0 = jnp.where(causal, att0, 0.0).astype(BF16)
            att1 = jnp.where(causal, att1, 0.0).astype(BF16)
            o_intra = jnp.where(first,
                                jnp.dot(att0, vpb, preferred_element_type=F32),
                                jnp.dot(att1, vpb, preferred_element_type=F32))
            st = st_ref[p]
            o_inter = _nt_dot(qd.astype(BF16), st.astype(BF16))
            u_t = _tn_dot(vpb, ke.astype(BF16))
            st_ref[p] = st * decay[:, cs] + jnp.where(same_head, u_t, 0.0)
            outs.append(o_intra + o_inter)
        o = jnp.concatenate(outs, axis=1)
        ms = jnp.dot(o * o, mean_ref[...], preferred_element_type=F32, precision=hi)
        o = o * lax.rsqrt(ms + 1e-6) * ng_ref[...]
        gf = g_ref[rows, :]
        y_ref[rows, :] = o * (gf * _sigmoid(gf))


def _gla(p_gla, w2, bg, ng, *, tg):
    B, S, _ = p_gla.shape
    W = GLA_WIDTH
    const = lambda b, j: (0, 0)
    tri = jnp.tril(jnp.ones((GLA_CHUNK, GLA_CHUNK), F32))
    head = jnp.arange(W) // GLA_DV
    mean_blk = (head[:, None] == head[None, :]).astype(F32) / GLA_DV

    def colblk(c):
        return pl.BlockSpec((None, tg, W), lambda b, j: (b, j, c))

    return pl.pallas_call(
        _gla_kernel,
        grid=(B, S // tg),
        in_specs=[colblk(0), colblk(1), colblk(2), colblk(3),
                  pl.BlockSpec((None, tg, LANES), lambda b, j: (b, j, 4 * W // LANES)),
                  pl.BlockSpec((LANES, W), const),
                  pl.BlockSpec((1, W), const), pl.BlockSpec((1, W), const),
                  pl.BlockSpec((GLA_CHUNK, GLA_CHUNK), const),
                  pl.BlockSpec((W, W), const)],
        out_specs=pl.BlockSpec((None, tg, W), lambda b, j: (b, j, 0)),
        out_shape=jax.ShapeDtypeStruct((B, S, W), F32),
        scratch_shapes=[pltpu.VMEM((GLA_HEADS // 2, LANES, LANES), F32)],
        compiler_params=pltpu.CompilerParams(
            dimension_semantics=("parallel", "arbitrary"), vmem_limit_bytes=VMEM_LIMIT_BYTES),
        name="gla",
    )(p_gla, p_gla, p_gla, p_gla, p_gla, w2, bg, ng, tri, mean_blk)


def _relbias_kernel(rb_ref, tb_ref):
    r = lax.broadcasted_iota(I32, (LANES, LANES), 0)
    c = lax.broadcasted_iota(I32, (LANES, LANES), 1)
    max_exact = REL_BUCKETS // 2
    for delta in range(REL_SLABS):
        n = jnp.maximum(delta * LANES + r - c, 0)
        nf = jnp.maximum(n, 1).astype(F32)
        large = max_exact + (jnp.log(nf / max_exact) / math.log(REL_MAX_DIST / max_exact)
                             * (REL_BUCKETS - max_exact)).astype(I32)
        large = jnp.minimum(large, REL_BUCKETS - 1)
        bucket = jnp.where(n < max_exact, n, large)
        for h in range(DSA_HEADS):
            acc = jnp.zeros((LANES, LANES), F32)
            for b in range(REL_BUCKETS):
                acc = jnp.where(bucket == b, rb_ref[b, h], acc)
            tb_ref[h, delta] = (acc - rb_ref[REL_BUCKETS - 1, h]) * LOG2E


def _relbias_tiles(rel_bias):
    return pl.pallas_call(
        _relbias_kernel,
        in_specs=[pl.BlockSpec(memory_space=pltpu.SMEM)],
        out_specs=pl.BlockSpec(memory_space=pltpu.VMEM),
        out_shape=jax.ShapeDtypeStruct((DSA_HEADS, REL_SLABS, LANES, LANES), F32),
        name="rel_bias_tiles",
    )(rel_bias)


def _dsa_kernel(iq_ref, iw_ref, q_ref, kit_ref, kt_ref, v_ref, tb_ref, uo_ref, y_ref,
                key_ref, kmx_ref, m_ref, acc_ref, *, top_k):
    i = pl.program_id(1)
    TQ, CK, L = Q_BLOCK, KEY_CHUNK, LANES
    S = key_ref.shape[1]
    first_blk = i * Q_SUB
    n_chunks = (first_blk + Q_SUB - 1) // (CK // L) + 1
    lane = lax.broadcasted_iota(I32, (TQ, L), 1)

    @pl.when(i == 0)
    def _():
        nb = 2 * CK
        for h in range(DSA_HEADS):
            def norm_chunk(c, mx):
                kk = kt_ref[h * DSA_HEAD_DIM:(h + 1) * DSA_HEAD_DIM, pl.ds(pl.multiple_of(c * nb, nb), nb)]
                kk = kk.astype(F32)
                return jnp.maximum(mx, jnp.sum(kk * kk, axis=0, keepdims=True))
            mx = lax.fori_loop(0, S // nb, norm_chunk, jnp.zeros((1, nb), F32))
            kmx_ref[h:h + 1, :] = jnp.broadcast_to(jnp.max(mx, axis=1, keepdims=True), (1, L))

    iq = iq_ref[...]
    iq_h = [iq[:, h * IDX_DIM:(h + 1) * IDX_DIM] for h in range(IDX_HEADS)]
    scale = (IDX_DIM ** -0.5) * (IDX_HEADS ** -0.5)
    w_h = [jnp.broadcast_to(iw_ref[:, h:h + 1] * scale, (TQ, CK)) for h in range(IDX_HEADS)]

    def score_chunk(c, masked):
        off = pl.multiple_of(c * CK, CK)
        kc = kit_ref[:, pl.ds(off, CK)]
        sc = jnp.zeros((TQ, CK), F32)
        for h in range(IDX_HEADS):
            z = jnp.dot(iq_h[h], kc, preferred_element_type=F32)
            sc = sc + jnp.maximum(z, 0.0) * w_h[h]
        sc = sc + 0.0
        if masked:
            row_t = i * TQ + lax.broadcasted_iota(I32, (TQ, CK), 0)
            pos = off + lax.broadcasted_iota(I32, (TQ, CK), 1)
            sc = jnp.where(pos <= row_t, sc, -jnp.inf)
        bits = pltpu.bitcast(sc, I32)
        key_ref[:, pl.ds(off, CK)] = bits ^ ((bits >> 31) & I32(0x7FFFFFFF))

    def score_body(c, carry):
        score_chunk(c, False)
        return carry

    lax.fori_loop(0, n_chunks - 1, score_body, 0)
    score_chunk(n_chunks - 1, True)

    def count_ge(cand):
        cand_all = jnp.broadcast_to(cand, (TQ, L))
        accs = []
        for r in range(Q_SUB):
            rows = slice(r * L, (r + 1) * L)
            cand_b = cand_all[rows]

            def body(c, acc, rows=rows, cand_b=cand_b):
                kk = key_ref[rows, pl.ds(pl.multiple_of(c * CK, CK), CK)]
                for s in range(CK // L):
                    acc = acc + (kk[:, s * L:(s + 1) * L] >= cand_b).astype(I32)
                return acc
            accs.append(lax.fori_loop(0, n_chunks, body, jnp.zeros((L, L), I32)))
        acc = accs[0] if Q_SUB == 1 else jnp.concatenate(accs, axis=0)
        return jnp.sum(acc, axis=1, keepdims=True)

    def bit_step(thr, cnt_thr, bit):
        cand = thr + bit
        cnt = count_ge(cand)
        ok = cnt >= top_k
        return jnp.where(ok, cand, thr), jnp.where(ok, cnt, cnt_thr)

    def high_step(b, st):
        return bit_step(st[0], st[1], lax.shift_left(I32(1), I32(31) - b))

    thr, cnt_thr = lax.fori_loop(
        0, 32 - HALF_BITS, high_step,
        (jnp.full((TQ, 1), INT_MIN, I32), jnp.full((TQ, 1), S, I32)))

    cnt_next = count_ge(thr + 1)
    exact = cnt_next < top_k

    def low_cond(st):
        b, _, _, active = st
        return (b < HALF_BITS) & (active > 0)

    def low_step(st):
        b, thr, cnt_thr, _ = st
        new_thr, new_cnt = bit_step(thr, cnt_thr, lax.shift_left(I32(1), I32(HALF_BITS - 1) - b))
        thr = jnp.where(exact, thr, new_thr)
        cnt_thr = jnp.where(exact, cnt_thr, new_cnt)
        active = jnp.max(jnp.where(exact | (cnt_thr == top_k), 0, 1))
        return b + 1, thr, cnt_thr, active

    active0 = jnp.max(jnp.where(exact | (cnt_thr == top_k), 0, 1))
    _, thr, cnt_thr, _ = lax.while_loop(low_cond, low_step, (I32(0), thr, cnt_thr, active0))

    t_sel = jnp.where(cnt_thr == top_k, thr - 1, thr)
    need = (top_k - count_ge(t_sel + 1)).astype(F32)
    t_sel_b = jnp.broadcast_to(t_sel, (TQ, L))
    need_b = jnp.broadcast_to(need, (TQ, L))
    uo = uo_ref[...]

    def mask_chunk(c, run):
        off = pl.multiple_of(c * CK, CK)
        kc = key_ref[:, pl.ds(off, CK)]
        out = []
        for s in range(CK // L):
            kk = kc[:, s * L:(s + 1) * L]
            eq = kk == t_sel_b
            pr = jnp.dot(jnp.where(eq, 1.0, 0.0).astype(BF16), uo, preferred_element_type=F32)
            sel = (kk > t_sel_b) | (eq & (run + pr[:, :L] <= need_b))
            sel = sel & (kk > NEG_INF_KEY)
            out.append(jnp.where(sel, 0.0, -jnp.inf).astype(F32))
            run = run + pr[:, L:]
        key_ref[:, pl.ds(off, CK)] = pltpu.bitcast(jnp.concatenate(out, axis=1), I32)
        return run

    lax.fori_loop(0, n_chunks, mask_chunk, jnp.zeros((TQ, L), F32))

    q2 = q_ref[...] * LOG2E
    q_ext = []
    shift_max = jnp.zeros((TQ, L), F32)
    for h in range(DSA_HEADS):
        blk = q2[:, (h // 2) * L:(h // 2 + 1) * L]
        if h % 2:
            blk = pltpu.roll(blk, DSA_HEAD_DIM, 1)
        qf = jnp.where(lane < DSA_HEAD_DIM, blk, 0.0).astype(BF16).astype(F32)
        bound = jnp.sqrt(jnp.sum(qf * qf, axis=1, keepdims=True) * kmx_ref[h:h + 1, :]) * (1.0 + 2.0 ** -7)
        shift_max = jnp.maximum(shift_max, bound)
        q_ext.append(jnp.where(lane == DSA_HEAD_DIM, bound, qf).astype(BF16))
    minus_one_row = jnp.where(lax.broadcasted_iota(I32, (DSA_HEAD_DIM, CK), 0) == 0, -1.0, 0.0).astype(BF16)
    ones_cols = jnp.ones((CK, L), BF16)

    acc_ref[...] = jnp.zeros(acc_ref.shape, F32)

    def attend(c, biased, fast):
        off = pl.multiple_of(c * CK, CK)
        width = CK
        mb = pltpu.bitcast(key_ref[:, pl.ds(off, width)], F32)
        for h in range(DSA_HEADS):
            k_ext = jnp.concatenate(
                [kt_ref[h * DSA_HEAD_DIM:(h + 1) * DSA_HEAD_DIM, pl.ds(off, width)], minus_one_row], axis=0)
            s = jnp.dot(q_ext[h], k_ext, preferred_element_type=F32) + mb
            if biased:
                def tile(r, j):
                    return tb_ref[h, jnp.clip(first_blk + r - (c * (CK // L) + j), 0, REL_SLABS - 1)]
                s = s + jnp.concatenate(
                    [jnp.concatenate([tile(r, j) for r in range(Q_SUB)], axis=0) for j in range(CK // L)], axis=1)
            pair = (h // 2) * L
            v_ext = jnp.concatenate([v_ref[pl.ds(off, width), pair:pair + L], ones_cols], axis=1)
            if fast:
                p = jnp.exp2(s).astype(BF16)
                acc_ref[h] += jnp.dot(p, v_ext, preferred_element_type=F32)
            else:
                m_old = m_ref[h]
                m_new = jnp.maximum(m_old, jnp.max(s, axis=1, keepdims=True))
                alpha = jnp.exp2(m_old - m_new)
                p = jnp.concatenate(
                    [jnp.exp2(s[:, j * L:(j + 1) * L] - m_new) for j in range(width // L)], axis=1)
                pv = jnp.dot(p.astype(BF16), v_ext, preferred_element_type=F32)
                acc_ref[h] = acc_ref[h] * jnp.concatenate([alpha, alpha], axis=1) + pv
                m_ref[h] = m_new

    def attend_all(fast):
        first_biased = jnp.maximum(first_blk - 1, 0) // (CK // L)

        def far_chunk(c, carry):
            attend(c, False, fast)
            return carry

        def near_chunk(c, carry):
            attend(c, True, fast)
            return carry

        lax.fori_loop(0, first_biased, far_chunk, 0)
        lax.fori_loop(first_biased, n_chunks, near_chunk, 0)

    use_fast = 2.0 * jnp.max(shift_max) + jnp.max(jnp.abs(tb_ref[...])) <= FAST_SOFTMAX_MAX_SHIFT

    @pl.when(use_fast)
    def _():
        attend_all(True)

    @pl.when(jnp.logical_not(use_fast))
    def _():
        m_ref[...] = jnp.full(m_ref.shape, NEG_BIG, F32)
        attend_all(False)

    for p in range(DSA_HEADS // 2):
        a0, a1 = acc_ref[2 * p], acc_ref[2 * p + 1]
        y_ref[:, p * L:(p + 1) * L] = jnp.where(lane < DSA_HEAD_DIM, a0[:, :L] / a0[:, L:], a1[:, :L] / a1[:, L:])


def _dsa(iq, iw, q, kit, kt, v, tb, *, top_k):
    B, S, W = q.shape
    upper = jnp.triu(jnp.ones((LANES, LANES), F32))
    uo = jnp.concatenate([upper, jnp.ones((LANES, LANES), F32)], axis=1).astype(BF16)
    qrow = lambda b, i: (b, i, 0)
    whole = lambda b, i: (b, 0, 0)
    one = pl.Buffered(1)
    return pl.pallas_call(
        functools.partial(_dsa_kernel, top_k=top_k),
        grid=(B, S // Q_BLOCK),
        in_specs=[pl.BlockSpec((None, Q_BLOCK, LANES), qrow),
                  pl.BlockSpec((None, Q_BLOCK, LANES), qrow),
                  pl.BlockSpec((None, Q_BLOCK, W), qrow),
                  pl.BlockSpec((None, IDX_DIM, S), whole, pipeline_mode=one),
                  pl.BlockSpec((None, W, S), whole, pipeline_mode=one),
                  pl.BlockSpec((None, S, W), whole, pipeline_mode=one),
                  pl.BlockSpec((DSA_HEADS, REL_SLABS, LANES, LANES), lambda b, i: (0, 0, 0, 0)),
                  pl.BlockSpec((LANES, 2 * LANES), lambda b, i: (0, 0))],
        out_specs=pl.BlockSpec((None, Q_BLOCK, W), qrow),
        out_shape=jax.ShapeDtypeStruct((B, S, W), F32),
        scratch_shapes=[pltpu.VMEM((Q_BLOCK, S), I32),
                        pltpu.VMEM((SUBLANES, LANES), F32),
                        pltpu.VMEM((DSA_HEADS, Q_BLOCK, LANES), F32),
                        pltpu.VMEM((DSA_HEADS, Q_BLOCK, 2 * LANES), F32)],
        compiler_params=pltpu.CompilerParams(
            dimension_semantics=("parallel", "arbitrary"), vmem_limit_bytes=VMEM_LIMIT_BYTES),
        name="dsa",
    )(iq, iw, q, kit, kt, v, tb, uo)


def _layer_norm(z, g, b):
    mu = jnp.mean(z, axis=-1, keepdims=True)
    zc = z - mu
    var = jnp.mean(zc * zc, axis=-1, keepdims=True)
    return zc * lax.rsqrt(var + LN_EPS) * g + b


def _mix_kernel(x_ref, ya_ref, yb_ref, yc_ref, wa_ref, wb_ref, wc_ref, g_ref, b_ref, o_ref):
    mix = jnp.dot(ya_ref[...].astype(BF16), wa_ref[...], preferred_element_type=F32)
    mix = mix + jnp.dot(yb_ref[...].astype(BF16), wb_ref[...], preferred_element_type=F32)
    mix = mix + jnp.dot(yc_ref[...].astype(BF16), wc_ref[...], preferred_element_type=F32)
    o_ref[...] = _layer_norm(DN_ALPHA * x_ref[...] + mix, g_ref[...], b_ref[...])


def _mix(x, ya, yb, yc, wa, wb, wc, g, b, *, tm):
    T, D = x.shape
    row = lambda m: (m, 0)
    const = lambda m: (0, 0)
    return pl.pallas_call(
        _mix_kernel,
        grid=(T // tm,),
        in_specs=[pl.BlockSpec((tm, D), row),
                  pl.BlockSpec((tm, ya.shape[1]), row), pl.BlockSpec((tm, yb.shape[1]), row),
                  pl.BlockSpec((tm, yc.shape[1]), row),
                  pl.BlockSpec(wa.shape, const), pl.BlockSpec(wb.shape, const), pl.BlockSpec(wc.shape, const),
                  pl.BlockSpec((1, D), const), pl.BlockSpec((1, D), const)],
        out_specs=pl.BlockSpec((tm, D), row),
        out_shape=jax.ShapeDtypeStruct((T, D), F32),
        compiler_params=pltpu.CompilerParams(
            dimension_semantics=("parallel",), vmem_limit_bytes=VMEM_LIMIT_BYTES),
        name="out_proj_ln",
    )(x, ya, yb, yc, wa, wb, wc, g, b)


def _ffn_kernel(x_ref, wg_ref, wu_ref, wd_ref, g_ref, b_ref, o_ref, xb_ref, acc_ref):
    f = pl.program_id(1)

    @pl.when(f == 0)
    def _():
        xb_ref[...] = x_ref[...].astype(BF16)
        acc_ref[...] = jnp.zeros_like(acc_ref)

    xb = xb_ref[...]
    gate = jnp.dot(xb, wg_ref[...], preferred_element_type=F32)
    up = jnp.dot(xb, wu_ref[...], preferred_element_type=F32)
    h = (gate * _sigmoid(gate)) * up
    acc_ref[...] += jnp.dot(h.astype(BF16), wd_ref[...], preferred_element_type=F32)

    @pl.when(f == pl.num_programs(1) - 1)
    def _():
        o_ref[...] = _layer_norm(DN_ALPHA * x_ref[...] + acc_ref[...], g_ref[...], b_ref[...])


def _ffn(x, wg, wu, wd, g, b, *, tm, tf):
    T, D = x.shape
    FF = wg.shape[1]
    row = lambda m, f: (m, 0)
    const = lambda m, f: (0, 0)
    return pl.pallas_call(
        _ffn_kernel,
        grid=(T // tm, FF // tf),
        in_specs=[pl.BlockSpec((tm, D), row),
                  pl.BlockSpec((D, tf), lambda m, f: (0, f)),
                  pl.BlockSpec((D, tf), lambda m, f: (0, f)),
                  pl.BlockSpec((tf, D), lambda m, f: (f, 0)),
                  pl.BlockSpec((1, D), const), pl.BlockSpec((1, D), const)],
        out_specs=pl.BlockSpec((tm, D), row),
        out_shape=jax.ShapeDtypeStruct((T, D), F32),
        scratch_shapes=[pltpu.VMEM((tm, D), BF16), pltpu.VMEM((tm, D), F32)],
        compiler_params=pltpu.CompilerParams(
            dimension_semantics=("parallel", "arbitrary"), vmem_limit_bytes=VMEM_LIMIT_BYTES),
        name="ffn_ln",
    )(x, wg, wu, wd, g, b)


def _split_w_in(w):
    parts = [w[:, IN_OFFSETS[j]:IN_OFFSETS[j + 1]] for j in range(len(IN_SIZES))]
    lru_x, lru_g, gq, gk, gv, gg, g_lr, dq, dk, dv, iq, ik, iw = parts
    D = w.shape[0]
    pad = lambda a, n: jnp.pad(a, ((0, 0), (0, n - a.shape[1])))
    wa = jnp.concatenate([lru_x, lru_g, gq, gk, gv, gg, pad(g_lr, LANES),
                          dq * (DSA_HEAD_DIM ** -0.5), dv, iq, pad(iw, LANES)], axis=1)
    wt = jnp.concatenate([dk, ik], axis=1).T
    assert wa.shape == (D, PA_END) and wt.shape == (PT_ROWS, D)
    return wa.astype(BF16), wt.astype(BF16)


def _block_diag(w):
    n, k, _ = w.shape
    eye = jnp.eye(n, dtype=w.dtype)
    return (eye[:, None, :, None] * w[:, :, None, :]).reshape(n * k, n * k)


def kernel(x, w_in, conv_w, conv_b, lru_wa, lru_ba, lru_wx, lru_bx, lru_lambda, gla_w_gate2, gla_b_gate,
           gla_norm_g, rel_bias, w_out, ln1_g, ln1_b, w_ffn_gate, w_ffn_up, w_ffn_down, ln2_g, ln2_b):
    B, S, D = x.shape
    T = B * S
    top_k = min(IDX_TOPK_MAX, S // 4)
    assert S % (2 * KEY_CHUNK) == 0 and D == D_MODEL
    tm = min(512, S)
    tb = _relbias_tiles(rel_bias)
    row = lambda a: a.reshape(1, -1)
    for l in range(w_in.shape[0]):
        wa, wt = _split_w_in(w_in[l])
        p_lru, p_gla, dq, dv, iq, iw, kt, kit = _proj(x, wa, wt, tm=tm)
        y_lru = _lru(p_lru, conv_w[l], row(conv_b[l]),
                     _block_diag(lru_wa[l]).astype(BF16), row(lru_ba[l]),
                     _block_diag(lru_wx[l]).astype(BF16), row(lru_bx[l]), row(lru_lambda[l]), tl=tm)
        w2 = jnp.pad(gla_w_gate2[l], ((0, LANES - GLA_GATE_RANK), (0, 0)))
        y_gla = _gla(p_gla, w2, row(gla_b_gate[l]), row(gla_norm_g[l]), tg=min(256, S))
        y_dsa = _dsa(iq, iw, dq, kit, kt, dv, tb, top_k=top_k)
        wo = w_out[l].astype(BF16)
        x1 = _mix(x.reshape(T, D), y_lru.reshape(T, -1), y_gla.reshape(T, -1), y_dsa.reshape(T, -1),
                  wo[:LRU_WIDTH], wo[LRU_WIDTH:LRU_WIDTH + GLA_WIDTH], wo[LRU_WIDTH + GLA_WIDTH:],
                  row(ln1_g[l]), row(ln1_b[l]), tm=tm)
        x2 = _ffn(x1, w_ffn_gate[l].astype(BF16), w_ffn_up[l].astype(BF16), w_ffn_down[l].astype(BF16),
                  row(ln2_g[l]), row(ln2_b[l]), tm=tm, tf=D_FF // 2)
        x = x2.reshape(B, S, D)
    return x
```

```python
import functools
import math

import jax
import jax.numpy as jnp
import numpy as np
from jax import lax
from jax.experimental import pallas as pl
from jax.experimental.pallas import tpu as pltpu

F32 = jnp.float32
BF16 = jnp.bfloat16
I32 = jnp.int32

D_MODEL = 1024
DEPTH = 2
LRU_WIDTH = 256
LRU_BLOCKS = 4
LRU_BLOCK = LRU_WIDTH // LRU_BLOCKS
CONV_WIDTH = 4
LRU_C = 8.0
GLA_HEADS = 6
GLA_DK = 64
GLA_DV = 64
GLA_GATE_RANK = 16
GLA_GATE_TAU = 16.0
GLA_CHUNK = 64
GLA_WIDTH = GLA_HEADS * GLA_DV
DSA_HEADS = 6
DSA_HEAD_DIM = 64
DSA_WIDTH = DSA_HEADS * DSA_HEAD_DIM
IDX_HEADS = 4
IDX_DIM = 32
IDX_TOPK_MAX = 256
REL_BUCKETS = 32
REL_MAX_DIST = 128
D_FF = 2816
DN_ALPHA = (2.0 * DEPTH) ** 0.25
LN_EPS = 1e-5

IN_SIZES = (LRU_WIDTH, LRU_WIDTH, GLA_WIDTH, GLA_WIDTH, GLA_WIDTH, GLA_WIDTH, GLA_GATE_RANK,
            DSA_WIDTH, DSA_WIDTH, DSA_WIDTH, IDX_HEADS * IDX_DIM, IDX_DIM, IDX_HEADS)
IN_OFFSETS = [0] + [int(o) for o in np.cumsum(IN_SIZES)]

LANES = 128
SUBLANES = 8
VMEM_LIMIT_BYTES = 56 * 1024 * 1024

LRU_COLS = 2 * LRU_WIDTH
GLA_COLS = 4 * GLA_WIDTH + LANES
PA_LRU = 0
PA_GLA = PA_LRU + LRU_COLS
PA_DQ = PA_GLA + GLA_COLS
PA_DV = PA_DQ + DSA_WIDTH
PA_IQ = PA_DV + DSA_WIDTH
PA_IW = PA_IQ + LANES
PA_END = PA_IW + LANES
PT_ROWS = DSA_WIDTH + IDX_DIM

Q_BLOCK = 256
Q_SUB = Q_BLOCK // LANES
KEY_CHUNK = 512
HALF_BITS = 16
LEAD_BITS = 9
SAMPLE_SHIFT = 4
MAX_LEAD_MOVES = 4
INT_MIN = -2147483648
NEG_INF_KEY = -2139095041
NEG_BIG = -1e30
LOG2E = 1.4426950408889634
REL_SLABS = 3
FAST_SOFTMAX_MAX_SHIFT = 120.0


def _nt_dot(a, b):
    return lax.dot_general(a, b, (((1,), (1,)), ((), ())), preferred_element_type=F32)


def _tn_dot(a, b):
    return lax.dot_general(a, b, (((0,), (0,)), ((), ())), preferred_element_type=F32)


def _softplus(z):
    return jnp.maximum(z, 0.0) + jnp.log(1.0 + jnp.exp(-jnp.abs(z)))


def _sigmoid(z):
    return 1.0 / (1.0 + jnp.exp(-z))


def _proj_kernel(x_ref, wa_ref, wt_ref, lru_ref, gla_ref, q_ref, v_ref, iq_ref, iw_ref, kt_ref, kit_ref):
    xb = x_ref[...].astype(BF16)

    def mm(c0, c1):
        return jnp.dot(xb, wa_ref[:, c0:c1], preferred_element_type=F32)

    lru_ref[...] = mm(PA_LRU, PA_GLA)
    gla_ref[...] = mm(PA_GLA, PA_DQ)
    q_ref[...] = mm(PA_DQ, PA_DV)
    v_ref[...] = mm(PA_DV, PA_IQ).astype(BF16)
    iq_ref[...] = mm(PA_IQ, PA_IW).astype(BF16)
    iw_ref[...] = mm(PA_IW, PA_END)
    t = _nt_dot(wt_ref[...], xb)
    kt_ref[...] = t[:DSA_WIDTH].astype(BF16)
    kit_ref[...] = t[DSA_WIDTH:PT_ROWS].astype(BF16)


def _proj(x, wa, wt, *, tm):
    B, S, D = x.shape
    grid = (B, S // tm)
    row = lambda b, m: (b, m, 0)
    col = lambda b, m: (b, 0, m)
    const = lambda b, m: (0, 0)
    out_shape = (
        jax.ShapeDtypeStruct((B, S, LRU_COLS), F32),
        jax.ShapeDtypeStruct((B, S, GLA_COLS), F32),
        jax.ShapeDtypeStruct((B, S, DSA_WIDTH), F32),
        jax.ShapeDtypeStruct((B, S, DSA_WIDTH), BF16),
        jax.ShapeDtypeStruct((B, S, LANES), BF16),
        jax.ShapeDtypeStruct((B, S, LANES), F32),
        jax.ShapeDtypeStruct((B, DSA_WIDTH, S), BF16),
        jax.ShapeDtypeStruct((B, IDX_DIM, S), BF16),
    )
    out_specs = (
        pl.BlockSpec((None, tm, LRU_COLS), row),
        pl.BlockSpec((None, tm, GLA_COLS), row),
        pl.BlockSpec((None, tm, DSA_WIDTH), row),
        pl.BlockSpec((None, tm, DSA_WIDTH), row),
        pl.BlockSpec((None, tm, LANES), row),
        pl.BlockSpec((None, tm, LANES), row),
        pl.BlockSpec((None, DSA_WIDTH, tm), col),
        pl.BlockSpec((None, IDX_DIM, tm), col),
    )
    return pl.pallas_call(
        _proj_kernel,
        grid=grid,
        in_specs=[pl.BlockSpec((None, tm, D), row),
                  pl.BlockSpec((D, PA_END), const),
                  pl.BlockSpec((PT_ROWS, D), const)],
        out_specs=out_specs,
        out_shape=out_shape,
        compiler_params=pltpu.CompilerParams(
            dimension_semantics=("parallel", "parallel"), vmem_limit_bytes=VMEM_LIMIT_BYTES),
        name="in_proj",
    )(x, wa, wt)


def _shift_rows(x, d, fill, row):
    return jnp.where(row >= d, pltpu.roll(x, d, 0), fill)


def _lru_kernel(p_ref, cw_ref, cb_ref, wa_ref, ba_ref, wx_ref, bx_ref, lam_ref, y_ref, xprev_ref, h_ref):
    tl = p_ref.shape[0]

    @pl.when(pl.program_id(1) == 0)
    def _():
        xprev_ref[...] = jnp.zeros_like(xprev_ref)
        h_ref[...] = jnp.zeros_like(h_ref)

    xb = p_ref[:, 0:LRU_WIDTH]
    gb = p_ref[:, LRU_WIDTH:2 * LRU_WIDTH]
    prev = xprev_ref[...]
    row8 = lax.broadcasted_iota(I32, (SUBLANES, LRU_WIDTH), 0)
    cw = cw_ref[...]
    xc = cb_ref[...] + cw[CONV_WIDTH - 1:CONV_WIDTH] * xb
    for d in range(1, CONV_WIDTH):
        r = pltpu.roll(xb, d, 0)
        top = jnp.where(row8 < d, pltpu.roll(prev, d, 0), r[0:SUBLANES])
        r = jnp.concatenate([top, r[SUBLANES:]], axis=0)
        xc = xc + cw[CONV_WIDTH - 1 - d:CONV_WIDTH - d] * r
    xprev_ref[...] = xb[tl - SUBLANES:tl]

    xcb = xc.astype(BF16)
    r_gate = _sigmoid(jnp.dot(xcb, wa_ref[...], preferred_element_type=F32) + ba_ref[...])
    i_gate = _sigmoid(jnp.dot(xcb, wx_ref[...], preferred_element_type=F32) + bx_ref[...])
    log_a = (-LRU_C) * r_gate * _softplus(-lam_ref[...])
    a = jnp.exp(log_a)
    u = jnp.sqrt(1.0 - a * a) * (i_gate * xc)

    row = lax.broadcasted_iota(I32, (tl, LRU_WIDTH), 0)
    d = 1
    while d < tl:
        a_s = _shift_rows(a, d, 1.0, row)
        u_s = _shift_rows(u, d, 0.0, row)
        u = u + a * u_s
        a = a * a_s
        d *= 2
    h = u + a * h_ref[SUBLANES - 1:SUBLANES, :]
    h_ref[...] = h[tl - SUBLANES:tl]
    y_ref[...] = h * jax.nn.gelu(gb)


def _lru(p_lru, cw, cb, wa, ba, wx, bx, lam, *, tl):
    B, S, _ = p_lru.shape
    const = lambda b, j: (0, 0)
    vec = pl.BlockSpec((1, LRU_WIDTH), const)
    mat = pl.BlockSpec((LRU_WIDTH, LRU_WIDTH), const)
    return pl.pallas_call(
        _lru_kernel,
        grid=(B, S // tl),
        in_specs=[pl.BlockSpec((None, tl, LRU_COLS), lambda b, j: (b, j, 0)),
                  pl.BlockSpec((CONV_WIDTH, LRU_WIDTH), const), vec, mat, vec, mat, vec, vec],
        out_specs=pl.BlockSpec((None, tl, LRU_WIDTH), lambda b, j: (b, j, 0)),
        out_shape=jax.ShapeDtypeStruct((B, S, LRU_WIDTH), F32),
        scratch_shapes=[pltpu.VMEM((SUBLANES, LRU_WIDTH), F32), pltpu.VMEM((SUBLANES, LRU_WIDTH), F32)],
        compiler_params=pltpu.CompilerParams(
            dimension_semantics=("parallel", "arbitrary"), vmem_limit_bytes=VMEM_LIMIT_BYTES),
        name="rg_lru",
    )(p_lru, cw, cb, wa, ba, wx, bx, lam)


def _gla_kernel(q_ref, k_ref, v_ref, g_ref, glr_ref, w2_ref, bg_ref, ng_ref, tri_ref, mean_ref,
                y_ref, st_ref):
    tg = q_ref.shape[0]
    C = GLA_CHUNK
    hi = lax.Precision.HIGHEST

    @pl.when(pl.program_id(1) == 0)
    def _():
        st_ref[...] = jnp.zeros_like(st_ref)

    z = jnp.dot(glr_ref[...], w2_ref[...], preferred_element_type=F32, precision=hi) + bg_ref[...]
    log_alpha = -_softplus(-z) * (1.0 / GLA_GATE_TAU)

    lane = lax.broadcasted_iota(I32, (C, LANES), 1)
    first = lane < GLA_DK
    rr = lax.broadcasted_iota(I32, (C, C), 0)
    cc = lax.broadcasted_iota(I32, (C, C), 1)
    causal = cc <= rr
    r2 = lax.broadcasted_iota(I32, (LANES, LANES), 0)
    c2 = lax.broadcasted_iota(I32, (LANES, LANES), 1)
    same_head = (r2 < GLA_DV) == (c2 < GLA_DK)
    tri = tri_ref[...]

    for c in range(tg // C):
        rows = slice(c * C, (c + 1) * C)
        bcum = jnp.dot(tri, log_alpha[rows], preferred_element_type=F32, precision=hi)
        blast = bcum[C - 1:C]
        kf = k_ref[rows, :]
        q_dec = q_ref[rows, :] * (GLA_DK ** -0.5) * jnp.exp(bcum)
        k_inv = kf * jnp.exp(-bcum)
        k_end = kf * jnp.exp(blast - bcum)
        decay = jnp.exp(blast)
        vf = v_ref[rows, :]
        outs = []
        for p in range(GLA_HEADS // 2):
            cs = slice(p * LANES, (p + 1) * LANES)
            qd, ki, ke, vp = q_dec[:, cs], k_inv[:, cs], k_end[:, cs], vf[:, cs]
            kib = ki.astype(BF16)
            vpb = vp.astype(BF16)
            att0 = _nt_dot(jnp.where(first, qd, 0.0).astype(BF16), kib)
            att1 = _nt_dot(jnp.where(first, 0.0, qd).astype(BF16), kib)
            att0 = jnp.where(causal, att0, 0.0).astype(BF16)
            att1 = jnp.where(causal, att1, 0.0).astype(BF16)
            o_intra = jnp.where(first,
                                jnp.dot(att0, vpb, preferred_element_type=F32),
                                jnp.dot(att1, vpb, preferred_element_type=F32))
            st = st_ref[p]
            o_inter = _nt_dot(qd.astype(BF16), st.astype(BF16))
            u_t = _tn_dot(vpb, ke.astype(BF16))
            st_ref[p] = st * decay[:, cs] + jnp.where(same_head, u_t, 0.0)
            outs.append(o_intra + o_inter)
        o = jnp.concatenate(outs, axis=1)
        ms = jnp.dot(o * o, mean_ref[...], preferred_element_type=F32, precision=hi)
        o = o * lax.rsqrt(ms + 1e-6) * ng_ref[...]
        gf = g_ref[rows, :]
        y_ref[rows, :] = o * (gf * _sigmoid(gf))


def _gla(p_gla, w2, bg, ng, *, tg):
    B, S, _ = p_gla.shape
    W = GLA_WIDTH
    const = lambda b, j: (0, 0)
    tri = jnp.tril(jnp.ones((GLA_CHUNK, GLA_CHUNK), F32))
    head = jnp.arange(W) // GLA_DV
    mean_blk = (head[:, None] == head[None, :]).astype(F32) / GLA_DV

    def colblk(c):
        return pl.BlockSpec((None, tg, W), lambda b, j: (b, j, c))

    return pl.pallas_call(
        _gla_kernel,
        grid=(B, S // tg),
        in_specs=[colblk(0), colblk(1), colblk(2), colblk(3),
                  pl.BlockSpec((None, tg, LANES), lambda b, j: (b, j, 4 * W // LANES)),
                  pl.BlockSpec((LANES, W), const),
                  pl.BlockSpec((1, W), const), pl.BlockSpec((1, W), const),
                  pl.BlockSpec((GLA_CHUNK, GLA_CHUNK), const),
                  pl.BlockSpec((W, W), const)],
        out_specs=pl.BlockSpec((None, tg, W), lambda b, j: (b, j, 0)),
        out_shape=jax.ShapeDtypeStruct((B, S, W), F32),
        scratch_shapes=[pltpu.VMEM((GLA_HEADS // 2, LANES, LANES), F32)],
        compiler_params=pltpu.CompilerParams(
            dimension_semantics=("parallel", "arbitrary"), vmem_limit_bytes=VMEM_LIMIT_BYTES),
        name="gla",
    )(p_gla, p_gla, p_gla, p_gla, p_gla, w2, bg, ng, tri, mean_blk)


def _relbias_kernel(rb_ref, tb_ref):
    r = lax.broadcasted_iota(I32, (LANES, LANES), 0)
    c = lax.broadcasted_iota(I32, (LANES, LANES), 1)
    max_exact = REL_BUCKETS // 2
    for delta in range(REL_SLABS):
        n = jnp.maximum(delta * LANES + r - c, 0)
        nf = jnp.maximum(n, 1).astype(F32)
        large = max_exact + (jnp.log(nf / max_exact) / math.log(REL_MAX_DIST / max_exact)
                             * (REL_BUCKETS - max_exact)).astype(I32)
        large = jnp.minimum(large, REL_BUCKETS - 1)
        bucket = jnp.where(n < max_exact, n, large)
        for h in range(DSA_HEADS):
            acc = jnp.zeros((LANES, LANES), F32)
            for b in range(REL_BUCKETS):
                acc = jnp.where(bucket == b, rb_ref[b, h], acc)
            tb_ref[h, delta] = (acc - rb_ref[REL_BUCKETS - 1, h]) * LOG2E


def _relbias_tiles(rel_bias):
    return pl.pallas_call(
        _relbias_kernel,
        in_specs=[pl.BlockSpec(memory_space=pltpu.SMEM)],
        out_specs=pl.BlockSpec(memory_space=pltpu.VMEM),
        out_shape=jax.ShapeDtypeStruct((DSA_HEADS, REL_SLABS, LANES, LANES), F32),
        name="rel_bias_tiles",
    )(rel_bias)


def _dsa_kernel(iq_ref, iw_ref, q_ref, kit_ref, kt_ref, v_ref, tb_ref, uo_ref, y_ref,
                key_ref, kmx_ref, m_ref, acc_ref, *, top_k):
    i = pl.program_id(1)
    TQ, CK, L = Q_BLOCK, KEY_CHUNK, LANES
    S = key_ref.shape[1]
    first_blk = i * Q_SUB
    n_chunks = (first_blk + Q_SUB - 1) // (CK // L) + 1
    lane = lax.broadcasted_iota(I32, (TQ, L), 1)

    @pl.when(i == 0)
    def _():
        nb = 2 * CK
        for h in range(DSA_HEADS):
            def norm_chunk(c, mx):
                kk = kt_ref[h * DSA_HEAD_DIM:(h + 1) * DSA_HEAD_DIM, pl.ds(pl.multiple_of(c * nb, nb), nb)]
                kk = kk.astype(F32)
                return jnp.maximum(mx, jnp.sum(kk * kk, axis=0, keepdims=True))
            mx = lax.fori_loop(0, S // nb, norm_chunk, jnp.zeros((1, nb), F32))
            kmx_ref[h:h + 1, :] = jnp.broadcast_to(jnp.max(mx, axis=1, keepdims=True), (1, L))

    iq = iq_ref[...]
    iq_h = [iq[:, h * IDX_DIM:(h + 1) * IDX_DIM] for h in range(IDX_HEADS)]
    scale = (IDX_DIM ** -0.5) * (IDX_HEADS ** -0.5)
    w_h = [jnp.broadcast_to(iw_ref[:, h:h + 1] * scale, (TQ, CK)) for h in range(IDX_HEADS)]

    def score_chunk(c, masked):
        off = pl.multiple_of(c * CK, CK)
        kc = kit_ref[:, pl.ds(off, CK)]
        sc = jnp.zeros((TQ, CK), F32)
        for h in range(IDX_HEADS):
            z = jnp.dot(iq_h[h], kc, preferred_element_type=F32)
            sc = sc + jnp.maximum(z, 0.0) * w_h[h]
        sc = sc + 0.0
        if masked:
            row_t = i * TQ + lax.broadcasted_iota(I32, (TQ, CK), 0)
            pos = off + lax.broadcasted_iota(I32, (TQ, CK), 1)
            sc = jnp.where(pos <= row_t, sc, -jnp.inf)
        bits = pltpu.bitcast(sc, I32)
        key_ref[:, pl.ds(off, CK)] = bits ^ ((bits >> 31) & I32(0x7FFFFFFF))

    def score_body(c, carry):
        score_chunk(c, False)
        return carry

    lax.fori_loop(0, n_chunks - 1, score_body, 0)
    score_chunk(n_chunks - 1, True)

    def count_ge(cand):
        cand_all = jnp.broadcast_to(cand, (TQ, L))
        accs = []
        for r in range(Q_SUB):
            rows = slice(r * L, (r + 1) * L)
            cand_b = cand_all[rows]

            def body(c, acc, rows=rows, cand_b=cand_b):
                kk = key_ref[rows, pl.ds(pl.multiple_of(c * CK, CK), CK)]
                for s in range(CK // L):
                    acc = acc + (kk[:, s * L:(s + 1) * L] >= cand_b).astype(I32)
                return acc
            accs.append(lax.fori_loop(0, n_chunks, body, jnp.zeros((L, L), I32)))
        acc = accs[0] if Q_SUB == 1 else jnp.concatenate(accs, axis=0)
        return jnp.sum(acc, axis=1, keepdims=True)

    def bit_step(thr, cnt_thr, bit):
        cand = thr + bit
        cnt = count_ge(cand)
        ok = cnt >= top_k
        return jnp.where(ok, cand, thr), jnp.where(ok, cnt, cnt_thr)

    def high_step(b, st):
        return bit_step(st[0], st[1], lax.shift_left(I32(1), I32(31) - b))

    start = (jnp.full((TQ, 1), INT_MIN, I32), jnp.full((TQ, 1), S, I32))

    n_samp = (first_blk >> SAMPLE_SHIFT) + 1

    def count_sample(cand):
        cand_b = jnp.broadcast_to(cand, (TQ, L))

        def body(c, acc):
            off = pl.multiple_of(c * (L << SAMPLE_SHIFT), L << SAMPLE_SHIFT)
            return acc + (key_ref[:, pl.ds(off, L)] >= cand_b).astype(I32)
        acc = lax.fori_loop(0, n_samp, body, jnp.zeros((TQ, L), I32))
        return jnp.sum(acc, axis=1, keepdims=True)

    def guess_step(b, g):
        cand = g + lax.shift_left(I32(1), I32(31) - b)
        ok = count_sample(cand) * (first_blk + Q_SUB) >= top_k * n_samp
        return jnp.where(ok, cand, g)

    guess = lax.fori_loop(0, LEAD_BITS, guess_step, start[0])
    lead_step = I32(1 << (32 - LEAD_BITS))

    def settle_cond(st):
        return (st[0] < MAX_LEAD_MOVES) & (st[3] > 0)

    def settle_step(st):
        n, thr, _, _ = st
        c_here = count_ge(thr)
        too_high = c_here < top_k
        too_low = jnp.logical_not(too_high) & (count_ge(thr + lead_step) >= top_k)
        thr = jnp.where(too_high, thr - lead_step, jnp.where(too_low, thr + lead_step, thr))
        return n + 1, thr, c_here, jnp.max(jnp.where(too_high | too_low, 1, 0))

    _, thr, cnt_thr, unsettled = lax.while_loop(settle_cond, settle_step, (I32(0), guess, start[1], I32(1)))

    thr, cnt_thr = lax.cond(
        unsettled > 0,
        lambda: lax.fori_loop(0, LEAD_BITS, high_step, start),
        lambda: (thr, cnt_thr))
    thr, cnt_thr = lax.fori_loop(LEAD_BITS, 32 - HALF_BITS, high_step, (thr, cnt_thr))

    cnt_next = count_ge(thr + 1)
    exact = cnt_next < top_k

    def low_cond(st):
        b, _, _, active = st
        return (b < HALF_BITS) & (active > 0)

    def low_step(st):
        b, thr, cnt_thr, _ = st
        new_thr, new_cnt = bit_step(thr, cnt_thr, lax.shift_left(I32(1), I32(HALF_BITS - 1) - b))
        thr = jnp.where(exact, thr, new_thr)
        cnt_thr = jnp.where(exact, cnt_thr, new_cnt)
        active = jnp.max(jnp.where(exact | (cnt_thr == top_k), 0, 1))
        return b + 1, thr, cnt_thr, active

    active0 = jnp.max(jnp.where(exact | (cnt_thr == top_k), 0, 1))
    _, thr, cnt_thr, _ = lax.while_loop(low_cond, low_step, (I32(0), thr, cnt_thr, active0))

    t_sel = jnp.where(cnt_thr == top_k, thr - 1, thr)
    need = (top_k - count_ge(t_sel + 1)).astype(F32)
    t_sel_b = jnp.broadcast_to(t_sel, (TQ, L))
    need_b = jnp.broadcast_to(need, (TQ, L))
    uo = uo_ref[...]

    def mask_chunk(c, run):
        off = pl.multiple_of(c * CK, CK)
        kc = key_ref[:, pl.ds(off, CK)]
        out = []
        for s in range(CK // L):
            kk = kc[:, s * L:(s + 1) * L]
            eq = kk == t_sel_b
            pr = jnp.dot(jnp.where(eq, 1.0, 0.0).astype(BF16), uo, preferred_element_type=F32)
            sel = (kk > t_sel_b) | (eq & (run + pr[:, :L] <= need_b))
            sel = sel & (kk > NEG_INF_KEY)
            out.append(jnp.where(sel, 0.0, -jnp.inf).astype(F32))
            run = run + pr[:, L:]
        key_ref[:, pl.ds(off, CK)] = pltpu.bitcast(jnp.concatenate(out, axis=1), I32)
        return run

    lax.fori_loop(0, n_chunks, mask_chunk, jnp.zeros((TQ, L), F32))

    q2 = q_ref[...] * LOG2E
    q_ext = []
    shift_max = jnp.zeros((TQ, L), F32)
    for h in range(DSA_HEADS):
        blk = q2[:, (h // 2) * L:(h // 2 + 1) * L]
        if h % 2:
            blk = pltpu.roll(blk, DSA_HEAD_DIM, 1)
        qf = jnp.where(lane < DSA_HEAD_DIM, blk, 0.0).astype(BF16).astype(F32)
        bound = jnp.sqrt(jnp.sum(qf * qf, axis=1, keepdims=True) * kmx_ref[h:h + 1, :]) * (1.0 + 2.0 ** -7)
        shift_max = jnp.maximum(shift_max, bound)
        q_ext.append(jnp.where(lane == DSA_HEAD_DIM, bound, qf).astype(BF16))
    minus_one_row = jnp.where(lax.broadcasted_iota(I32, (DSA_HEAD_DIM, CK), 0) == 0, -1.0, 0.0).astype(BF16)
    ones_cols = jnp.ones((CK, L), BF16)

    acc_ref[...] = jnp.zeros(acc_ref.shape, F32)

    def attend(c, biased, fast):
        off = pl.multiple_of(c * CK, CK)
        width = CK
        mb = pltpu.bitcast(key_ref[:, pl.ds(off, width)], F32)
        for h in range(DSA_HEADS):
            k_ext = jnp.concatenate(
                [kt_ref[h * DSA_HEAD_DIM:(h + 1) * DSA_HEAD_DIM, pl.ds(off, width)], minus_one_row], axis=0)
            s = jnp.dot(q_ext[h], k_ext, preferred_element_type=F32) + mb
            if biased:
                def tile(r, j):
                    return tb_ref[h, jnp.clip(first_blk + r - (c * (CK // L) + j), 0, REL_SLABS - 1)]
                s = s + jnp.concatenate(
                    [jnp.concatenate([tile(r, j) for r in range(Q_SUB)], axis=0) for j in range(CK // L)], axis=1)
            pair = (h // 2) * L
            v_ext = jnp.concatenate([v_ref[pl.ds(off, width), pair:pair + L], ones_cols], axis=1)
            if fast:
                p = jnp.exp2(s).astype(BF16)
                acc_ref[h] += jnp.dot(p, v_ext, preferred_element_type=F32)
            else:
                m_old = m_ref[h]
                m_new = jnp.maximum(m_old, jnp.max(s, axis=1, keepdims=True))
                alpha = jnp.exp2(m_old - m_new)
                p = jnp.concatenate(
                    [jnp.exp2(s[:, j * L:(j + 1) * L] - m_new) for j in range(width // L)], axis=1)
                pv = jnp.dot(p.astype(BF16), v_ext, preferred_element_type=F32)
                acc_ref[h] = acc_ref[h] * jnp.concatenate([alpha, alpha], axis=1) + pv
                m_ref[h] = m_new

    def attend_all(fast):
        first_biased = jnp.maximum(first_blk - 1, 0) // (CK // L)

        def far_chunk(c, carry):
            attend(c, False, fast)
            return carry

        def near_chunk(c, carry):
            attend(c, True, fast)
            return carry

        lax.fori_loop(0, first_biased, far_chunk, 0)
        lax.fori_loop(first_biased, n_chunks, near_chunk, 0)

    use_fast = 2.0 * jnp.max(shift_max) + jnp.max(jnp.abs(tb_ref[...])) <= FAST_SOFTMAX_MAX_SHIFT

    @pl.when(use_fast)
    def _():
        attend_all(True)

    @pl.when(jnp.logical_not(use_fast))
    def _():
        m_ref[...] = jnp.full(m_ref.shape, NEG_BIG, F32)
        attend_all(False)

    for p in range(DSA_HEADS // 2):
        a0, a1 = acc_ref[2 * p], acc_ref[2 * p + 1]
        y_ref[:, p * L:(p + 1) * L] = jnp.where(lane < DSA_HEAD_DIM, a0[:, :L] / a0[:, L:], a1[:, :L] / a1[:, L:])


def _dsa(iq, iw, q, kit, kt, v, tb, *, top_k):
    B, S, W = q.shape
    upper = jnp.triu(jnp.ones((LANES, LANES), F32))
    uo = jnp.concatenate([upper, jnp.ones((LANES, LANES), F32)], axis=1).astype(BF16)
    qrow = lambda b, i: (b, i, 0)
    whole = lambda b, i: (b, 0, 0)
    one = pl.Buffered(1)
    return pl.pallas_call(
        functools.partial(_dsa_kernel, top_k=top_k),
        grid=(B, S // Q_BLOCK),
        in_specs=[pl.BlockSpec((None, Q_BLOCK, LANES), qrow),
                  pl.BlockSpec((None, Q_BLOCK, LANES), qrow),
                  pl.BlockSpec((None, Q_BLOCK, W), qrow),
                  pl.BlockSpec((None, IDX_DIM, S), whole, pipeline_mode=one),
                  pl.BlockSpec((None, W, S), whole, pipeline_mode=one),
                  pl.BlockSpec((None, S, W), whole, pipeline_mode=one),
                  pl.BlockSpec((DSA_HEADS, REL_SLABS, LANES, LANES), lambda b, i: (0, 0, 0, 0)),
                  pl.BlockSpec((LANES, 2 * LANES), lambda b, i: (0, 0))],
        out_specs=pl.BlockSpec((None, Q_BLOCK, W), qrow),
        out_shape=jax.ShapeDtypeStruct((B, S, W), F32),
        scratch_shapes=[pltpu.VMEM((Q_BLOCK, S), I32),
                        pltpu.VMEM((SUBLANES, LANES), F32),
                        pltpu.VMEM((DSA_HEADS, Q_BLOCK, LANES), F32),
                        pltpu.VMEM((DSA_HEADS, Q_BLOCK, 2 * LANES), F32)],
        compiler_params=pltpu.CompilerParams(
            dimension_semantics=("parallel", "arbitrary"), vmem_limit_bytes=VMEM_LIMIT_BYTES),
        name="dsa",
    )(iq, iw, q, kit, kt, v, tb, uo)


def _layer_norm(z, g, b):
    mu = jnp.mean(z, axis=-1, keepdims=True)
    zc = z - mu
    var = jnp.mean(zc * zc, axis=-1, keepdims=True)
    return zc * lax.rsqrt(var + LN_EPS) * g + b


def _mix_kernel(x_ref, ya_ref, yb_ref, yc_ref, wa_ref, wb_ref, wc_ref, g_ref, b_ref, o_ref):
    mix = jnp.dot(ya_ref[...].astype(BF16), wa_ref[...], preferred_element_type=F32)
    mix = mix + jnp.dot(yb_ref[...].astype(BF16), wb_ref[...], preferred_element_type=F32)
    mix = mix + jnp.dot(yc_ref[...].astype(BF16), wc_ref[...], preferred_element_type=F32)
    o_ref[...] = _layer_norm(DN_ALPHA * x_ref[...] + mix, g_ref[...], b_ref[...])


def _mix(x, ya, yb, yc, wa, wb, wc, g, b, *, tm):
    T, D = x.shape
    row = lambda m: (m, 0)
    const = lambda m: (0, 0)
    return pl.pallas_call(
        _mix_kernel,
        grid=(T // tm,),
        in_specs=[pl.BlockSpec((tm, D), row),
                  pl.BlockSpec((tm, ya.shape[1]), row), pl.BlockSpec((tm, yb.shape[1]), row),
                  pl.BlockSpec((tm, yc.shape[1]), row),
                  pl.BlockSpec(wa.shape, const), pl.BlockSpec(wb.shape, const), pl.BlockSpec(wc.shape, const),
                  pl.BlockSpec((1, D), const), pl.BlockSpec((1, D), const)],
        out_specs=pl.BlockSpec((tm, D), row),
        out_shape=jax.ShapeDtypeStruct((T, D), F32),
        compiler_params=pltpu.CompilerParams(
            dimension_semantics=("parallel",), vmem_limit_bytes=VMEM_LIMIT_BYTES),
        name="out_proj_ln",
    )(x, ya, yb, yc, wa, wb, wc, g, b)


def _ffn_kernel(x_ref, wg_ref, wu_ref, wd_ref, g_ref, b_ref, o_ref, xb_ref, acc_ref):
    f = pl.program_id(1)

    @pl.when(f == 0)
    def _():
        xb_ref[...] = x_ref[...].astype(BF16)
        acc_ref[...] = jnp.zeros_like(acc_ref)

    xb = xb_ref[...]
    gate = jnp.dot(xb, wg_ref[...], preferred_element_type=F32)
    up = jnp.dot(xb, wu_ref[...], preferred_element_type=F32)
    h = (gate * _sigmoid(gate)) * up
    acc_ref[...] += jnp.dot(h.astype(BF16), wd_ref[...], preferred_element_type=F32)

    @pl.when(f == pl.num_programs(1) - 1)
    def _():
        o_ref[...] = _layer_norm(DN_ALPHA * x_ref[...] + acc_ref[...], g_ref[...], b_ref[...])


def _ffn(x, wg, wu, wd, g, b, *, tm, tf):
    T, D = x.shape
    FF = wg.shape[1]
    row = lambda m, f: (m, 0)
    const = lambda m, f: (0, 0)
    return pl.pallas_call(
        _ffn_kernel,
        grid=(T // tm, FF // tf),
        in_specs=[pl.BlockSpec((tm, D), row),
                  pl.BlockSpec((D, tf), lambda m, f: (0, f)),
                  pl.BlockSpec((D, tf), lambda m, f: (0, f)),
                  pl.BlockSpec((tf, D), lambda m, f: (f, 0)),
                  pl.BlockSpec((1, D), const), pl.BlockSpec((1, D), const)],
        out_specs=pl.BlockSpec((tm, D), row),
        out_shape=jax.ShapeDtypeStruct((T, D), F32),
        scratch_shapes=[pltpu.VMEM((tm, D), BF16), pltpu.VMEM((tm, D), F32)],
        compiler_params=pltpu.CompilerParams(
            dimension_semantics=("parallel", "arbitrary"), vmem_limit_bytes=VMEM_LIMIT_BYTES),
        name="ffn_ln",
    )(x, wg, wu, wd, g, b)


def _split_w_in(w):
    parts = [w[:, IN_OFFSETS[j]:IN_OFFSETS[j + 1]] for j in range(len(IN_SIZES))]
    lru_x, lru_g, gq, gk, gv, gg, g_lr, dq, dk, dv, iq, ik, iw = parts
    D = w.shape[0]
    pad = lambda a, n: jnp.pad(a, ((0, 0), (0, n - a.shape[1])))
    wa = jnp.concatenate([lru_x, lru_g, gq, gk, gv, gg, pad(g_lr, LANES),
                          dq * (DSA_HEAD_DIM ** -0.5), dv, iq, pad(iw, LANES)], axis=1)
    wt = jnp.concatenate([dk, ik], axis=1).T
    assert wa.shape == (D, PA_END) and wt.shape == (PT_ROWS, D)
    return wa.astype(BF16), wt.astype(BF16)


def _block_diag(w):
    n, k, _ = w.shape
    eye = jnp.eye(n, dtype=w.dtype)
    return (eye[:, None, :, None] * w[:, :, None, :]).reshape(n * k, n * k)


def kernel(x, w_in, conv_w, conv_b, lru_wa, lru_ba, lru_wx, lru_bx, lru_lambda, gla_w_gate2, gla_b_gate,
           gla_norm_g, rel_bias, w_out, ln1_g, ln1_b, w_ffn_gate, w_ffn_up, w_ffn_down, ln2_g, ln2_b):
    B, S, D = x.shape
    T = B * S
    top_k = min(IDX_TOPK_MAX, S // 4)
    assert S % (2 * KEY_CHUNK) == 0 and D == D_MODEL
    tm = min(512, S)
    tb = _relbias_tiles(rel_bias)
    row = lambda a: a.reshape(1, -1)
    for l in range(w_in.shape[0]):
        wa, wt = _split_w_in(w_in[l])
        p_lru, p_gla, dq, dv, iq, iw, kt, kit = _proj(x, wa, wt, tm=tm)
        y_lru = _lru(p_lru, conv_w[l], row(conv_b[l]),
                     _block_diag(lru_wa[l]).astype(BF16), row(lru_ba[l]),
                     _block_diag(lru_wx[l]).astype(BF16), row(lru_bx[l]), row(lru_lambda[l]), tl=tm)
        w2 = jnp.pad(gla_w_gate2[l], ((0, LANES - GLA_GATE_RANK), (0, 0)))
        y_gla = _gla(p_gla, w2, row(gla_b_gate[l]), row(gla_norm_g[l]), tg=min(256, S))
        y_dsa = _dsa(iq, iw, dq, kit, kt, dv, tb, top_k=top_k)
        wo = w_out[l].astype(BF16)
        x1 = _mix(x.reshape(T, D), y_lru.reshape(T, -1), y_gla.reshape(T, -1), y_dsa.reshape(T, -1),
                  wo[:LRU_WIDTH], wo[LRU_WIDTH:LRU_WIDTH + GLA_WIDTH], wo[LRU_WIDTH + GLA_WIDTH:],
                  row(ln1_g[l]), row(ln1_b[l]), tm=tm)
        x2 = _ffn(x1, w_ffn_gate[l].astype(BF16), w_ffn_up[l].astype(BF16), w_ffn_down[l].astype(BF16),
                  row(ln2_g[l]), row(ln2_b[l]), tm=tm, tf=D_FF // 2)
        x = x2.reshape(B, S, D)
    return x
```

```python
import functools
import math

import jax
import jax.numpy as jnp
import numpy as np
from jax import lax
from jax.experimental import pallas as pl
from jax.experimental.pallas import tpu as pltpu

F32 = jnp.float32
BF16 = jnp.bfloat16
I32 = jnp.int32

D_MODEL = 1024
DEPTH = 2
LRU_WIDTH = 256
LRU_BLOCKS = 4
LRU_BLOCK = LRU_WIDTH // LRU_BLOCKS
CONV_WIDTH = 4
LRU_C = 8.0
GLA_HEADS = 6
GLA_DK = 64
GLA_DV = 64
GLA_GATE_RANK = 16
GLA_GATE_TAU = 16.0
GLA_CHUNK = 64
GLA_WIDTH = GLA_HEADS * GLA_DV
DSA_HEADS = 6
DSA_HEAD_DIM = 64
DSA_WIDTH = DSA_HEADS * DSA_HEAD_DIM
IDX_HEADS = 4
IDX_DIM = 32
IDX_TOPK_MAX = 256
REL_BUCKETS = 32
REL_MAX_DIST = 128
D_FF = 2816
DN_ALPHA = (2.0 * DEPTH) ** 0.25
LN_EPS = 1e-5

IN_SIZES = (LRU_WIDTH, LRU_WIDTH, GLA_WIDTH, GLA_WIDTH, GLA_WIDTH, GLA_WIDTH, GLA_GATE_RANK,
            DSA_WIDTH, DSA_WIDTH, DSA_WIDTH, IDX_HEADS * IDX_DIM, IDX_DIM, IDX_HEADS)
IN_OFFSETS = [0] + [int(o) for o in np.cumsum(IN_SIZES)]

LANES = 128
SUBLANES = 8
VMEM_LIMIT_BYTES = 56 * 1024 * 1024

LRU_COLS = 2 * LRU_WIDTH
GLA_COLS = 4 * GLA_WIDTH + LANES
PA_LRU = 0
PA_GLA = PA_LRU + LRU_COLS
PA_DQ = PA_GLA + GLA_COLS
PA_DV = PA_DQ + DSA_WIDTH
PA_IQ = PA_DV + DSA_WIDTH
PA_IW = PA_IQ + LANES
PA_END = PA_IW + LANES
PT_ROWS = DSA_WIDTH + IDX_DIM

Q_BLOCK = 256
Q_SUB = Q_BLOCK // LANES
KEY_CHUNK = 512
HALF_BITS = 16
INT_MIN = -2147483648
NEG_INF_KEY = -2139095041
NEG_BIG = -1e30
LOG2E = 1.4426950408889634
REL_SLABS = 3
FAST_SOFTMAX_MAX_SHIFT = 120.0


def _nt_dot(a, b):
    return lax.dot_general(a, b, (((1,), (1,)), ((), ())), preferred_element_type=F32)


def _tn_dot(a, b):
    return lax.dot_general(a, b, (((0,), (0,)), ((), ())), preferred_element_type=F32)


def _softplus(z):
    return jnp.maximum(z, 0.0) + jnp.log(1.0 + jnp.exp(-jnp.abs(z)))


def _sigmoid(z):
    return 1.0 / (1.0 + jnp.exp(-z))


def _proj_kernel(x_ref, wa_ref, wt_ref, lru_ref, gla_ref, q_ref, v_ref, iq_ref, iw_ref, kt_ref, kit_ref):
    xb = x_ref[...].astype(BF16)

    def mm(c0, c1):
        return jnp.dot(xb, wa_ref[:, c0:c1], preferred_element_type=F32)

    lru_ref[...] = mm(PA_LRU, PA_GLA)
    gla_ref[...] = mm(PA_GLA, PA_DQ)
    q_ref[...] = mm(PA_DQ, PA_DV)
    v_ref[...] = mm(PA_DV, PA_IQ).astype(BF16)
    iq_ref[...] = mm(PA_IQ, PA_IW).astype(BF16)
    iw_ref[...] = mm(PA_IW, PA_END)
    t = _nt_dot(wt_ref[...], xb)
    kt_ref[...] = t[:DSA_WIDTH].astype(BF16)
    kit_ref[...] = t[DSA_WIDTH:PT_ROWS].astype(BF16)


def _proj(x, wa, wt, *, tm):
    B, S, D = x.shape
    grid = (B, S // tm)
    row = lambda b, m: (b, m, 0)
    col = lambda b, m: (b, 0, m)
    const = lambda b, m: (0, 0)
    out_shape = (
        jax.ShapeDtypeStruct((B, S, LRU_COLS), F32),
        jax.ShapeDtypeStruct((B, S, GLA_COLS), F32),
        jax.ShapeDtypeStruct((B, S, DSA_WIDTH), F32),
        jax.ShapeDtypeStruct((B, S, DSA_WIDTH), BF16),
        jax.ShapeDtypeStruct((B, S, LANES), BF16),
        jax.ShapeDtypeStruct((B, S, LANES), F32),
        jax.ShapeDtypeStruct((B, DSA_WIDTH, S), BF16),
        jax.ShapeDtypeStruct((B, IDX_DIM, S), BF16),
    )
    out_specs = (
        pl.BlockSpec((None, tm, LRU_COLS), row),
        pl.BlockSpec((None, tm, GLA_COLS), row),
        pl.BlockSpec((None, tm, DSA_WIDTH), row),
        pl.BlockSpec((None, tm, DSA_WIDTH), row),
        pl.BlockSpec((None, tm, LANES), row),
        pl.BlockSpec((None, tm, LANES), row),
        pl.BlockSpec((None, DSA_WIDTH, tm), col),
        pl.BlockSpec((None, IDX_DIM, tm), col),
    )
    return pl.pallas_call(
        _proj_kernel,
        grid=grid,
        in_specs=[pl.BlockSpec((None, tm, D), row),
                  pl.BlockSpec((D, PA_END), const),
                  pl.BlockSpec((PT_ROWS, D), const)],
        out_specs=out_specs,
        out_shape=out_shape,
        compiler_params=pltpu.CompilerParams(
            dimension_semantics=("parallel", "parallel"), vmem_limit_bytes=VMEM_LIMIT_BYTES),
        name="in_proj",
    )(x, wa, wt)


def _shift_rows(x, d, fill, row):
    return jnp.where(row >= d, pltpu.roll(x, d, 0), fill)


def _lru_kernel(p_ref, cw_ref, cb_ref, wa_ref, ba_ref, wx_ref, bx_ref, lam_ref, y_ref, xprev_ref, h_ref):
    tl = p_ref.shape[0]

    @pl.when(pl.program_id(1) == 0)
    def _():
        xprev_ref[...] = jnp.zeros_like(xprev_ref)
        h_ref[...] = jnp.zeros_like(h_ref)

    xb = p_ref[:, 0:LRU_WIDTH]
    gb = p_ref[:, LRU_WIDTH:2 * LRU_WIDTH]
    prev = xprev_ref[...]
    row8 = lax.broadcasted_iota(I32, (SUBLANES, LRU_WIDTH), 0)
    cw = cw_ref[...]
    xc = cb_ref[...] + cw[CONV_WIDTH - 1:CONV_WIDTH] * xb
    for d in range(1, CONV_WIDTH):
        r = pltpu.roll(xb, d, 0)
        top = jnp.where(row8 < d, pltpu.roll(prev, d, 0), r[0:SUBLANES])
        r = jnp.concatenate([top, r[SUBLANES:]], axis=0)
        xc = xc + cw[CONV_WIDTH - 1 - d:CONV_WIDTH - d] * r
    xprev_ref[...] = xb[tl - SUBLANES:tl]

    xcb = xc.astype(BF16)
    r_gate = _sigmoid(jnp.dot(xcb, wa_ref[...], preferred_element_type=F32) + ba_ref[...])
    i_gate = _sigmoid(jnp.dot(xcb, wx_ref[...], preferred_element_type=F32) + bx_ref[...])
    log_a = (-LRU_C) * r_gate * _softplus(-lam_ref[...])
    a = jnp.exp(log_a)
    u = jnp.sqrt(1.0 - a * a) * (i_gate * xc)

    row = lax.broadcasted_iota(I32, (tl, LRU_WIDTH), 0)
    d = 1
    while d < tl:
        a_s = _shift_rows(a, d, 1.0, row)
        u_s = _shift_rows(u, d, 0.0, row)
        u = u + a * u_s
        a = a * a_s
        d *= 2
    h = u + a * h_ref[SUBLANES - 1:SUBLANES, :]
    h_ref[...] = h[tl - SUBLANES:tl]
    y_ref[...] = h * jax.nn.gelu(gb)


def _lru(p_lru, cw, cb, wa, ba, wx, bx, lam, *, tl):
    B, S, _ = p_lru.shape
    const = lambda b, j: (0, 0)
    vec = pl.BlockSpec((1, LRU_WIDTH), const)
    mat = pl.BlockSpec((LRU_WIDTH, LRU_WIDTH), const)
    return pl.pallas_call(
        _lru_kernel,
        grid=(B, S // tl),
        in_specs=[pl.BlockSpec((None, tl, LRU_COLS), lambda b, j: (b, j, 0)),
                  pl.BlockSpec((CONV_WIDTH, LRU_WIDTH), const), vec, mat, vec, mat, vec, vec],
        out_specs=pl.BlockSpec((None, tl, LRU_WIDTH), lambda b, j: (b, j, 0)),
        out_shape=jax.ShapeDtypeStruct((B, S, LRU_WIDTH), F32),
        scratch_shapes=[pltpu.VMEM((SUBLANES, LRU_WIDTH), F32), pltpu.VMEM((SUBLANES, LRU_WIDTH), F32)],
        compiler_params=pltpu.CompilerParams(
            dimension_semantics=("parallel", "arbitrary"), vmem_limit_bytes=VMEM_LIMIT_BYTES),
        name="rg_lru",
    )(p_lru, cw, cb, wa, ba, wx, bx, lam)


def _gla_kernel(q_ref, k_ref, v_ref, g_ref, glr_ref, w2_ref, bg_ref, ng_ref, tri_ref, mean_ref,
                y_ref, st_ref):
    tg = q_ref.shape[0]
    C = GLA_CHUNK
    hi = lax.Precision.HIGHEST

    @pl.when(pl.program_id(1) == 0)
    def _():
        st_ref[...] = jnp.zeros_like(st_ref)

    z = jnp.dot(glr_ref[...], w2_ref[...], preferred_element_type=F32, precision=hi) + bg_ref[...]
    log_alpha = -_softplus(-z) * (1.0 / GLA_GATE_TAU)

    lane = lax.broadcasted_iota(I32, (C, LANES), 1)
    first = lane < GLA_DK
    rr = lax.broadcasted_iota(I32, (C, C), 0)
    cc = lax.broadcasted_iota(I32, (C, C), 1)
    causal = cc <= rr
    r2 = lax.broadcasted_iota(I32, (LANES, LANES), 0)
    c2 = lax.broadcasted_iota(I32, (LANES, LANES), 1)
    same_head = (r2 < GLA_DV) == (c2 < GLA_DK)
    tri = tri_ref[...]

    for c in range(tg // C):
        rows = slice(c * C, (c + 1) * C)
        bcum = jnp.dot(tri, log_alpha[rows], preferred_element_type=F32, precision=hi)
        blast = bcum[C - 1:C]
        kf = k_ref[rows, :]
        q_dec = q_ref[rows, :] * (GLA_DK ** -0.5) * jnp.exp(bcum)
        k_inv = kf * jnp.exp(-bcum)
        k_end = kf * jnp.exp(blast - bcum)
        decay = jnp.exp(blast)
        vf = v_ref[rows, :]
        outs = []
        for p in range(GLA_HEADS // 2):
            cs = slice(p * LANES, (p + 1) * LANES)
            qd, ki, ke, vp = q_dec[:, cs], k_inv[:, cs], k_end[:, cs], vf[:, cs]
            kib = ki.astype(BF16)
            vpb = vp.astype(BF16)
            att0 = _nt_dot(jnp.where(first, qd, 0.0).astype(BF16), kib)
            att1 = _nt_dot(jnp.where(first, 0.0, qd).astype(BF16), kib)
            att0 = jnp.where(causal, att0, 0.0).astype(BF16)
            att1 = jnp.where(causal, att1, 0.0).astype(BF16)
            o_intra = jnp.where(first,
                                jnp.dot(att0, vpb, preferred_element_type=F32),
                                jnp.dot(att1, vpb, preferred_element_type=F32))
            st = st_ref[p]
            o_inter = _nt_dot(qd.astype(BF16), st.astype(BF16))
            u_t = _tn_dot(vpb, ke.astype(BF16))
            st_ref[p] = st * decay[:, cs] + jnp.where(same_head, u_t, 0.0)
            outs.append(o_intra + o_inter)
        o = jnp.concatenate(outs, axis=1)
        ms = jnp.dot(o * o, mean_ref[...], preferred_element_type=F32, precision=hi)
        o = o * lax.rsqrt(ms + 1e-6) * ng_ref[...]
        gf = g_ref[rows, :]
        y_ref[rows, :] = o * (gf * _sigmoid(gf))


def _gla(p_gla, w2, bg, ng, *, tg):
    B, S, _ = p_gla.shape
    W = GLA_WIDTH
    const = lambda b, j: (0, 0)
    tri = jnp.tril(jnp.ones((GLA_CHUNK, GLA_CHUNK), F32))
    head = jnp.arange(W) // GLA_DV
    mean_blk = (head[:, None] == head[None, :]).astype(F32) / GLA_DV

    def colblk(c):
        return pl.BlockSpec((None, tg, W), lambda b, j: (b, j, c))

    return pl.pallas_call(
        _gla_kernel,
        grid=(B, S // tg),
        in_specs=[colblk(0), colblk(1), colblk(2), colblk(3),
                  pl.BlockSpec((None, tg, LANES), lambda b, j: (b, j, 4 * W // LANES)),
                  pl.BlockSpec((LANES, W), const),
                  pl.BlockSpec((1, W), const), pl.BlockSpec((1, W), const),
                  pl.BlockSpec((GLA_CHUNK, GLA_CHUNK), const),
                  pl.BlockSpec((W, W), const)],
        out_specs=pl.BlockSpec((None, tg, W), lambda b, j: (b, j, 0)),
        out_shape=jax.ShapeDtypeStruct((B, S, W), F32),
        scratch_shapes=[pltpu.VMEM((GLA_HEADS // 2, LANES, LANES), F32)],
        compiler_params=pltpu.CompilerParams(
            dimension_semantics=("parallel", "arbitrary"), vmem_limit_bytes=VMEM_LIMIT_BYTES),
        name="gla",
    )(p_gla, p_gla, p_gla, p_gla, p_gla, w2, bg, ng, tri, mean_blk)


def _relbias_kernel(rb_ref, tb_ref):
    r = lax.broadcasted_iota(I32, (LANES, LANES), 0)
    c = lax.broadcasted_iota(I32, (LANES, LANES), 1)
    max_exact = REL_BUCKETS // 2
    for delta in range(REL_SLABS):
        n = jnp.maximum(delta * LANES + r - c, 0)
        nf = jnp.maximum(n, 1).astype(F32)
        large = max_exact + (jnp.log(nf / max_exact) / math.log(REL_MAX_DIST / max_exact)
                             * (REL_BUCKETS - max_exact)).astype(I32)
        large = jnp.minimum(large, REL_BUCKETS - 1)
        bucket = jnp.where(n < max_exact, n, large)
        for h in range(DSA_HEADS):
            acc = jnp.zeros((LANES, LANES), F32)
            for b in range(REL_BUCKETS):
                acc = jnp.where(bucket == b, rb_ref[b, h], acc)
            tb_ref[h, delta] = (acc - rb_ref[REL_BUCKETS - 1, h]) * LOG2E


def _relbias_tiles(rel_bias):
    return pl.pallas_call(
        _relbias_kernel,
        in_specs=[pl.BlockSpec(memory_space=pltpu.SMEM)],
        out_specs=pl.BlockSpec(memory_space=pltpu.VMEM),
        out_shape=jax.ShapeDtypeStruct((DSA_HEADS, REL_SLABS, LANES, LANES), F32),
        name="rel_bias_tiles",
    )(rel_bias)


def _dsa_kernel(iq_ref, iw_ref, q_ref, kit_ref, kt_ref, v_ref, tb_ref, uo_ref, y_ref,
                key_ref, kmx_ref, m_ref, acc_ref, *, top_k):
    i = pl.program_id(1)
    TQ, CK, L = Q_BLOCK, KEY_CHUNK, LANES
    S = key_ref.shape[1]
    first_blk = i * Q_SUB
    n_chunks = (first_blk + Q_SUB - 1) // (CK // L) + 1
    lane = lax.broadcasted_iota(I32, (TQ, L), 1)

    @pl.when(i == 0)
    def _():
        nb = 2 * CK
        for h in range(DSA_HEADS):
            def norm_chunk(c, mx):
                kk = kt_ref[h * DSA_HEAD_DIM:(h + 1) * DSA_HEAD_DIM, pl.ds(pl.multiple_of(c * nb, nb), nb)]
                kk = kk.astype(F32)
                return jnp.maximum(mx, jnp.sum(kk * kk, axis=0, keepdims=True))
            mx = lax.fori_loop(0, S // nb, norm_chunk, jnp.zeros((1, nb), F32))
            kmx_ref[h:h + 1, :] = jnp.broadcast_to(jnp.max(mx, axis=1, keepdims=True), (1, L))

    iq = iq_ref[...]
    iq_h = [iq[:, h * IDX_DIM:(h + 1) * IDX_DIM] for h in range(IDX_HEADS)]
    scale = (IDX_DIM ** -0.5) * (IDX_HEADS ** -0.5)
    w_h = [jnp.broadcast_to(iw_ref[:, h:h + 1] * scale, (TQ, CK)) for h in range(IDX_HEADS)]

    def score_chunk(c, masked):
        off = pl.multiple_of(c * CK, CK)
        kc = kit_ref[:, pl.ds(off, CK)]
        sc = jnp.zeros((TQ, CK), F32)
        for h in range(IDX_HEADS):
            z = jnp.dot(iq_h[h], kc, preferred_element_type=F32)
            sc = sc + jnp.maximum(z, 0.0) * w_h[h]
        sc = sc + 0.0
        if masked:
            row_t = i * TQ + lax.broadcasted_iota(I32, (TQ, CK), 0)
            pos = off + lax.broadcasted_iota(I32, (TQ, CK), 1)
            sc = jnp.where(pos <= row_t, sc, -jnp.inf)
        bits = pltpu.bitcast(sc, I32)
        key_ref[:, pl.ds(off, CK)] = bits ^ ((bits >> 31) & I32(0x7FFFFFFF))

    def score_body(c, carry):
        score_chunk(c, False)
        return carry

    lax.fori_loop(0, n_chunks - 1, score_body, 0)
    score_chunk(n_chunks - 1, True)

    def count_ge(cand, live=None):
        cand_all = jnp.broadcast_to(cand, (TQ, L))
        accs = []
        for r in range(Q_SUB):
            rows = slice(r * L, (r + 1) * L)
            cand_b = cand_all[rows]

            def body(c, acc, rows=rows, cand_b=cand_b):
                kk = key_ref[rows, pl.ds(pl.multiple_of(c * CK, CK), CK)]
                for s in range(CK // L):
                    acc = acc + (kk[:, s * L:(s + 1) * L] >= cand_b).astype(I32)
                return acc
            zero = jnp.zeros((L, L), I32)
            trips = n_chunks if live is None else jnp.where(live[r] > 0, n_chunks, 0)
            accs.append(lax.fori_loop(0, trips, body, zero))
        acc = accs[0] if Q_SUB == 1 else jnp.concatenate(accs, axis=0)
        return jnp.sum(acc, axis=1, keepdims=True)

    def bit_step(thr, cnt_thr, cnt_rej, bit, frozen=None, live=None):
        cand = thr + bit
        cnt = count_ge(cand, live)
        ok = cnt >= top_k
        take = ok if frozen is None else ok & jnp.logical_not(frozen)
        drop = jnp.logical_not(ok) if frozen is None else jnp.logical_not(ok | frozen)
        return jnp.where(take, cand, thr), jnp.where(take, cnt, cnt_thr), jnp.where(drop, cnt, cnt_rej)

    def high_step(b, st):
        return bit_step(*st, lax.shift_left(I32(1), I32(31) - b))

    thr, cnt_thr, cnt_rej = lax.fori_loop(
        0, 32 - HALF_BITS, high_step,
        (jnp.full((TQ, 1), INT_MIN, I32), jnp.full((TQ, 1), S, I32), jnp.zeros((TQ, 1), I32)))

    cnt_next = count_ge(thr + 1)
    exact = cnt_next < top_k

    def live_subs(cnt_thr):
        open_rows = jnp.where(exact | (cnt_thr == top_k), 0, 1)
        return tuple(jnp.max(open_rows[r * L:(r + 1) * L]) for r in range(Q_SUB))

    def low_cond(st):
        return (st[0] < HALF_BITS) & (functools.reduce(lambda x, y: x + y, st[4]) > 0)

    def low_step(st):
        b, thr, cnt_thr, cnt_rej, live = st
        thr, cnt_thr, cnt_rej = bit_step(thr, cnt_thr, cnt_rej, lax.shift_left(I32(1), I32(HALF_BITS - 1) - b),
                                         frozen=exact | (cnt_thr == top_k), live=live)
        return b + 1, thr, cnt_thr, cnt_rej, live_subs(cnt_thr)

    _, thr, cnt_thr, cnt_rej, _ = lax.while_loop(
        low_cond, low_step, (I32(0), thr, cnt_thr, cnt_rej, live_subs(cnt_thr)))

    t_sel = jnp.where(cnt_thr == top_k, thr - 1, thr)
    need = jnp.where(cnt_thr == top_k, 0, top_k - jnp.where(exact, cnt_next, cnt_rej)).astype(F32)
    t_sel_b = jnp.broadcast_to(t_sel, (TQ, L))
    need_b = jnp.broadcast_to(need, (TQ, L))
    uo = uo_ref[...]

    def mask_chunk(c, run):
        off = pl.multiple_of(c * CK, CK)
        kc = key_ref[:, pl.ds(off, CK)]
        out = []
        for s in range(CK // L):
            kk = kc[:, s * L:(s + 1) * L]
            eq = kk == t_sel_b
            pr = jnp.dot(jnp.where(eq, 1.0, 0.0).astype(BF16), uo, preferred_element_type=F32)
            sel = (kk > t_sel_b) | (eq & (run + pr[:, :L] <= need_b))
            sel = sel & (kk > NEG_INF_KEY)
            out.append(jnp.where(sel, 0.0, -jnp.inf).astype(F32))
            run = run + pr[:, L:]
        key_ref[:, pl.ds(off, CK)] = pltpu.bitcast(jnp.concatenate(out, axis=1), I32)
        return run

    lax.fori_loop(0, n_chunks, mask_chunk, jnp.zeros((TQ, L), F32))

    q2 = q_ref[...] * LOG2E
    q_ext = []
    shift_max = jnp.zeros((TQ, L), F32)
    for h in range(DSA_HEADS):
        blk = q2[:, (h // 2) * L:(h // 2 + 1) * L]
        if h % 2:
            blk = pltpu.roll(blk, DSA_HEAD_DIM, 1)
        qf = jnp.where(lane < DSA_HEAD_DIM, blk, 0.0).astype(BF16).astype(F32)
        bound = jnp.sqrt(jnp.sum(qf * qf, axis=1, keepdims=True) * kmx_ref[h:h + 1, :]) * (1.0 + 2.0 ** -7)
        shift_max = jnp.maximum(shift_max, bound)
        q_ext.append(jnp.where(lane == DSA_HEAD_DIM, bound, qf).astype(BF16))
    minus_one_row = jnp.where(lax.broadcasted_iota(I32, (DSA_HEAD_DIM, CK), 0) == 0, -1.0, 0.0).astype(BF16)
    ones_cols = jnp.ones((CK, L), BF16)

    acc_ref[...] = jnp.zeros(acc_ref.shape, F32)

    def attend(c, biased, fast):
        off = pl.multiple_of(c * CK, CK)
        width = CK
        mb = pltpu.bitcast(key_ref[:, pl.ds(off, width)], F32)
        for h in range(DSA_HEADS):
            k_ext = jnp.concatenate(
                [kt_ref[h * DSA_HEAD_DIM:(h + 1) * DSA_HEAD_DIM, pl.ds(off, width)], minus_one_row], axis=0)
            s = jnp.dot(q_ext[h], k_ext, preferred_element_type=F32) + mb
            if biased:
                def tile(r, j):
                    return tb_ref[h, jnp.clip(first_blk + r - (c * (CK // L) + j), 0, REL_SLABS - 1)]
                s = s + jnp.concatenate(
                    [jnp.concatenate([tile(r, j) for r in range(Q_SUB)], axis=0) for j in range(CK // L)], axis=1)
            pair = (h // 2) * L
            v_ext = jnp.concatenate([v_ref[pl.ds(off, width), pair:pair + L], ones_cols], axis=1)
            if fast:
                p = jnp.exp2(s).astype(BF16)
                acc_ref[h] += jnp.dot(p, v_ext, preferred_element_type=F32)
            else:
                m_old = m_ref[h]
                m_new = jnp.maximum(m_old, jnp.max(s, axis=1, keepdims=True))
                alpha = jnp.exp2(m_old - m_new)
                p = jnp.concatenate(
                    [jnp.exp2(s[:, j * L:(j + 1) * L] - m_new) for j in range(width // L)], axis=1)
                pv = jnp.dot(p.astype(BF16), v_ext, preferred_element_type=F32)
                acc_ref[h] = acc_ref[h] * jnp.concatenate([alpha, alpha], axis=1) + pv
                m_ref[h] = m_new

    def attend_all(fast):
        first_biased = jnp.maximum(first_blk - 1, 0) // (CK // L)

        def far_chunk(c, carry):
            attend(c, False, fast)
            return carry

        def near_chunk(c, carry):
            attend(c, True, fast)
            return carry

        lax.fori_loop(0, first_biased, far_chunk, 0)
        lax.fori_loop(first_biased, n_chunks, near_chunk, 0)

    use_fast = 2.0 * jnp.max(shift_max) + jnp.max(jnp.abs(tb_ref[...])) <= FAST_SOFTMAX_MAX_SHIFT

    @pl.when(use_fast)
    def _():
        attend_all(True)

    @pl.when(jnp.logical_not(use_fast))
    def _():
        m_ref[...] = jnp.full(m_ref.shape, NEG_BIG, F32)
        attend_all(False)

    for p in range(DSA_HEADS // 2):
        a0, a1 = acc_ref[2 * p], acc_ref[2 * p + 1]
        y_ref[:, p * L:(p + 1) * L] = jnp.where(lane < DSA_HEAD_DIM, a0[:, :L] / a0[:, L:], a1[:, :L] / a1[:, L:])


def _dsa(iq, iw, q, kit, kt, v, tb, *, top_k):
    B, S, W = q.shape
    upper = jnp.triu(jnp.ones((LANES, LANES), F32))
    uo = jnp.concatenate([upper, jnp.ones((LANES, LANES), F32)], axis=1).astype(BF16)
    qrow = lambda b, i: (b, i, 0)
    whole = lambda b, i: (b, 0, 0)
    one = pl.Buffered(1)
    return pl.pallas_call(
        functools.partial(_dsa_kernel, top_k=top_k),
        grid=(B, S // Q_BLOCK),
        in_specs=[pl.BlockSpec((None, Q_BLOCK, LANES), qrow),
                  pl.BlockSpec((None, Q_BLOCK, LANES), qrow),
                  pl.BlockSpec((None, Q_BLOCK, W), qrow),
                  pl.BlockSpec((None, IDX_DIM, S), whole, pipeline_mode=one),
                  pl.BlockSpec((None, W, S), whole, pipeline_mode=one),
                  pl.BlockSpec((None, S, W), whole, pipeline_mode=one),
                  pl.BlockSpec((DSA_HEADS, REL_SLABS, LANES, LANES), lambda b, i: (0, 0, 0, 0)),
                  pl.BlockSpec((LANES, 2 * LANES), lambda b, i: (0, 0))],
        out_specs=pl.BlockSpec((None, Q_BLOCK, W), qrow),
        out_shape=jax.ShapeDtypeStruct((B, S, W), F32),
        scratch_shapes=[pltpu.VMEM((Q_BLOCK, S), I32),
                        pltpu.VMEM((SUBLANES, LANES), F32),
                        pltpu.VMEM((DSA_HEADS, Q_BLOCK, LANES), F32),
                        pltpu.VMEM((DSA_HEADS, Q_BLOCK, 2 * LANES), F32)],
        compiler_params=pltpu.CompilerParams(
            dimension_semantics=("parallel", "arbitrary"), vmem_limit_bytes=VMEM_LIMIT_BYTES),
        name="dsa",
    )(iq, iw, q, kit, kt, v, tb, uo)


def _layer_norm(z, g, b):
    mu = jnp.mean(z, axis=-1, keepdims=True)
    zc = z - mu
    var = jnp.mean(zc * zc, axis=-1, keepdims=True)
    return zc * lax.rsqrt(var + LN_EPS) * g + b


def _mix_kernel(x_ref, ya_ref, yb_ref, yc_ref, wa_ref, wb_ref, wc_ref, g_ref, b_ref, o_ref):
    mix = jnp.dot(ya_ref[...].astype(BF16), wa_ref[...], preferred_element_type=F32)
    mix = mix + jnp.dot(yb_ref[...].astype(BF16), wb_ref[...], preferred_element_type=F32)
    mix = mix + jnp.dot(yc_ref[...].astype(BF16), wc_ref[...], preferred_element_type=F32)
    o_ref[...] = _layer_norm(DN_ALPHA * x_ref[...] + mix, g_ref[...], b_ref[...])


def _mix(x, ya, yb, yc, wa, wb, wc, g, b, *, tm):
    T, D = x.shape
    row = lambda m: (m, 0)
    const = lambda m: (0, 0)
    return pl.pallas_call(
        _mix_kernel,
        grid=(T // tm,),
        in_specs=[pl.BlockSpec((tm, D), row),
                  pl.BlockSpec((tm, ya.shape[1]), row), pl.BlockSpec((tm, yb.shape[1]), row),
                  pl.BlockSpec((tm, yc.shape[1]), row),
                  pl.BlockSpec(wa.shape, const), pl.BlockSpec(wb.shape, const), pl.BlockSpec(wc.shape, const),
                  pl.BlockSpec((1, D), const), pl.BlockSpec((1, D), const)],
        out_specs=pl.BlockSpec((tm, D), row),
        out_shape=jax.ShapeDtypeStruct((T, D), F32),
        compiler_params=pltpu.CompilerParams(
            dimension_semantics=("parallel",), vmem_limit_bytes=VMEM_LIMIT_BYTES),
        name="out_proj_ln",
    )(x, ya, yb, yc, wa, wb, wc, g, b)


def _ffn_kernel(x_ref, wg_ref, wu_ref, wd_ref, g_ref, b_ref, o_ref, xb_ref, acc_ref):
    f = pl.program_id(1)

    @pl.when(f == 0)
    def _():
        xb_ref[...] = x_ref[...].astype(BF16)
        acc_ref[...] = jnp.zeros_like(acc_ref)

    xb = xb_ref[...]
    gate = jnp.dot(xb, wg_ref[...], preferred_element_type=F32)
    up = jnp.dot(xb, wu_ref[...], preferred_element_type=F32)
    h = (gate * _sigmoid(gate)) * up
    acc_ref[...] += jnp.dot(h.astype(BF16), wd_ref[...], preferred_element_type=F32)

    @pl.when(f == pl.num_programs(1) - 1)
    def _():
        o_ref[...] = _layer_norm(DN_ALPHA * x_ref[...] + acc_ref[...], g_ref[...], b_ref[...])


def _ffn(x, wg, wu, wd, g, b, *, tm, tf):
    T, D = x.shape
    FF = wg.shape[1]
    row = lambda m, f: (m, 0)
    const = lambda m, f: (0, 0)
    return pl.pallas_call(
        _ffn_kernel,
        grid=(T // tm, FF // tf),
        in_specs=[pl.BlockSpec((tm, D), row),
                  pl.BlockSpec((D, tf), lambda m, f: (0, f)),
                  pl.BlockSpec((D, tf), lambda m, f: (0, f)),
                  pl.BlockSpec((tf, D), lambda m, f: (f, 0)),
                  pl.BlockSpec((1, D), const), pl.BlockSpec((1, D), const)],
        out_specs=pl.BlockSpec((tm, D), row),
        out_shape=jax.ShapeDtypeStruct((T, D), F32),
        scratch_shapes=[pltpu.VMEM((tm, D), BF16), pltpu.VMEM((tm, D), F32)],
        compiler_params=pltpu.CompilerParams(
            dimension_semantics=("parallel", "arbitrary"), vmem_limit_bytes=VMEM_LIMIT_BYTES),
        name="ffn_ln",
    )(x, wg, wu, wd, g, b)


def _split_w_in(w):
    parts = [w[:, IN_OFFSETS[j]:IN_OFFSETS[j + 1]] for j in range(len(IN_SIZES))]
    lru_x, lru_g, gq, gk, gv, gg, g_lr, dq, dk, dv, iq, ik, iw = parts
    D = w.shape[0]
    pad = lambda a, n: jnp.pad(a, ((0, 0), (0, n - a.shape[1])))
    wa = jnp.concatenate([lru_x, lru_g, gq, gk, gv, gg, pad(g_lr, LANES),
                          dq * (DSA_HEAD_DIM ** -0.5), dv, iq, pad(iw, LANES)], axis=1)
    wt = jnp.concatenate([dk, ik], axis=1).T
    assert wa.shape == (D, PA_END) and wt.shape == (PT_ROWS, D)
    return wa.astype(BF16), wt.astype(BF16)


def _block_diag(w):
    n, k, _ = w.shape
    eye = jnp.eye(n, dtype=w.dtype)
    return (eye[:, None, :, None] * w[:, :, None, :]).reshape(n * k, n * k)


def kernel(x, w_in, conv_w, conv_b, lru_wa, lru_ba, lru_wx, lru_bx, lru_lambda, gla_w_gate2, gla_b_gate,
           gla_norm_g, rel_bias, w_out, ln1_g, ln1_b, w_ffn_gate, w_ffn_up, w_ffn_down, ln2_g, ln2_b):
    B, S, D = x.shape
    T = B * S
    top_k = min(IDX_TOPK_MAX, S // 4)
    assert S % (2 * KEY_CHUNK) == 0 and D == D_MODEL
    tm = min(512, S)
    tb = _relbias_tiles(rel_bias)
    row = lambda a: a.reshape(1, -1)
    for l in range(w_in.shape[0]):
        wa, wt = _split_w_in(w_in[l])
        p_lru, p_gla, dq, dv, iq, iw, kt, kit = _proj(x, wa, wt, tm=tm)
        y_lru = _lru(p_lru, conv_w[l], row(conv_b[l]),
                     _block_diag(lru_wa[l]).astype(BF16), row(lru_ba[l]),
                     _block_diag(lru_wx[l]).astype(BF16), row(lru_bx[l]), row(lru_lambda[l]), tl=tm)
        w2 = jnp.pad(gla_w_gate2[l], ((0, LANES - GLA_GATE_RANK), (0, 0)))
        y_gla = _gla(p_gla, w2, row(gla_b_gate[l]), row(gla_norm_g[l]), tg=min(256, S))
        y_dsa = _dsa(iq, iw, dq, kit, kt, dv, tb, top_k=top_k)
        wo = w_out[l].astype(BF16)
        x1 = _mix(x.reshape(T, D), y_lru.reshape(T, -1), y_gla.reshape(T, -1), y_dsa.reshape(T, -1),
                  wo[:LRU_WIDTH], wo[LRU_WIDTH:LRU_WIDTH + GLA_WIDTH], wo[LRU_WIDTH + GLA_WIDTH:],
                  row(ln1_g[l]), row(ln1_b[l]), tm=tm)
        x2 = _ffn(x1, w_ffn_gate[l].astype(BF16), w_ffn_up[l].astype(BF16), w_ffn_down[l].astype(BF16),
                  row(ln2_g[l]), row(ln2_b[l]), tm=tm, tf=D_FF // 2)
        x = x2.reshape(B, S, D)
    return x
```

```python
import functools
import math

import jax
import jax.numpy as jnp
import numpy as np
from jax import lax
from jax.experimental import pallas as pl
from jax.experimental.pallas import tpu as pltpu

F32 = jnp.float32
BF16 = jnp.bfloat16
I32 = jnp.int32

D_MODEL = 1024
DEPTH = 2
LRU_WIDTH = 256
LRU_BLOCKS = 4
LRU_BLOCK = LRU_WIDTH // LRU_BLOCKS
CONV_WIDTH = 4
LRU_C = 8.0
GLA_HEADS = 6
GLA_DK = 64
GLA_DV = 64
GLA_GATE_RANK = 16
GLA_GATE_TAU = 16.0
GLA_CHUNK = 64
GLA_WIDTH = GLA_HEADS * GLA_DV
DSA_HEADS = 6
DSA_HEAD_DIM = 64
DSA_WIDTH = DSA_HEADS * DSA_HEAD_DIM
IDX_HEADS = 4
IDX_DIM = 32
IDX_TOPK_MAX = 256
REL_BUCKETS = 32
REL_MAX_DIST = 128
D_FF = 2816
DN_ALPHA = (2.0 * DEPTH) ** 0.25
LN_EPS = 1e-5

IN_SIZES = (LRU_WIDTH, LRU_WIDTH, GLA_WIDTH, GLA_WIDTH, GLA_WIDTH, GLA_WIDTH, GLA_GATE_RANK,
            DSA_WIDTH, DSA_WIDTH, DSA_WIDTH, IDX_HEADS * IDX_DIM, IDX_DIM, IDX_HEADS)
IN_OFFSETS = [0] + [int(o) for o in np.cumsum(IN_SIZES)]

LANES = 128
SUBLANES = 8
VMEM_LIMIT_BYTES = 56 * 1024 * 1024

LRU_COLS = 2 * LRU_WIDTH
GLA_COLS = 4 * GLA_WIDTH + LANES
PA_LRU = 0
PA_GLA = PA_LRU + LRU_COLS
PA_DQ = PA_GLA + GLA_COLS
PA_DV = PA_DQ + DSA_WIDTH
PA_IQ = PA_DV + DSA_WIDTH
PA_IW = PA_IQ + LANES
PA_END = PA_IW + LANES
PT_ROWS = DSA_WIDTH + IDX_DIM

Q_BLOCK = 256
Q_SUB = Q_BLOCK // LANES
KEY_CHUNK = 512
HALF_BITS = 16
INT_MIN = -2147483648
NEG_INF_KEY = -2139095041
NEG_BIG = -1e30
LOG2E = 1.4426950408889634
REL_SLABS = 3
FAST_SOFTMAX_MAX_SHIFT = 120.0


def _nt_dot(a, b):
    return lax.dot_general(a, b, (((1,), (1,)), ((), ())), preferred_element_type=F32)


def _tn_dot(a, b):
    return lax.dot_general(a, b, (((0,), (0,)), ((), ())), preferred_element_type=F32)


def _split3(x):
    x1 = x.astype(BF16)
    r1 = x - x1.astype(F32)
    x2 = r1.astype(BF16)
    return x1, x2, (r1 - x2.astype(F32)).astype(BF16)


def _softplus(z):
    return jnp.maximum(z, 0.0) + jnp.log(1.0 + jnp.exp(-jnp.abs(z)))


def _sigmoid(z):
    return 1.0 / (1.0 + jnp.exp(-z))


def _proj_kernel(x_ref, wa_ref, wt_ref, lru_ref, gla_ref, q_ref, v_ref, iq_ref, iw_ref, kt_ref, kit_ref):
    xb = x_ref[...].astype(BF16)

    def mm(c0, c1):
        return jnp.dot(xb, wa_ref[:, c0:c1], preferred_element_type=F32)

    lru_ref[...] = mm(PA_LRU, PA_GLA)
    gla_ref[...] = mm(PA_GLA, PA_DQ)
    q_ref[...] = mm(PA_DQ, PA_DV)
    v_ref[...] = mm(PA_DV, PA_IQ).astype(BF16)
    iq_ref[...] = mm(PA_IQ, PA_IW).astype(BF16)
    iw_ref[...] = mm(PA_IW, PA_END)
    t = _nt_dot(wt_ref[...], xb)
    kt_ref[...] = t[:DSA_WIDTH].astype(BF16)
    kit_ref[...] = t[DSA_WIDTH:PT_ROWS].astype(BF16)


def _proj(x, wa, wt, *, tm):
    B, S, D = x.shape
    grid = (B, S // tm)
    row = lambda b, m: (b, m, 0)
    col = lambda b, m: (b, 0, m)
    const = lambda b, m: (0, 0)
    out_shape = (
        jax.ShapeDtypeStruct((B, S, LRU_COLS), F32),
        jax.ShapeDtypeStruct((B, S, GLA_COLS), F32),
        jax.ShapeDtypeStruct((B, S, DSA_WIDTH), F32),
        jax.ShapeDtypeStruct((B, S, DSA_WIDTH), BF16),
        jax.ShapeDtypeStruct((B, S, LANES), BF16),
        jax.ShapeDtypeStruct((B, S, LANES), F32),
        jax.ShapeDtypeStruct((B, DSA_WIDTH, S), BF16),
        jax.ShapeDtypeStruct((B, IDX_DIM, S), BF16),
    )
    out_specs = (
        pl.BlockSpec((None, tm, LRU_COLS), row),
        pl.BlockSpec((None, tm, GLA_COLS), row),
        pl.BlockSpec((None, tm, DSA_WIDTH), row),
        pl.BlockSpec((None, tm, DSA_WIDTH), row),
        pl.BlockSpec((None, tm, LANES), row),
        pl.BlockSpec((None, tm, LANES), row),
        pl.BlockSpec((None, DSA_WIDTH, tm), col),
        pl.BlockSpec((None, IDX_DIM, tm), col),
    )
    return pl.pallas_call(
        _proj_kernel,
        grid=grid,
        in_specs=[pl.BlockSpec((None, tm, D), row),
                  pl.BlockSpec((D, PA_END), const),
                  pl.BlockSpec((PT_ROWS, D), const)],
        out_specs=out_specs,
        out_shape=out_shape,
        compiler_params=pltpu.CompilerParams(
            dimension_semantics=("parallel", "parallel"), vmem_limit_bytes=VMEM_LIMIT_BYTES),
        name="in_proj",
    )(x, wa, wt)


def _shift_rows(x, d, fill, row):
    return jnp.where(row >= d, pltpu.roll(x, d, 0), fill)


def _lru_kernel(p_ref, cw_ref, cb_ref, wa_ref, ba_ref, wx_ref, bx_ref, lam_ref, y_ref, xprev_ref, h_ref):
    tl = p_ref.shape[0]

    @pl.when(pl.program_id(1) == 0)
    def _():
        xprev_ref[...] = jnp.zeros_like(xprev_ref)
        h_ref[...] = jnp.zeros_like(h_ref)

    xb = p_ref[:, 0:LRU_WIDTH]
    gb = p_ref[:, LRU_WIDTH:2 * LRU_WIDTH]
    prev = xprev_ref[...]
    row8 = lax.broadcasted_iota(I32, (SUBLANES, LRU_WIDTH), 0)
    cw = cw_ref[...]
    xc = cb_ref[...] + cw[CONV_WIDTH - 1:CONV_WIDTH] * xb
    for d in range(1, CONV_WIDTH):
        r = pltpu.roll(xb, d, 0)
        top = jnp.where(row8 < d, pltpu.roll(prev, d, 0), r[0:SUBLANES])
        r = jnp.concatenate([top, r[SUBLANES:]], axis=0)
        xc = xc + cw[CONV_WIDTH - 1 - d:CONV_WIDTH - d] * r
    xprev_ref[...] = xb[tl - SUBLANES:tl]

    xcb = xc.astype(BF16)
    r_gate = _sigmoid(jnp.dot(xcb, wa_ref[...], preferred_element_type=F32) + ba_ref[...])
    i_gate = _sigmoid(jnp.dot(xcb, wx_ref[...], preferred_element_type=F32) + bx_ref[...])
    log_a = (-LRU_C) * r_gate * _softplus(-lam_ref[...])
    a = jnp.exp(log_a)
    u = jnp.sqrt(1.0 - a * a) * (i_gate * xc)

    row = lax.broadcasted_iota(I32, (tl, LRU_WIDTH), 0)
    d = 1
    while d < tl:
        a_s = _shift_rows(a, d, 1.0, row)
        u_s = _shift_rows(u, d, 0.0, row)
        u = u + a * u_s
        a = a * a_s
        d *= 2
    h = u + a * h_ref[SUBLANES - 1:SUBLANES, :]
    h_ref[...] = h[tl - SUBLANES:tl]
    y_ref[...] = h * jax.nn.gelu(gb)


def _lru(p_lru, cw, cb, wa, ba, wx, bx, lam, *, tl):
    B, S, _ = p_lru.shape
    const = lambda b, j: (0, 0)
    vec = pl.BlockSpec((1, LRU_WIDTH), const)
    mat = pl.BlockSpec((LRU_WIDTH, LRU_WIDTH), const)
    return pl.pallas_call(
        _lru_kernel,
        grid=(B, S // tl),
        in_specs=[pl.BlockSpec((None, tl, LRU_COLS), lambda b, j: (b, j, 0)),
                  pl.BlockSpec((CONV_WIDTH, LRU_WIDTH), const), vec, mat, vec, mat, vec, vec],
        out_specs=pl.BlockSpec((None, tl, LRU_WIDTH), lambda b, j: (b, j, 0)),
        out_shape=jax.ShapeDtypeStruct((B, S, LRU_WIDTH), F32),
        scratch_shapes=[pltpu.VMEM((SUBLANES, LRU_WIDTH), F32), pltpu.VMEM((SUBLANES, LRU_WIDTH), F32)],
        compiler_params=pltpu.CompilerParams(
            dimension_semantics=("parallel", "arbitrary"), vmem_limit_bytes=VMEM_LIMIT_BYTES),
        name="rg_lru",
    )(p_lru, cw, cb, wa, ba, wx, bx, lam)


def _gla_kernel(q_ref, k_ref, v_ref, g_ref, glr_ref, w2_ref, bg_ref, ng_ref, tri_ref, mean_ref,
                y_ref, st_ref):
    tg = q_ref.shape[0]
    C = GLA_CHUNK
    hi = lax.Precision.HIGHEST

    @pl.when(pl.program_id(1) == 0)
    def _():
        st_ref[...] = jnp.zeros_like(st_ref)

    z = jnp.dot(glr_ref[...], w2_ref[...], preferred_element_type=F32, precision=hi) + bg_ref[...]
    log_alpha = -_softplus(-z) * (1.0 / GLA_GATE_TAU)

    lane = lax.broadcasted_iota(I32, (C, LANES), 1)
    first = lane < GLA_DK
    rr = lax.broadcasted_iota(I32, (C, C), 0)
    cc = lax.broadcasted_iota(I32, (C, C), 1)
    causal = cc <= rr
    causal2 = jnp.concatenate([causal, causal], axis=0)
    r2 = lax.broadcasted_iota(I32, (LANES, LANES), 0)
    c2 = lax.broadcasted_iota(I32, (LANES, LANES), 1)
    same_head = (r2 < GLA_DV) == (c2 < GLA_DK)
    tri = tri_ref[...]

    for c in range(tg // C):
        rows = slice(c * C, (c + 1) * C)
        bcum = sum(jnp.dot(tri, piece, preferred_element_type=F32)
                   for piece in _split3(log_alpha[rows]))
        blast = bcum[C - 1:C]
        kf = k_ref[rows, :]
        q_dec = q_ref[rows, :] * (GLA_DK ** -0.5) * jnp.exp(bcum)
        k_inv = kf * jnp.exp(-bcum)
        k_end = kf * jnp.exp(blast - bcum)
        decay = jnp.exp(blast)
        vf = v_ref[rows, :]
        outs = []
        for p in range(GLA_HEADS // 2):
            cs = slice(p * LANES, (p + 1) * LANES)
            qd, ki, ke, vp = q_dec[:, cs], k_inv[:, cs], k_end[:, cs], vf[:, cs]
            kib = ki.astype(BF16)
            vpb = vp.astype(BF16)
            q_two = jnp.concatenate([jnp.where(first, qd, 0.0), jnp.where(first, 0.0, qd)], axis=0).astype(BF16)
            att = jnp.where(causal2, _nt_dot(q_two, kib), 0.0).astype(BF16)
            o_two = jnp.dot(att, vpb, preferred_element_type=F32)
            o_intra = jnp.where(first, o_two[:C], o_two[C:])
            st = st_ref[p]
            o_inter = _nt_dot(qd.astype(BF16), st.astype(BF16))
            u_t = _tn_dot(vpb, ke.astype(BF16))
            st_ref[p] = st * decay[:, cs] + jnp.where(same_head, u_t, 0.0)
            outs.append(o_intra + o_inter)
        o = jnp.concatenate(outs, axis=1)
        ms = sum(jnp.dot(piece, mean_ref[...], preferred_element_type=F32)
                 for piece in _split3(o * o))
        o = o * lax.rsqrt(ms + 1e-6) * ng_ref[...]
        gf = g_ref[rows, :]
        y_ref[rows, :] = o * (gf * _sigmoid(gf))


def _gla(p_gla, w2, bg, ng, *, tg):
    B, S, _ = p_gla.shape
    W = GLA_WIDTH
    const = lambda b, j: (0, 0)
    tri = jnp.tril(jnp.ones((GLA_CHUNK, GLA_CHUNK), F32)).astype(BF16)
    head = jnp.arange(W) // GLA_DV
    mean_blk = ((head[:, None] == head[None, :]).astype(F32) / GLA_DV).astype(BF16)

    def colblk(c):
        return pl.BlockSpec((None, tg, W), lambda b, j: (b, j, c))

    return pl.pallas_call(
        _gla_kernel,
        grid=(B, S // tg),
        in_specs=[colblk(0), colblk(1), colblk(2), colblk(3),
                  pl.BlockSpec((None, tg, LANES), lambda b, j: (b, j, 4 * W // LANES)),
                  pl.BlockSpec((LANES, W), const),
                  pl.BlockSpec((1, W), const), pl.BlockSpec((1, W), const),
                  pl.BlockSpec((GLA_CHUNK, GLA_CHUNK), const),
                  pl.BlockSpec((W, W), const)],
        out_specs=pl.BlockSpec((None, tg, W), lambda b, j: (b, j, 0)),
        out_shape=jax.ShapeDtypeStruct((B, S, W), F32),
        scratch_shapes=[pltpu.VMEM((GLA_HEADS // 2, LANES, LANES), F32)],
        compiler_params=pltpu.CompilerParams(
            dimension_semantics=("parallel", "arbitrary"), vmem_limit_bytes=VMEM_LIMIT_BYTES),
        name="gla",
    )(p_gla, p_gla, p_gla, p_gla, p_gla, w2, bg, ng, tri, mean_blk)


def _relbias_kernel(rb_ref, tb_ref):
    r = lax.broadcasted_iota(I32, (LANES, LANES), 0)
    c = lax.broadcasted_iota(I32, (LANES, LANES), 1)
    max_exact = REL_BUCKETS // 2
    for delta in range(REL_SLABS):
        n = jnp.maximum(delta * LANES + r - c, 0)
        nf = jnp.maximum(n, 1).astype(F32)
        large = max_exact + (jnp.log(nf / max_exact) / math.log(REL_MAX_DIST / max_exact)
                             * (REL_BUCKETS - max_exact)).astype(I32)
        large = jnp.minimum(large, REL_BUCKETS - 1)
        bucket = jnp.where(n < max_exact, n, large)
        for h in range(DSA_HEADS):
            acc = jnp.zeros((LANES, LANES), F32)
            for b in range(REL_BUCKETS):
                acc = jnp.where(bucket == b, rb_ref[b, h], acc)
            tb_ref[h, delta] = (acc - rb_ref[REL_BUCKETS - 1, h]) * LOG2E


def _relbias_tiles(rel_bias):
    return pl.pallas_call(
        _relbias_kernel,
        in_specs=[pl.BlockSpec(memory_space=pltpu.SMEM)],
        out_specs=pl.BlockSpec(memory_space=pltpu.VMEM),
        out_shape=jax.ShapeDtypeStruct((DSA_HEADS, REL_SLABS, LANES, LANES), F32),
        name="rel_bias_tiles",
    )(rel_bias)


def _dsa_kernel(iq_ref, iw_ref, q_ref, kit_ref, kt_ref, v_ref, tb_ref, uo_ref, y_ref,
                key_ref, kmx_ref, m_ref, acc_ref, *, top_k):
    i = pl.program_id(1)
    TQ, CK, L = Q_BLOCK, KEY_CHUNK, LANES
    S = key_ref.shape[1]
    first_blk = i * Q_SUB
    n_chunks = (first_blk + Q_SUB - 1) // (CK // L) + 1
    lane = lax.broadcasted_iota(I32, (TQ, L), 1)

    @pl.when(i == 0)
    def _():
        nb = 2 * CK
        for h in range(DSA_HEADS):
            def norm_chunk(c, mx):
                kk = kt_ref[h * DSA_HEAD_DIM:(h + 1) * DSA_HEAD_DIM, pl.ds(pl.multiple_of(c * nb, nb), nb)]
                kk = kk.astype(F32)
                return jnp.maximum(mx, jnp.sum(kk * kk, axis=0, keepdims=True))
            mx = lax.fori_loop(0, S // nb, norm_chunk, jnp.zeros((1, nb), F32))
            kmx_ref[h:h + 1, :] = jnp.broadcast_to(jnp.max(mx, axis=1, keepdims=True), (1, L))

    iq = iq_ref[...]
    iq_h = [iq[:, h * IDX_DIM:(h + 1) * IDX_DIM] for h in range(IDX_HEADS)]
    scale = (IDX_DIM ** -0.5) * (IDX_HEADS ** -0.5)
    w_h = [jnp.broadcast_to(iw_ref[:, h:h + 1] * scale, (TQ, CK)) for h in range(IDX_HEADS)]

    def score_chunk(c, masked):
        off = pl.multiple_of(c * CK, CK)
        kc = kit_ref[:, pl.ds(off, CK)]
        sc = jnp.zeros((TQ, CK), F32)
        for h in range(IDX_HEADS):
            z = jnp.dot(iq_h[h], kc, preferred_element_type=F32)
            sc = sc + jnp.maximum(z, 0.0) * w_h[h]
        sc = sc + 0.0
        if masked:
            row_t = i * TQ + lax.broadcasted_iota(I32, (TQ, CK), 0)
            pos = off + lax.broadcasted_iota(I32, (TQ, CK), 1)
            sc = jnp.where(pos <= row_t, sc, -jnp.inf)
        bits = pltpu.bitcast(sc, I32)
        key_ref[:, pl.ds(off, CK)] = bits ^ ((bits >> 31) & I32(0x7FFFFFFF))

    def score_body(c, carry):
        score_chunk(c, False)
        return carry

    lax.fori_loop(0, n_chunks - 1, score_body, 0)
    score_chunk(n_chunks - 1, True)

    def count_ge(cand):
        cand_all = jnp.broadcast_to(cand, (TQ, L))
        accs = []
        for r in range(Q_SUB):
            rows = slice(r * L, (r + 1) * L)
            cand_b = cand_all[rows]

            def body(c, acc, rows=rows, cand_b=cand_b):
                kk = key_ref[rows, pl.ds(pl.multiple_of(c * CK, CK), CK)]
                for s in range(CK // L):
                    acc = acc + (kk[:, s * L:(s + 1) * L] >= cand_b).astype(I32)
                return acc
            accs.append(lax.fori_loop(0, n_chunks, body, jnp.zeros((L, L), I32)))
        acc = accs[0] if Q_SUB == 1 else jnp.concatenate(accs, axis=0)
        return jnp.sum(acc, axis=1, keepdims=True)

    def bit_step(thr, cnt_thr, bit):
        cand = thr + bit
        cnt = count_ge(cand)
        ok = cnt >= top_k
        return jnp.where(ok, cand, thr), jnp.where(ok, cnt, cnt_thr)

    def high_step(b, st):
        return bit_step(st[0], st[1], lax.shift_left(I32(1), I32(31) - b))

    thr, cnt_thr = lax.fori_loop(
        0, 32 - HALF_BITS, high_step,
        (jnp.full((TQ, 1), INT_MIN, I32), jnp.full((TQ, 1), S, I32)))

    cnt_next = count_ge(thr + 1)
    exact = cnt_next < top_k

    def low_cond(st):
        b, _, _, active = st
        return (b < HALF_BITS) & (active > 0)

    def low_step(st):
        b, thr, cnt_thr, _ = st
        new_thr, new_cnt = bit_step(thr, cnt_thr, lax.shift_left(I32(1), I32(HALF_BITS - 1) - b))
        thr = jnp.where(exact, thr, new_thr)
        cnt_thr = jnp.where(exact, cnt_thr, new_cnt)
        active = jnp.max(jnp.where(exact | (cnt_thr == top_k), 0, 1))
        return b + 1, thr, cnt_thr, active

    active0 = jnp.max(jnp.where(exact | (cnt_thr == top_k), 0, 1))
    _, thr, cnt_thr, _ = lax.while_loop(low_cond, low_step, (I32(0), thr, cnt_thr, active0))

    t_sel = jnp.where(cnt_thr == top_k, thr - 1, thr)
    need = (top_k - count_ge(t_sel + 1)).astype(F32)
    t_sel_b = jnp.broadcast_to(t_sel, (TQ, L))
    need_b = jnp.broadcast_to(need, (TQ, L))
    uo = uo_ref[...]

    def mask_chunk(c, run):
        off = pl.multiple_of(c * CK, CK)
        kc = key_ref[:, pl.ds(off, CK)]
        out = []
        for s in range(CK // L):
            kk = kc[:, s * L:(s + 1) * L]
            eq = kk == t_sel_b
            pr = jnp.dot(jnp.where(eq, 1.0, 0.0).astype(BF16), uo, preferred_element_type=F32)
            sel = (kk > t_sel_b) | (eq & (run + pr[:, :L] <= need_b))
            sel = sel & (kk > NEG_INF_KEY)
            out.append(jnp.where(sel, 0.0, -jnp.inf).astype(F32))
            run = run + pr[:, L:]
        key_ref[:, pl.ds(off, CK)] = pltpu.bitcast(jnp.concatenate(out, axis=1), I32)
        return run

    lax.fori_loop(0, n_chunks, mask_chunk, jnp.zeros((TQ, L), F32))

    q2 = q_ref[...] * LOG2E
    q_ext = []
    shift_max = jnp.zeros((TQ, L), F32)
    for h in range(DSA_HEADS):
        blk = q2[:, (h // 2) * L:(h // 2 + 1) * L]
        if h % 2:
            blk = pltpu.roll(blk, DSA_HEAD_DIM, 1)
        qf = jnp.where(lane < DSA_HEAD_DIM, blk, 0.0).astype(BF16).astype(F32)
        bound = jnp.sqrt(jnp.sum(qf * qf, axis=1, keepdims=True) * kmx_ref[h:h + 1, :]) * (1.0 + 2.0 ** -7)
        shift_max = jnp.maximum(shift_max, bound)
        q_ext.append(jnp.where(lane == DSA_HEAD_DIM, bound, qf).astype(BF16))
    minus_one_row = jnp.where(lax.broadcasted_iota(I32, (DSA_HEAD_DIM, CK), 0) == 0, -1.0, 0.0).astype(BF16)
    ones_cols = jnp.ones((CK, L), BF16)

    acc_ref[...] = jnp.zeros(acc_ref.shape, F32)

    def attend(c, biased, fast):
        off = pl.multiple_of(c * CK, CK)
        width = CK
        mb = pltpu.bitcast(key_ref[:, pl.ds(off, width)], F32)
        for h in range(DSA_HEADS):
            k_ext = jnp.concatenate(
                [kt_ref[h * DSA_HEAD_DIM:(h + 1) * DSA_HEAD_DIM, pl.ds(off, width)], minus_one_row], axis=0)
            s = jnp.dot(q_ext[h], k_ext, preferred_element_type=F32) + mb
            if biased:
                def tile(r, j):
                    return tb_ref[h, jnp.clip(first_blk + r - (c * (CK // L) + j), 0, REL_SLABS - 1)]
                s = s + jnp.concatenate(
                    [jnp.concatenate([tile(r, j) for r in range(Q_SUB)], axis=0) for j in range(CK // L)], axis=1)
            pair = (h // 2) * L
            v_ext = jnp.concatenate([v_ref[pl.ds(off, width), pair:pair + L], ones_cols], axis=1)
            if fast:
                p = jnp.exp2(s).astype(BF16)
                acc_ref[h] += jnp.dot(p, v_ext, preferred_element_type=F32)
            else:
                m_old = m_ref[h]
                m_new = jnp.maximum(m_old, jnp.max(s, axis=1, keepdims=True))
                alpha = jnp.exp2(m_old - m_new)
                p = jnp.concatenate(
                    [jnp.exp2(s[:, j * L:(j + 1) * L] - m_new) for j in range(width // L)], axis=1)
                pv = jnp.dot(p.astype(BF16), v_ext, preferred_element_type=F32)
                acc_ref[h] = acc_ref[h] * jnp.concatenate([alpha, alpha], axis=1) + pv
                m_ref[h] = m_new

    def attend_all(fast):
        first_biased = jnp.maximum(first_blk - 1, 0) // (CK // L)

        def far_chunk(c, carry):
            attend(c, False, fast)
            return carry

        def near_chunk(c, carry):
            attend(c, True, fast)
            return carry

        lax.fori_loop(0, first_biased, far_chunk, 0)
        lax.fori_loop(first_biased, n_chunks, near_chunk, 0)

    use_fast = 2.0 * jnp.max(shift_max) + jnp.max(jnp.abs(tb_ref[...])) <= FAST_SOFTMAX_MAX_SHIFT

    @pl.when(use_fast)
    def _():
        attend_all(True)

    @pl.when(jnp.logical_not(use_fast))
    def _():
        m_ref[...] = jnp.full(m_ref.shape, NEG_BIG, F32)
        attend_all(False)

    for p in range(DSA_HEADS // 2):
        a0, a1 = acc_ref[2 * p], acc_ref[2 * p + 1]
        y_ref[:, p * L:(p + 1) * L] = jnp.where(lane < DSA_HEAD_DIM, a0[:, :L] / a0[:, L:], a1[:, :L] / a1[:, L:])


def _dsa(iq, iw, q, kit, kt, v, tb, *, top_k):
    B, S, W = q.shape
    upper = jnp.triu(jnp.ones((LANES, LANES), F32))
    uo = jnp.concatenate([upper, jnp.ones((LANES, LANES), F32)], axis=1).astype(BF16)
    qrow = lambda b, i: (b, i, 0)
    whole = lambda b, i: (b, 0, 0)
    one = pl.Buffered(1)
    return pl.pallas_call(
        functools.partial(_dsa_kernel, top_k=top_k),
        grid=(B, S // Q_BLOCK),
        in_specs=[pl.BlockSpec((None, Q_BLOCK, LANES), qrow),
                  pl.BlockSpec((None, Q_BLOCK, LANES), qrow),
                  pl.BlockSpec((None, Q_BLOCK, W), qrow),
                  pl.BlockSpec((None, IDX_DIM, S), whole, pipeline_mode=one),
                  pl.BlockSpec((None, W, S), whole, pipeline_mode=one),
                  pl.BlockSpec((None, S, W), whole, pipeline_mode=one),
                  pl.BlockSpec((DSA_HEADS, REL_SLABS, LANES, LANES), lambda b, i: (0, 0, 0, 0)),
                  pl.BlockSpec((LANES, 2 * LANES), lambda b, i: (0, 0))],
        out_specs=pl.BlockSpec((None, Q_BLOCK, W), qrow),
        out_shape=jax.ShapeDtypeStruct((B, S, W), F32),
        scratch_shapes=[pltpu.VMEM((Q_BLOCK, S), I32),
                        pltpu.VMEM((SUBLANES, LANES), F32),
                        pltpu.VMEM((DSA_HEADS, Q_BLOCK, LANES), F32),
                        pltpu.VMEM((DSA_HEADS, Q_BLOCK, 2 * LANES), F32)],
        compiler_params=pltpu.CompilerParams(
            dimension_semantics=("parallel", "arbitrary"), vmem_limit_bytes=VMEM_LIMIT_BYTES),
        name="dsa",
    )(iq, iw, q, kit, kt, v, tb, uo)


def _layer_norm(z, g, b):
    mu = jnp.mean(z, axis=-1, keepdims=True)
    zc = z - mu
    var = jnp.mean(zc * zc, axis=-1, keepdims=True)
    return zc * lax.rsqrt(var + LN_EPS) * g + b


def _mix_kernel(x_ref, ya_ref, yb_ref, yc_ref, wa_ref, wb_ref, wc_ref, g_ref, b_ref, o_ref):
    mix = jnp.dot(ya_ref[...].astype(BF16), wa_ref[...], preferred_element_type=F32)
    mix = mix + jnp.dot(yb_ref[...].astype(BF16), wb_ref[...], preferred_element_type=F32)
    mix = mix + jnp.dot(yc_ref[...].astype(BF16), wc_ref[...], preferred_element_type=F32)
    o_ref[...] = _layer_norm(DN_ALPHA * x_ref[...] + mix, g_ref[...], b_ref[...])


def _mix(x, ya, yb, yc, wa, wb, wc, g, b, *, tm):
    T, D = x.shape
    row = lambda m: (m, 0)
    const = lambda m: (0, 0)
    return pl.pallas_call(
        _mix_kernel,
        grid=(T // tm,),
        in_specs=[pl.BlockSpec((tm, D), row),
                  pl.BlockSpec((tm, ya.shape[1]), row), pl.BlockSpec((tm, yb.shape[1]), row),
                  pl.BlockSpec((tm, yc.shape[1]), row),
                  pl.BlockSpec(wa.shape, const), pl.BlockSpec(wb.shape, const), pl.BlockSpec(wc.shape, const),
                  pl.BlockSpec((1, D), const), pl.BlockSpec((1, D), const)],
        out_specs=pl.BlockSpec((tm, D), row),
        out_shape=jax.ShapeDtypeStruct((T, D), F32),
        compiler_params=pltpu.CompilerParams(
            dimension_semantics=("parallel",), vmem_limit_bytes=VMEM_LIMIT_BYTES),
        name="out_proj_ln",
    )(x, ya, yb, yc, wa, wb, wc, g, b)


def _ffn_kernel(x_ref, wg_ref, wu_ref, wd_ref, g_ref, b_ref, o_ref, xb_ref, acc_ref):
    f = pl.program_id(1)

    @pl.when(f == 0)
    def _():
        xb_ref[...] = x_ref[...].astype(BF16)
        acc_ref[...] = jnp.zeros_like(acc_ref)

    xb = xb_ref[...]
    gate = jnp.dot(xb, wg_ref[...], preferred_element_type=F32)
    up = jnp.dot(xb, wu_ref[...], preferred_element_type=F32)
    h = (gate * _sigmoid(gate)) * up
    acc_ref[...] += jnp.dot(h.astype(BF16), wd_ref[...], preferred_element_type=F32)

    @pl.when(f == pl.num_programs(1) - 1)
    def _():
        o_ref[...] = _layer_norm(DN_ALPHA * x_ref[...] + acc_ref[...], g_ref[...], b_ref[...])


def _ffn(x, wg, wu, wd, g, b, *, tm, tf):
    T, D = x.shape
    FF = wg.shape[1]
    row = lambda m, f: (m, 0)
    const = lambda m, f: (0, 0)
    return pl.pallas_call(
        _ffn_kernel,
        grid=(T // tm, FF // tf),
        in_specs=[pl.BlockSpec((tm, D), row),
                  pl.BlockSpec((D, tf), lambda m, f: (0, f)),
                  pl.BlockSpec((D, tf), lambda m, f: (0, f)),
                  pl.BlockSpec((tf, D), lambda m, f: (f, 0)),
                  pl.BlockSpec((1, D), const), pl.BlockSpec((1, D), const)],
        out_specs=pl.BlockSpec((tm, D), row),
        out_shape=jax.ShapeDtypeStruct((T, D), F32),
        scratch_shapes=[pltpu.VMEM((tm, D), BF16), pltpu.VMEM((tm, D), F32)],
        compiler_params=pltpu.CompilerParams(
            dimension_semantics=("parallel", "arbitrary"), vmem_limit_bytes=VMEM_LIMIT_BYTES),
        name="ffn_ln",
    )(x, wg, wu, wd, g, b)


def _split_w_in(w):
    parts = [w[:, IN_OFFSETS[j]:IN_OFFSETS[j + 1]] for j in range(len(IN_SIZES))]
    lru_x, lru_g, gq, gk, gv, gg, g_lr, dq, dk, dv, iq, ik, iw = parts
    D = w.shape[0]
    pad = lambda a, n: jnp.pad(a, ((0, 0), (0, n - a.shape[1])))
    wa = jnp.concatenate([lru_x, lru_g, gq, gk, gv, gg, pad(g_lr, LANES),
                          dq * (DSA_HEAD_DIM ** -0.5), dv, iq, pad(iw, LANES)], axis=1)
    wt = jnp.concatenate([dk, ik], axis=1).T
    assert wa.shape == (D, PA_END) and wt.shape == (PT_ROWS, D)
    return wa.astype(BF16), wt.astype(BF16)


def _block_diag(w):
    n, k, _ = w.shape
    eye = jnp.eye(n, dtype=w.dtype)
    return (eye[:, None, :, None] * w[:, :, None, :]).reshape(n * k, n * k)


def kernel(x, w_in, conv_w, conv_b, lru_wa, lru_ba, lru_wx, lru_bx, lru_lambda, gla_w_gate2, gla_b_gate,
           gla_norm_g, rel_bias, w_out, ln1_g, ln1_b, w_ffn_gate, w_ffn_up, w_ffn_down, ln2_g, ln2_b):
    B, S, D = x.shape
    T = B * S
    top_k = min(IDX_TOPK_MAX, S // 4)
    assert S % (2 * KEY_CHUNK) == 0 and D == D_MODEL
    tm = min(512, S)
    tb = _relbias_tiles(rel_bias)
    row = lambda a: a.reshape(1, -1)
    for l in range(w_in.shape[0]):
        wa, wt = _split_w_in(w_in[l])
        p_lru, p_gla, dq, dv, iq, iw, kt, kit = _proj(x, wa, wt, tm=tm)
        y_lru = _lru(p_lru, conv_w[l], row(conv_b[l]),
                     _block_diag(lru_wa[l]).astype(BF16), row(lru_ba[l]),
                     _block_diag(lru_wx[l]).astype(BF16), row(lru_bx[l]), row(lru_lambda[l]), tl=tm)
        w2 = jnp.pad(gla_w_gate2[l], ((0, LANES - GLA_GATE_RANK), (0, 0)))
        y_gla = _gla(p_gla, w2, row(gla_b_gate[l]), row(gla_norm_g[l]), tg=min(256, S))
        y_dsa = _dsa(iq, iw, dq, kit, kt, dv, tb, top_k=top_k)
        wo = w_out[l].astype(BF16)
        x1 = _mix(x.reshape(T, D), y_lru.reshape(T, -1), y_gla.reshape(T, -1), y_dsa.reshape(T, -1),
                  wo[:LRU_WIDTH], wo[LRU_WIDTH:LRU_WIDTH + GLA_WIDTH], wo[LRU_WIDTH + GLA_WIDTH:],
                  row(ln1_g[l]), row(ln1_b[l]), tm=tm)
        x2 = _ffn(x1, w_ffn_gate[l].astype(BF16), w_ffn_up[l].astype(BF16), w_ffn_down[l].astype(BF16),
                  row(ln2_g[l]), row(ln2_b[l]), tm=tm, tf=D_FF // 2)
        x = x2.reshape(B, S, D)
    return x
```

```python
import functools
import math

import jax
import jax.numpy as jnp
import numpy as np
from jax import lax
from jax.experimental import pallas as pl
from jax.experimental.pallas import tpu as pltpu

F32 = jnp.float32
BF16 = jnp.bfloat16
I32 = jnp.int32

D_MODEL = 1024
DEPTH = 2
LRU_WIDTH = 256
LRU_BLOCKS = 4
LRU_BLOCK = LRU_WIDTH // LRU_BLOCKS
CONV_WIDTH = 4
LRU_C = 8.0
GLA_HEADS = 6
GLA_DK = 64
GLA_DV = 64
GLA_GATE_RANK = 16
GLA_GATE_TAU = 16.0
GLA_CHUNK = 64
GLA_WIDTH = GLA_HEADS * GLA_DV
DSA_HEADS = 6
DSA_HEAD_DIM = 64
DSA_WIDTH = DSA_HEADS * DSA_HEAD_DIM
IDX_HEADS = 4
IDX_DIM = 32
IDX_TOPK_MAX = 256
REL_BUCKETS = 32
REL_MAX_DIST = 128
D_FF = 2816
DN_ALPHA = (2.0 * DEPTH) ** 0.25
LN_EPS = 1e-5

IN_SIZES = (LRU_WIDTH, LRU_WIDTH, GLA_WIDTH, GLA_WIDTH, GLA_WIDTH, GLA_WIDTH, GLA_GATE_RANK,
            DSA_WIDTH, DSA_WIDTH, DSA_WIDTH, IDX_HEADS * IDX_DIM, IDX_DIM, IDX_HEADS)
IN_OFFSETS = [0] + [int(o) for o in np.cumsum(IN_SIZES)]

LANES = 128
SUBLANES = 8
VMEM_LIMIT_BYTES = 56 * 1024 * 1024

LRU_COLS = 2 * LRU_WIDTH
GLA_COLS = 4 * GLA_WIDTH + LANES
PA_LRU = 0
PA_GLA = PA_LRU + LRU_COLS
PA_DQ = PA_GLA + GLA_COLS
PA_DV = PA_DQ + DSA_WIDTH
PA_IQ = PA_DV + DSA_WIDTH
PA_IW = PA_IQ + LANES
PA_END = PA_IW + LANES
PT_ROWS = DSA_WIDTH + IDX_DIM

Q_BLOCK = 256
Q_SUB = Q_BLOCK // LANES
KEY_CHUNK = 512
HALF_BITS = 16
INT_MIN = -2147483648
NEG_INF_KEY = -2139095041
NEG_BIG = -1e30
LOG2E = 1.4426950408889634
REL_SLABS = 3
FAST_SOFTMAX_MAX_SHIFT = 120.0


def _nt_dot(a, b):
    return lax.dot_general(a, b, (((1,), (1,)), ((), ())), preferred_element_type=F32)


def _tn_dot(a, b):
    return lax.dot_general(a, b, (((0,), (0,)), ((), ())), preferred_element_type=F32)


def _split3(x):
    x1 = x.astype(BF16)
    r1 = x - x1.astype(F32)
    x2 = r1.astype(BF16)
    return x1, x2, (r1 - x2.astype(F32)).astype(BF16)


def _softplus(z):
    return jnp.maximum(z, 0.0) + jnp.log(1.0 + jnp.exp(-jnp.abs(z)))


def _sigmoid(z):
    return 1.0 / (1.0 + jnp.exp(-z))


def _proj_kernel(x_ref, wa_ref, wt_ref, lru_ref, gla_ref, q_ref, v_ref, iq_ref, iw_ref, kt_ref, kit_ref):
    xb = x_ref[...].astype(BF16)

    def mm(c0, c1):
        return jnp.dot(xb, wa_ref[:, c0:c1], preferred_element_type=F32)

    lru_ref[...] = mm(PA_LRU, PA_GLA)
    gla_ref[...] = mm(PA_GLA, PA_DQ)
    q_ref[...] = mm(PA_DQ, PA_DV)
    v_ref[...] = mm(PA_DV, PA_IQ).astype(BF16)
    iq_ref[...] = mm(PA_IQ, PA_IW).astype(BF16)
    iw_ref[...] = mm(PA_IW, PA_END)
    t = _nt_dot(wt_ref[...], xb)
    kt_ref[...] = t[:DSA_WIDTH].astype(BF16)
    kit_ref[...] = t[DSA_WIDTH:PT_ROWS].astype(BF16)


def _proj(x, wa, wt, *, tm):
    B, S, D = x.shape
    grid = (B, S // tm)
    row = lambda b, m: (b, m, 0)
    col = lambda b, m: (b, 0, m)
    const = lambda b, m: (0, 0)
    out_shape = (
        jax.ShapeDtypeStruct((B, S, LRU_COLS), F32),
        jax.ShapeDtypeStruct((B, S, GLA_COLS), F32),
        jax.ShapeDtypeStruct((B, S, DSA_WIDTH), F32),
        jax.ShapeDtypeStruct((B, S, DSA_WIDTH), BF16),
        jax.ShapeDtypeStruct((B, S, LANES), BF16),
        jax.ShapeDtypeStruct((B, S, LANES), F32),
        jax.ShapeDtypeStruct((B, DSA_WIDTH, S), BF16),
        jax.ShapeDtypeStruct((B, IDX_DIM, S), BF16),
    )
    out_specs = (
        pl.BlockSpec((None, tm, LRU_COLS), row),
        pl.BlockSpec((None, tm, GLA_COLS), row),
        pl.BlockSpec((None, tm, DSA_WIDTH), row),
        pl.BlockSpec((None, tm, DSA_WIDTH), row),
        pl.BlockSpec((None, tm, LANES), row),
        pl.BlockSpec((None, tm, LANES), row),
        pl.BlockSpec((None, DSA_WIDTH, tm), col),
        pl.BlockSpec((None, IDX_DIM, tm), col),
    )
    return pl.pallas_call(
        _proj_kernel,
        grid=grid,
        in_specs=[pl.BlockSpec((None, tm, D), row),
                  pl.BlockSpec((D, PA_END), const),
                  pl.BlockSpec((PT_ROWS, D), const)],
        out_specs=out_specs,
        out_shape=out_shape,
        compiler_params=pltpu.CompilerParams(
            dimension_semantics=("parallel", "parallel"), vmem_limit_bytes=VMEM_LIMIT_BYTES),
        name="in_proj",
    )(x, wa, wt)


def _shift_rows(x, d, fill, row):
    return jnp.where(row >= d, pltpu.roll(x, d, 0), fill)


def _lru_kernel(p_ref, cw_ref, cb_ref, wa_ref, ba_ref, wx_ref, bx_ref, lam_ref, y_ref, xprev_ref, h_ref):
    tl = p_ref.shape[0]

    @pl.when(pl.program_id(1) == 0)
    def _():
        xprev_ref[...] = jnp.zeros_like(xprev_ref)
        h_ref[...] = jnp.zeros_like(h_ref)

    xb = p_ref[:, 0:LRU_WIDTH]
    gb = p_ref[:, LRU_WIDTH:2 * LRU_WIDTH]
    prev = xprev_ref[...]
    row8 = lax.broadcasted_iota(I32, (SUBLANES, LRU_WIDTH), 0)
    cw = cw_ref[...]
    xc = cb_ref[...] + cw[CONV_WIDTH - 1:CONV_WIDTH] * xb
    for d in range(1, CONV_WIDTH):
        r = pltpu.roll(xb, d, 0)
        top = jnp.where(row8 < d, pltpu.roll(prev, d, 0), r[0:SUBLANES])
        r = jnp.concatenate([top, r[SUBLANES:]], axis=0)
        xc = xc + cw[CONV_WIDTH - 1 - d:CONV_WIDTH - d] * r
    xprev_ref[...] = xb[tl - SUBLANES:tl]

    xcb = xc.astype(BF16)
    r_gate = _sigmoid(jnp.dot(xcb, wa_ref[...], preferred_element_type=F32) + ba_ref[...])
    i_gate = _sigmoid(jnp.dot(xcb, wx_ref[...], preferred_element_type=F32) + bx_ref[...])
    log_a = (-LRU_C) * r_gate * _softplus(-lam_ref[...])
    a = jnp.exp(log_a)
    u = jnp.sqrt(1.0 - a * a) * (i_gate * xc)

    row = lax.broadcasted_iota(I32, (tl, LRU_WIDTH), 0)
    d = 1
    while d < tl:
        a_s = _shift_rows(a, d, 1.0, row)
        u_s = _shift_rows(u, d, 0.0, row)
        u = u + a * u_s
        a = a * a_s
        d *= 2
    h = u + a * h_ref[SUBLANES - 1:SUBLANES, :]
    h_ref[...] = h[tl - SUBLANES:tl]
    y_ref[...] = h * jax.nn.gelu(gb)


def _lru(p_lru, cw, cb, wa, ba, wx, bx, lam, *, tl):
    B, S, _ = p_lru.shape
    const = lambda b, j: (0, 0)
    vec = pl.BlockSpec((1, LRU_WIDTH), const)
    mat = pl.BlockSpec((LRU_WIDTH, LRU_WIDTH), const)
    return pl.pallas_call(
        _lru_kernel,
        grid=(B, S // tl),
        in_specs=[pl.BlockSpec((None, tl, LRU_COLS), lambda b, j: (b, j, 0)),
                  pl.BlockSpec((CONV_WIDTH, LRU_WIDTH), const), vec, mat, vec, mat, vec, vec],
        out_specs=pl.BlockSpec((None, tl, LRU_WIDTH), lambda b, j: (b, j, 0)),
        out_shape=jax.ShapeDtypeStruct((B, S, LRU_WIDTH), F32),
        scratch_shapes=[pltpu.VMEM((SUBLANES, LRU_WIDTH), F32), pltpu.VMEM((SUBLANES, LRU_WIDTH), F32)],
        compiler_params=pltpu.CompilerParams(
            dimension_semantics=("parallel", "arbitrary"), vmem_limit_bytes=VMEM_LIMIT_BYTES),
        name="rg_lru",
    )(p_lru, cw, cb, wa, ba, wx, bx, lam)


def _gla_kernel(q_ref, k_ref, v_ref, g_ref, glr_ref, w2_ref, bg_ref, ng_ref, tri_ref, mean_ref,
                y_ref, st_ref):
    tg = q_ref.shape[0]
    C = GLA_CHUNK
    hi = lax.Precision.HIGHEST

    @pl.when(pl.program_id(1) == 0)
    def _():
        st_ref[...] = jnp.zeros_like(st_ref)

    z = jnp.dot(glr_ref[...], w2_ref[...], preferred_element_type=F32, precision=hi) + bg_ref[...]
    log_alpha = -_softplus(-z) * (1.0 / GLA_GATE_TAU)

    lane = lax.broadcasted_iota(I32, (C, LANES), 1)
    first = lane < GLA_DK
    rr = lax.broadcasted_iota(I32, (C, C), 0)
    cc = lax.broadcasted_iota(I32, (C, C), 1)
    causal = cc <= rr
    causal2 = jnp.concatenate([causal, causal], axis=0)
    r2 = lax.broadcasted_iota(I32, (LANES, LANES), 0)
    c2 = lax.broadcasted_iota(I32, (LANES, LANES), 1)
    same_head = (r2 < GLA_DV) == (c2 < GLA_DK)
    tri = tri_ref[...]

    for c in range(tg // C):
        rows = slice(c * C, (c + 1) * C)
        bcum = sum(jnp.dot(tri, piece, preferred_element_type=F32)
                   for piece in _split3(log_alpha[rows]))
        blast = bcum[C - 1:C]
        kf = k_ref[rows, :]
        q_dec = q_ref[rows, :] * (GLA_DK ** -0.5) * jnp.exp(bcum)
        k_inv = kf * jnp.exp(-bcum)
        k_end = kf * jnp.exp(blast - bcum)
        decay = jnp.exp(blast)
        vf = v_ref[rows, :]
        outs = []
        for p in range(GLA_HEADS // 2):
            cs = slice(p * LANES, (p + 1) * LANES)
            qd, ki, ke, vp = q_dec[:, cs], k_inv[:, cs], k_end[:, cs], vf[:, cs]
            kib = ki.astype(BF16)
            vpb = vp.astype(BF16)
            q_two = jnp.concatenate([jnp.where(first, qd, 0.0), jnp.where(first, 0.0, qd)], axis=0).astype(BF16)
            att = jnp.where(causal2, _nt_dot(q_two, kib), 0.0).astype(BF16)
            o_two = jnp.dot(att, vpb, preferred_element_type=F32)
            o_intra = jnp.where(first, o_two[:C], o_two[C:])
            st = st_ref[p]
            o_inter = _nt_dot(qd.astype(BF16), st.astype(BF16))
            u_t = _tn_dot(vpb, ke.astype(BF16))
            st_ref[p] = st * decay[:, cs] + jnp.where(same_head, u_t, 0.0)
            outs.append(o_intra + o_inter)
        o = jnp.concatenate(outs, axis=1)
        ms = sum(jnp.dot(piece, mean_ref[...], preferred_element_type=F32)
                 for piece in _split3(o * o))
        o = o * lax.rsqrt(ms + 1e-6) * ng_ref[...]
        gf = g_ref[rows, :]
        y_ref[rows, :] = o * (gf * _sigmoid(gf))


def _gla(p_gla, w2, bg, ng, *, tg):
    B, S, _ = p_gla.shape
    W = GLA_WIDTH
    const = lambda b, j: (0, 0)
    tri = jnp.tril(jnp.ones((GLA_CHUNK, GLA_CHUNK), F32)).astype(BF16)
    head = jnp.arange(W) // GLA_DV
    mean_blk = ((head[:, None] == head[None, :]).astype(F32) / GLA_DV).astype(BF16)

    def colblk(c):
        return pl.BlockSpec((None, tg, W), lambda b, j: (b, j, c))

    return pl.pallas_call(
        _gla_kernel,
        grid=(B, S // tg),
        in_specs=[colblk(0), colblk(1), colblk(2), colblk(3),
                  pl.BlockSpec((None, tg, LANES), lambda b, j: (b, j, 4 * W // LANES)),
                  pl.BlockSpec((LANES, W), const),
                  pl.BlockSpec((1, W), const), pl.BlockSpec((1, W), const),
                  pl.BlockSpec((GLA_CHUNK, GLA_CHUNK), const),
                  pl.BlockSpec((W, W), const)],
        out_specs=pl.BlockSpec((None, tg, W), lambda b, j: (b, j, 0)),
        out_shape=jax.ShapeDtypeStruct((B, S, W), F32),
        scratch_shapes=[pltpu.VMEM((GLA_HEADS // 2, LANES, LANES), F32)],
        compiler_params=pltpu.CompilerParams(
            dimension_semantics=("parallel", "arbitrary"), vmem_limit_bytes=VMEM_LIMIT_BYTES),
        name="gla",
    )(p_gla, p_gla, p_gla, p_gla, p_gla, w2, bg, ng, tri, mean_blk)


def _relbias_kernel(rb_ref, tb_ref):
    r = lax.broadcasted_iota(I32, (LANES, LANES), 0)
    c = lax.broadcasted_iota(I32, (LANES, LANES), 1)
    max_exact = REL_BUCKETS // 2
    for delta in range(REL_SLABS):
        n = jnp.maximum(delta * LANES + r - c, 0)
        nf = jnp.maximum(n, 1).astype(F32)
        large = max_exact + (jnp.log(nf / max_exact) / math.log(REL_MAX_DIST / max_exact)
                             * (REL_BUCKETS - max_exact)).astype(I32)
        large = jnp.minimum(large, REL_BUCKETS - 1)
        bucket = jnp.where(n < max_exact, n, large)
        for h in range(DSA_HEADS):
            acc = jnp.zeros((LANES, LANES), F32)
            for b in range(REL_BUCKETS):
                acc = jnp.where(bucket == b, rb_ref[b, h], acc)
            tb_ref[h, delta] = (acc - rb_ref[REL_BUCKETS - 1, h]) * LOG2E


def _relbias_tiles(rel_bias):
    return pl.pallas_call(
        _relbias_kernel,
        in_specs=[pl.BlockSpec(memory_space=pltpu.SMEM)],
        out_specs=pl.BlockSpec(memory_space=pltpu.VMEM),
        out_shape=jax.ShapeDtypeStruct((DSA_HEADS, REL_SLABS, LANES, LANES), F32),
        name="rel_bias_tiles",
    )(rel_bias)


def _dsa_kernel(iq_ref, iw_ref, q_ref, kit_ref, kt_ref, v_ref, tb_ref, uo_ref, y_ref,
                key_ref, kmx_ref, m_ref, acc_ref, *, top_k):
    i = pl.program_id(1)
    TQ, CK, L = Q_BLOCK, KEY_CHUNK, LANES
    S = key_ref.shape[1]
    first_blk = i * Q_SUB
    n_chunks = (first_blk + Q_SUB - 1) // (CK // L) + 1
    lane = lax.broadcasted_iota(I32, (TQ, L), 1)

    @pl.when(i == 0)
    def _():
        nb = 2 * CK
        for h in range(DSA_HEADS):
            def norm_chunk(c, mx):
                kk = kt_ref[h * DSA_HEAD_DIM:(h + 1) * DSA_HEAD_DIM, pl.ds(pl.multiple_of(c * nb, nb), nb)]
                kk = kk.astype(F32)
                return jnp.maximum(mx, jnp.sum(kk * kk, axis=0, keepdims=True))
            mx = lax.fori_loop(0, S // nb, norm_chunk, jnp.zeros((1, nb), F32))
            kmx_ref[h:h + 1, :] = jnp.broadcast_to(jnp.max(mx, axis=1, keepdims=True), (1, L))

    iq = iq_ref[...]
    iq_h = [iq[:, h * IDX_DIM:(h + 1) * IDX_DIM] for h in range(IDX_HEADS)]
    scale = (IDX_DIM ** -0.5) * (IDX_HEADS ** -0.5)
    w_h = [jnp.broadcast_to(iw_ref[:, h:h + 1] * scale, (TQ, CK)) for h in range(IDX_HEADS)]

    def score_chunk(c, masked):
        off = pl.multiple_of(c * CK, CK)
        kc = kit_ref[:, pl.ds(off, CK)]
        sc = jnp.zeros((TQ, CK), F32)
        for h in range(IDX_HEADS):
            z = jnp.dot(iq_h[h], kc, preferred_element_type=F32)
            sc = sc + jnp.maximum(z, 0.0) * w_h[h]
        sc = sc + 0.0
        if masked:
            row_t = i * TQ + lax.broadcasted_iota(I32, (TQ, CK), 0)
            pos = off + lax.broadcasted_iota(I32, (TQ, CK), 1)
            sc = jnp.where(pos <= row_t, sc, -jnp.inf)
        bits = pltpu.bitcast(sc, I32)
        key_ref[:, pl.ds(off, CK)] = bits ^ ((bits >> 31) & I32(0x7FFFFFFF))

    def score_body(c, carry):
        score_chunk(c, False)
        return carry

    lax.fori_loop(0, n_chunks - 1, score_body, 0)
    score_chunk(n_chunks - 1, True)

    @pl.when((n_chunks & 1) == 1)
    def _():
        key_ref[:, pl.ds(pl.multiple_of(n_chunks * CK, CK), CK)] = jnp.full((TQ, CK), NEG_INF_KEY, I32)

    def count_ge(cand):
        cand_all = jnp.broadcast_to(cand, (TQ, L))
        accs = []
        for r in range(Q_SUB):
            rows = slice(r * L, (r + 1) * L)
            cand_b = cand_all[rows]

            def body(c, acc, rows=rows, cand_b=cand_b):
                kk = key_ref[rows, pl.ds(pl.multiple_of(c * (2 * CK), 2 * CK), 2 * CK)]
                for s in range(2 * CK // L):
                    acc = acc + (kk[:, s * L:(s + 1) * L] >= cand_b).astype(I32)
                return acc
            accs.append(lax.fori_loop(0, (n_chunks + 1) >> 1, body, jnp.zeros((L, L), I32)))
        acc = accs[0] if Q_SUB == 1 else jnp.concatenate(accs, axis=0)
        return jnp.sum(acc, axis=1, keepdims=True)

    def bit_step(thr, cnt_thr, bit):
        cand = thr + bit
        cnt = count_ge(cand)
        ok = cnt >= top_k
        return jnp.where(ok, cand, thr), jnp.where(ok, cnt, cnt_thr)

    def high_step(b, st):
        return bit_step(st[0], st[1], lax.shift_left(I32(1), I32(31) - b))

    thr, cnt_thr = lax.fori_loop(
        0, 32 - HALF_BITS, high_step,
        (jnp.full((TQ, 1), INT_MIN, I32), jnp.full((TQ, 1), S, I32)))

    cnt_next = count_ge(thr + 1)
    exact = cnt_next < top_k

    def low_cond(st):
        b, _, _, active = st
        return (b < HALF_BITS) & (active > 0)

    def low_step(st):
        b, thr, cnt_thr, _ = st
        new_thr, new_cnt = bit_step(thr, cnt_thr, lax.shift_left(I32(1), I32(HALF_BITS - 1) - b))
        thr = jnp.where(exact, thr, new_thr)
        cnt_thr = jnp.where(exact, cnt_thr, new_cnt)
        active = jnp.max(jnp.where(exact | (cnt_thr == top_k), 0, 1))
        return b + 1, thr, cnt_thr, active

    active0 = jnp.max(jnp.where(exact | (cnt_thr == top_k), 0, 1))
    _, thr, cnt_thr, _ = lax.while_loop(low_cond, low_step, (I32(0), thr, cnt_thr, active0))

    t_sel = jnp.where(cnt_thr == top_k, thr - 1, thr)
    need = (top_k - count_ge(t_sel + 1)).astype(F32)
    t_sel_b = jnp.broadcast_to(t_sel, (TQ, L))
    need_b = jnp.broadcast_to(need, (TQ, L))
    uo = uo_ref[...]

    def mask_chunk(c, run):
        off = pl.multiple_of(c * CK, CK)
        kc = key_ref[:, pl.ds(off, CK)]
        out = []
        for s in range(CK // L):
            kk = kc[:, s * L:(s + 1) * L]
            eq = kk == t_sel_b
            pr = jnp.dot(jnp.where(eq, 1.0, 0.0).astype(BF16), uo, preferred_element_type=F32)
            sel = (kk > t_sel_b) | (eq & (run + pr[:, :L] <= need_b))
            sel = sel & (kk > NEG_INF_KEY)
            out.append(jnp.where(sel, 0.0, -jnp.inf).astype(F32))
            run = run + pr[:, L:]
        key_ref[:, pl.ds(off, CK)] = pltpu.bitcast(jnp.concatenate(out, axis=1), I32)
        return run

    lax.fori_loop(0, n_chunks, mask_chunk, jnp.zeros((TQ, L), F32))

    q2 = q_ref[...] * LOG2E
    q_ext = []
    shift_max = jnp.zeros((TQ, L), F32)
    for h in range(DSA_HEADS):
        blk = q2[:, (h // 2) * L:(h // 2 + 1) * L]
        if h % 2:
            blk = pltpu.roll(blk, DSA_HEAD_DIM, 1)
        qf = jnp.where(lane < DSA_HEAD_DIM, blk, 0.0).astype(BF16).astype(F32)
        bound = jnp.sqrt(jnp.sum(qf * qf, axis=1, keepdims=True) * kmx_ref[h:h + 1, :]) * (1.0 + 2.0 ** -7)
        shift_max = jnp.maximum(shift_max, bound)
        q_ext.append(jnp.where(lane == DSA_HEAD_DIM, bound, qf).astype(BF16))
    minus_one_row = jnp.where(lax.broadcasted_iota(I32, (DSA_HEAD_DIM, CK), 0) == 0, -1.0, 0.0).astype(BF16)
    ones_cols = jnp.ones((CK, L), BF16)

    acc_ref[...] = jnp.zeros(acc_ref.shape, F32)

    def attend(c, biased, fast):
        off = pl.multiple_of(c * CK, CK)
        width = CK
        mb = pltpu.bitcast(key_ref[:, pl.ds(off, width)], F32)
        for h in range(DSA_HEADS):
            k_ext = jnp.concatenate(
                [kt_ref[h * DSA_HEAD_DIM:(h + 1) * DSA_HEAD_DIM, pl.ds(off, width)], minus_one_row], axis=0)
            s = jnp.dot(q_ext[h], k_ext, preferred_element_type=F32) + mb
            if biased:
                def tile(r, j):
                    return tb_ref[h, jnp.clip(first_blk + r - (c * (CK // L) + j), 0, REL_SLABS - 1)]
                s = s + jnp.concatenate(
                    [jnp.concatenate([tile(r, j) for r in range(Q_SUB)], axis=0) for j in range(CK // L)], axis=1)
            pair = (h // 2) * L
            v_ext = jnp.concatenate([v_ref[pl.ds(off, width), pair:pair + L], ones_cols], axis=1)
            if fast:
                p = jnp.exp2(s).astype(BF16)
                acc_ref[h] += jnp.dot(p, v_ext, preferred_element_type=F32)
            else:
                m_old = m_ref[h]
                m_new = jnp.maximum(m_old, jnp.max(s, axis=1, keepdims=True))
                alpha = jnp.exp2(m_old - m_new)
                p = jnp.concatenate(
                    [jnp.exp2(s[:, j * L:(j + 1) * L] - m_new) for j in range(width // L)], axis=1)
                pv = jnp.dot(p.astype(BF16), v_ext, preferred_element_type=F32)
                acc_ref[h] = acc_ref[h] * jnp.concatenate([alpha, alpha], axis=1) + pv
                m_ref[h] = m_new

    def attend_all(fast):
        first_biased = jnp.maximum(first_blk - 1, 0) // (CK // L)

        def far_chunk(c, carry):
            attend(c, False, fast)
            return carry

        def near_chunk(c, carry):
            attend(c, True, fast)
            return carry

        lax.fori_loop(0, first_biased, far_chunk, 0)
        lax.fori_loop(first_biased, n_chunks, near_chunk, 0)

    use_fast = 2.0 * jnp.max(shift_max) + jnp.max(jnp.abs(tb_ref[...])) <= FAST_SOFTMAX_MAX_SHIFT

    @pl.when(use_fast)
    def _():
        attend_all(True)

    @pl.when(jnp.logical_not(use_fast))
    def _():
        m_ref[...] = jnp.full(m_ref.shape, NEG_BIG, F32)
        attend_all(False)

    for p in range(DSA_HEADS // 2):
        a0, a1 = acc_ref[2 * p], acc_ref[2 * p + 1]
        y_ref[:, p * L:(p + 1) * L] = jnp.where(lane < DSA_HEAD_DIM, a0[:, :L] / a0[:, L:], a1[:, :L] / a1[:, L:])


def _dsa(iq, iw, q, kit, kt, v, tb, *, top_k):
    B, S, W = q.shape
    upper = jnp.triu(jnp.ones((LANES, LANES), F32))
    uo = jnp.concatenate([upper, jnp.ones((LANES, LANES), F32)], axis=1).astype(BF16)
    qrow = lambda b, i: (b, i, 0)
    whole = lambda b, i: (b, 0, 0)
    one = pl.Buffered(1)
    return pl.pallas_call(
        functools.partial(_dsa_kernel, top_k=top_k),
        grid=(B, S // Q_BLOCK),
        in_specs=[pl.BlockSpec((None, Q_BLOCK, LANES), qrow),
                  pl.BlockSpec((None, Q_BLOCK, LANES), qrow),
                  pl.BlockSpec((None, Q_BLOCK, W), qrow),
                  pl.BlockSpec((None, IDX_DIM, S), whole, pipeline_mode=one),
                  pl.BlockSpec((None, W, S), whole, pipeline_mode=one),
                  pl.BlockSpec((None, S, W), whole, pipeline_mode=one),
                  pl.BlockSpec((DSA_HEADS, REL_SLABS, LANES, LANES), lambda b, i: (0, 0, 0, 0)),
                  pl.BlockSpec((LANES, 2 * LANES), lambda b, i: (0, 0))],
        out_specs=pl.BlockSpec((None, Q_BLOCK, W), qrow),
        out_shape=jax.ShapeDtypeStruct((B, S, W), F32),
        scratch_shapes=[pltpu.VMEM((Q_BLOCK, S), I32),
                        pltpu.VMEM((SUBLANES, LANES), F32),
                        pltpu.VMEM((DSA_HEADS, Q_BLOCK, LANES), F32),
                        pltpu.VMEM((DSA_HEADS, Q_BLOCK, 2 * LANES), F32)],
        compiler_params=pltpu.CompilerParams(
            dimension_semantics=("parallel", "arbitrary"), vmem_limit_bytes=VMEM_LIMIT_BYTES),
        name="dsa",
    )(iq, iw, q, kit, kt, v, tb, uo)


def _layer_norm(z, g, b):
    mu = jnp.mean(z, axis=-1, keepdims=True)
    zc = z - mu
    var = jnp.mean(zc * zc, axis=-1, keepdims=True)
    return zc * lax.rsqrt(var + LN_EPS) * g + b


def _mix_kernel(x_ref, ya_ref, yb_ref, yc_ref, wa_ref, wb_ref, wc_ref, g_ref, b_ref, o_ref):
    mix = jnp.dot(ya_ref[...].astype(BF16), wa_ref[...], preferred_element_type=F32)
    mix = mix + jnp.dot(yb_ref[...].astype(BF16), wb_ref[...], preferred_element_type=F32)
    mix = mix + jnp.dot(yc_ref[...].astype(BF16), wc_ref[...], preferred_element_type=F32)
    o_ref[...] = _layer_norm(DN_ALPHA * x_ref[...] + mix, g_ref[...], b_ref[...])


def _mix(x, ya, yb, yc, wa, wb, wc, g, b, *, tm):
    T, D = x.shape
    row = lambda m: (m, 0)
    const = lambda m: (0, 0)
    return pl.pallas_call(
        _mix_kernel,
        grid=(T // tm,),
        in_specs=[pl.BlockSpec((tm, D), row),
                  pl.BlockSpec((tm, ya.shape[1]), row), pl.BlockSpec((tm, yb.shape[1]), row),
                  pl.BlockSpec((tm, yc.shape[1]), row),
                  pl.BlockSpec(wa.shape, const), pl.BlockSpec(wb.shape, const), pl.BlockSpec(wc.shape, const),
                  pl.BlockSpec((1, D), const), pl.BlockSpec((1, D), const)],
        out_specs=pl.BlockSpec((tm, D), row),
        out_shape=jax.ShapeDtypeStruct((T, D), F32),
        compiler_params=pltpu.CompilerParams(
            dimension_semantics=("parallel",), vmem_limit_bytes=VMEM_LIMIT_BYTES),
        name="out_proj_ln",
    )(x, ya, yb, yc, wa, wb, wc, g, b)


def _ffn_kernel(x_ref, wg_ref, wu_ref, wd_ref, g_ref, b_ref, o_ref, xb_ref, acc_ref):
    f = pl.program_id(1)

    @pl.when(f == 0)
    def _():
        xb_ref[...] = x_ref[...].astype(BF16)
        acc_ref[...] = jnp.zeros_like(acc_ref)

    xb = xb_ref[...]
    gate = jnp.dot(xb, wg_ref[...], preferred_element_type=F32)
    up = jnp.dot(xb, wu_ref[...], preferred_element_type=F32)
    h = (gate * _sigmoid(gate)) * up
    acc_ref[...] += jnp.dot(h.astype(BF16), wd_ref[...], preferred_element_type=F32)

    @pl.when(f == pl.num_programs(1) - 1)
    def _():
        o_ref[...] = _layer_norm(DN_ALPHA * x_ref[...] + acc_ref[...], g_ref[...], b_ref[...])


def _ffn(x, wg, wu, wd, g, b, *, tm, tf):
    T, D = x.shape
    FF = wg.shape[1]
    row = lambda m, f: (m, 0)
    const = lambda m, f: (0, 0)
    return pl.pallas_call(
        _ffn_kernel,
        grid=(T // tm, FF // tf),
        in_specs=[pl.BlockSpec((tm, D), row),
                  pl.BlockSpec((D, tf), lambda m, f: (0, f)),
                  pl.BlockSpec((D, tf), lambda m, f: (0, f)),
                  pl.BlockSpec((tf, D), lambda m, f: (f, 0)),
                  pl.BlockSpec((1, D), const), pl.BlockSpec((1, D), const)],
        out_specs=pl.BlockSpec((tm, D), row),
        out_shape=jax.ShapeDtypeStruct((T, D), F32),
        scratch_shapes=[pltpu.VMEM((tm, D), BF16), pltpu.VMEM((tm, D), F32)],
        compiler_params=pltpu.CompilerParams(
            dimension_semantics=("parallel", "arbitrary"), vmem_limit_bytes=VMEM_LIMIT_BYTES),
        name="ffn_ln",
    )(x, wg, wu, wd, g, b)


def _split_w_in(w):
    parts = [w[:, IN_OFFSETS[j]:IN_OFFSETS[j + 1]] for j in range(len(IN_SIZES))]
    lru_x, lru_g, gq, gk, gv, gg, g_lr, dq, dk, dv, iq, ik, iw = parts
    D = w.shape[0]
    pad = lambda a, n: jnp.pad(a, ((0, 0), (0, n - a.shape[1])))
    wa = jnp.concatenate([lru_x, lru_g, gq, gk, gv, gg, pad(g_lr, LANES),
                          dq * (DSA_HEAD_DIM ** -0.5), dv, iq, pad(iw, LANES)], axis=1)
    wt = jnp.concatenate([dk, ik], axis=1).T
    assert wa.shape == (D, PA_END) and wt.shape == (PT_ROWS, D)
    return wa.astype(BF16), wt.astype(BF16)


def _block_diag(w):
    n, k, _ = w.shape
    eye = jnp.eye(n, dtype=w.dtype)
    return (eye[:, None, :, None] * w[:, :, None, :]).reshape(n * k, n * k)


def kernel(x, w_in, conv_w, conv_b, lru_wa, lru_ba, lru_wx, lru_bx, lru_lambda, gla_w_gate2, gla_b_gate,
           gla_norm_g, rel_bias, w_out, ln1_g, ln1_b, w_ffn_gate, w_ffn_up, w_ffn_down, ln2_g, ln2_b):
    B, S, D = x.shape
    T = B * S
    top_k = min(IDX_TOPK_MAX, S // 4)
    assert S % (2 * KEY_CHUNK) == 0 and D == D_MODEL
    tm = min(512, S)
    tb = _relbias_tiles(rel_bias)
    row = lambda a: a.reshape(1, -1)
    for l in range(w_in.shape[0]):
        wa, wt = _split_w_in(w_in[l])
        p_lru, p_gla, dq, dv, iq, iw, kt, kit = _proj(x, wa, wt, tm=tm)
        y_lru = _lru(p_lru, conv_w[l], row(conv_b[l]),
                     _block_diag(lru_wa[l]).astype(BF16), row(lru_ba[l]),
                     _block_diag(lru_wx[l]).astype(BF16), row(lru_bx[l]), row(lru_lambda[l]), tl=tm)
        w2 = jnp.pad(gla_w_gate2[l], ((0, LANES - GLA_GATE_RANK), (0, 0)))
        y_gla = _gla(p_gla, w2, row(gla_b_gate[l]), row(gla_norm_g[l]), tg=min(256, S))
        y_dsa = _dsa(iq, iw, dq, kit, kt, dv, tb, top_k=top_k)
        wo = w_out[l].astype(BF16)
        x1 = _mix(x.reshape(T, D), y_lru.reshape(T, -1), y_gla.reshape(T, -1), y_dsa.reshape(T, -1),
                  wo[:LRU_WIDTH], wo[LRU_WIDTH:LRU_WIDTH + GLA_WIDTH], wo[LRU_WIDTH + GLA_WIDTH:],
                  row(ln1_g[l]), row(ln1_b[l]), tm=tm)
        x2 = _ffn(x1, w_ffn_gate[l].astype(BF16), w_ffn_up[l].astype(BF16), w_ffn_down[l].astype(BF16),
                  row(ln2_g[l]), row(ln2_b[l]), tm=tm, tf=D_FF // 2)
        x = x2.reshape(B, S, D)
    return x
```

```python
import functools
import math

import jax
import jax.numpy as jnp
import numpy as np
from jax import lax
from jax.experimental import pallas as pl
from jax.experimental.pallas import tpu as pltpu

F32 = jnp.float32
BF16 = jnp.bfloat16
I32 = jnp.int32

D_MODEL = 1024
DEPTH = 2
LRU_WIDTH = 256
LRU_BLOCKS = 4
LRU_BLOCK = LRU_WIDTH // LRU_BLOCKS
CONV_WIDTH = 4
LRU_C = 8.0
GLA_HEADS = 6
GLA_DK = 64
GLA_DV = 64
GLA_GATE_RANK = 16
GLA_GATE_TAU = 16.0
GLA_CHUNK = 64
GLA_WIDTH = GLA_HEADS * GLA_DV
DSA_HEADS = 6
DSA_HEAD_DIM = 64
DSA_WIDTH = DSA_HEADS * DSA_HEAD_DIM
IDX_HEADS = 4
IDX_DIM = 32
IDX_TOPK_MAX = 256
REL_BUCKETS = 32
REL_MAX_DIST = 128
D_FF = 2816
DN_ALPHA = (2.0 * DEPTH) ** 0.25
LN_EPS = 1e-5

IN_SIZES = (LRU_WIDTH, LRU_WIDTH, GLA_WIDTH, GLA_WIDTH, GLA_WIDTH, GLA_WIDTH, GLA_GATE_RANK,
            DSA_WIDTH, DSA_WIDTH, DSA_WIDTH, IDX_HEADS * IDX_DIM, IDX_DIM, IDX_HEADS)
IN_OFFSETS = [0] + [int(o) for o in np.cumsum(IN_SIZES)]

LANES = 128
SUBLANES = 8
VMEM_LIMIT_BYTES = 56 * 1024 * 1024

LRU_COLS = 2 * LRU_WIDTH
GLA_COLS = 4 * GLA_WIDTH + LANES
PA_LRU = 0
PA_GLA = PA_LRU + LRU_COLS
PA_DQ = PA_GLA + GLA_COLS
PA_DV = PA_DQ + DSA_WIDTH
PA_IQ = PA_DV + DSA_WIDTH
PA_IW = PA_IQ + LANES
PA_END = PA_IW + LANES
PT_ROWS = DSA_WIDTH + IDX_DIM

Q_BLOCK = 256
Q_SUB = Q_BLOCK // LANES
KEY_CHUNK = 512
HALF_BITS = 16
INT_MIN = -2147483648
NEG_INF_KEY = -2139095041
NEG_BIG = -1e30
LOG2E = 1.4426950408889634
REL_SLABS = 3
FAST_SOFTMAX_MAX_SHIFT = 120.0


def _nt_dot(a, b):
    return lax.dot_general(a, b, (((1,), (1,)), ((), ())), preferred_element_type=F32)


def _tn_dot(a, b):
    return lax.dot_general(a, b, (((0,), (0,)), ((), ())), preferred_element_type=F32)


def _split3(x):
    x1 = x.astype(BF16)
    r1 = x - x1.astype(F32)
    x2 = r1.astype(BF16)
    return x1, x2, (r1 - x2.astype(F32)).astype(BF16)


def _softplus(z):
    return jnp.maximum(z, 0.0) + jnp.log(1.0 + jnp.exp(-jnp.abs(z)))


def _sigmoid(z):
    return 1.0 / (1.0 + jnp.exp(-z))


def _proj_kernel(x_ref, wa_ref, wt_ref, lru_ref, gla_ref, q_ref, v_ref, iq_ref, iw_ref, kt_ref, kit_ref):
    xb = x_ref[...].astype(BF16)

    def mm(c0, c1):
        return jnp.dot(xb, wa_ref[:, c0:c1], preferred_element_type=F32)

    lru_ref[...] = mm(PA_LRU, PA_GLA)
    gla_ref[...] = mm(PA_GLA, PA_DQ)
    q_ref[...] = mm(PA_DQ, PA_DV)
    v_ref[...] = mm(PA_DV, PA_IQ).astype(BF16)
    iq_ref[...] = mm(PA_IQ, PA_IW).astype(BF16)
    iw_ref[...] = mm(PA_IW, PA_END)
    t = _nt_dot(wt_ref[...], xb)
    kt_ref[...] = t[:DSA_WIDTH].astype(BF16)
    kit_ref[...] = t[DSA_WIDTH:PT_ROWS].astype(BF16)


def _proj(x, wa, wt, *, tm):
    B, S, D = x.shape
    grid = (B, S // tm)
    row = lambda b, m: (b, m, 0)
    col = lambda b, m: (b, 0, m)
    const = lambda b, m: (0, 0)
    out_shape = (
        jax.ShapeDtypeStruct((B, S, LRU_COLS), F32),
        jax.ShapeDtypeStruct((B, S, GLA_COLS), F32),
        jax.ShapeDtypeStruct((B, S, DSA_WIDTH), F32),
        jax.ShapeDtypeStruct((B, S, DSA_WIDTH), BF16),
        jax.ShapeDtypeStruct((B, S, LANES), BF16),
        jax.ShapeDtypeStruct((B, S, LANES), F32),
        jax.ShapeDtypeStruct((B, DSA_WIDTH, S), BF16),
        jax.ShapeDtypeStruct((B, IDX_DIM, S), BF16),
    )
    out_specs = (
        pl.BlockSpec((None, tm, LRU_COLS), row),
        pl.BlockSpec((None, tm, GLA_COLS), row),
        pl.BlockSpec((None, tm, DSA_WIDTH), row),
        pl.BlockSpec((None, tm, DSA_WIDTH), row),
        pl.BlockSpec((None, tm, LANES), row),
        pl.BlockSpec((None, tm, LANES), row),
        pl.BlockSpec((None, DSA_WIDTH, tm), col),
        pl.BlockSpec((None, IDX_DIM, tm), col),
    )
    return pl.pallas_call(
        _proj_kernel,
        grid=grid,
        in_specs=[pl.BlockSpec((None, tm, D), row),
                  pl.BlockSpec((D, PA_END), const),
                  pl.BlockSpec((PT_ROWS, D), const)],
        out_specs=out_specs,
        out_shape=out_shape,
        compiler_params=pltpu.CompilerParams(
            dimension_semantics=("parallel", "parallel"), vmem_limit_bytes=VMEM_LIMIT_BYTES),
        name="in_proj",
    )(x, wa, wt)


def _shift_rows(x, d, fill, row):
    return jnp.where(row >= d, pltpu.roll(x, d, 0), fill)


def _lru_kernel(p_ref, cw_ref, cb_ref, wa_ref, ba_ref, wx_ref, bx_ref, lam_ref, y_ref, xprev_ref, h_ref):
    tl = p_ref.shape[0]

    @pl.when(pl.program_id(1) == 0)
    def _():
        xprev_ref[...] = jnp.zeros_like(xprev_ref)
        h_ref[...] = jnp.zeros_like(h_ref)

    xb = p_ref[:, 0:LRU_WIDTH]
    gb = p_ref[:, LRU_WIDTH:2 * LRU_WIDTH]
    prev = xprev_ref[...]
    row8 = lax.broadcasted_iota(I32, (SUBLANES, LRU_WIDTH), 0)
    cw = cw_ref[...]
    xc = cb_ref[...] + cw[CONV_WIDTH - 1:CONV_WIDTH] * xb
    for d in range(1, CONV_WIDTH):
        r = pltpu.roll(xb, d, 0)
        top = jnp.where(row8 < d, pltpu.roll(prev, d, 0), r[0:SUBLANES])
        r = jnp.concatenate([top, r[SUBLANES:]], axis=0)
        xc = xc + cw[CONV_WIDTH - 1 - d:CONV_WIDTH - d] * r
    xprev_ref[...] = xb[tl - SUBLANES:tl]

    xcb = xc.astype(BF16)
    r_gate = _sigmoid(jnp.dot(xcb, wa_ref[...], preferred_element_type=F32) + ba_ref[...])
    i_gate = _sigmoid(jnp.dot(xcb, wx_ref[...], preferred_element_type=F32) + bx_ref[...])
    log_a = (-LRU_C) * r_gate * _softplus(-lam_ref[...])
    a = jnp.exp(log_a)
    u = jnp.sqrt(1.0 - a * a) * (i_gate * xc)

    row = lax.broadcasted_iota(I32, (tl, LRU_WIDTH), 0)
    d = 1
    while d < tl:
        a_s = _shift_rows(a, d, 1.0, row)
        u_s = _shift_rows(u, d, 0.0, row)
        u = u + a * u_s
        a = a * a_s
        d *= 2
    h = u + a * h_ref[SUBLANES - 1:SUBLANES, :]
    h_ref[...] = h[tl - SUBLANES:tl]
    y_ref[...] = h * jax.nn.gelu(gb)


def _lru(p_lru, cw, cb, wa, ba, wx, bx, lam, *, tl):
    B, S, _ = p_lru.shape
    const = lambda b, j: (0, 0)
    vec = pl.BlockSpec((1, LRU_WIDTH), const)
    mat = pl.BlockSpec((LRU_WIDTH, LRU_WIDTH), const)
    return pl.pallas_call(
        _lru_kernel,
        grid=(B, S // tl),
        in_specs=[pl.BlockSpec((None, tl, LRU_COLS), lambda b, j: (b, j, 0)),
                  pl.BlockSpec((CONV_WIDTH, LRU_WIDTH), const), vec, mat, vec, mat, vec, vec],
        out_specs=pl.BlockSpec((None, tl, LRU_WIDTH), lambda b, j: (b, j, 0)),
        out_shape=jax.ShapeDtypeStruct((B, S, LRU_WIDTH), F32),
        scratch_shapes=[pltpu.VMEM((SUBLANES, LRU_WIDTH), F32), pltpu.VMEM((SUBLANES, LRU_WIDTH), F32)],
        compiler_params=pltpu.CompilerParams(
            dimension_semantics=("parallel", "arbitrary"), vmem_limit_bytes=VMEM_LIMIT_BYTES),
        name="rg_lru",
    )(p_lru, cw, cb, wa, ba, wx, bx, lam)


def _gla_kernel(q_ref, k_ref, v_ref, g_ref, glr_ref, w2_ref, bg_ref, ng_ref, tri_ref, mean_ref,
                y_ref, st_ref):
    tg = q_ref.shape[0]
    C = GLA_CHUNK
    hi = lax.Precision.HIGHEST

    @pl.when(pl.program_id(1) == 0)
    def _():
        st_ref[...] = jnp.zeros_like(st_ref)

    z = jnp.dot(glr_ref[...], w2_ref[...], preferred_element_type=F32, precision=hi) + bg_ref[...]
    log_alpha = -_softplus(-z) * (1.0 / GLA_GATE_TAU)

    lane = lax.broadcasted_iota(I32, (C, LANES), 1)
    first = lane < GLA_DK
    rr = lax.broadcasted_iota(I32, (C, C), 0)
    cc = lax.broadcasted_iota(I32, (C, C), 1)
    causal = cc <= rr
    causal2 = jnp.concatenate([causal, causal], axis=0)
    r2 = lax.broadcasted_iota(I32, (LANES, LANES), 0)
    c2 = lax.broadcasted_iota(I32, (LANES, LANES), 1)
    same_head = (r2 < GLA_DV) == (c2 < GLA_DK)
    tri = tri_ref[...]

    for c in range(tg // C):
        rows = slice(c * C, (c + 1) * C)
        bcum = sum(jnp.dot(tri, piece, preferred_element_type=F32)
                   for piece in _split3(log_alpha[rows]))
        blast = bcum[C - 1:C]
        kf = k_ref[rows, :]
        q_dec = q_ref[rows, :] * (GLA_DK ** -0.5) * jnp.exp(bcum)
        k_inv = kf * jnp.exp(-bcum)
        k_end = kf * jnp.exp(blast - bcum)
        decay = jnp.exp(blast)
        vf = v_ref[rows, :]
        outs = []
        for p in range(GLA_HEADS // 2):
            cs = slice(p * LANES, (p + 1) * LANES)
            qd, ki, ke, vp = q_dec[:, cs], k_inv[:, cs], k_end[:, cs], vf[:, cs]
            kib = ki.astype(BF16)
            vpb = vp.astype(BF16)
            q_two = jnp.concatenate([jnp.where(first, qd, 0.0), jnp.where(first, 0.0, qd)], axis=0).astype(BF16)
            att = jnp.where(causal2, _nt_dot(q_two, kib), 0.0).astype(BF16)
            o_two = jnp.dot(att, vpb, preferred_element_type=F32)
            o_intra = jnp.where(first, o_two[:C], o_two[C:])
            st = st_ref[p]
            o_inter = _nt_dot(qd.astype(BF16), st.astype(BF16))
            u_t = _tn_dot(vpb, ke.astype(BF16))
            st_ref[p] = st * decay[:, cs] + jnp.where(same_head, u_t, 0.0)
            outs.append(o_intra + o_inter)
        o = jnp.concatenate(outs, axis=1)
        ms = sum(jnp.dot(piece, mean_ref[...], preferred_element_type=F32)
                 for piece in _split3(o * o))
        o = o * lax.rsqrt(ms + 1e-6) * ng_ref[...]
        gf = g_ref[rows, :]
        y_ref[rows, :] = o * (gf * _sigmoid(gf))


def _gla(p_gla, w2, bg, ng, *, tg):
    B, S, _ = p_gla.shape
    W = GLA_WIDTH
    const = lambda b, j: (0, 0)
    tri = jnp.tril(jnp.ones((GLA_CHUNK, GLA_CHUNK), F32)).astype(BF16)
    head = jnp.arange(W) // GLA_DV
    mean_blk = ((head[:, None] == head[None, :]).astype(F32) / GLA_DV).astype(BF16)

    def colblk(c):
        return pl.BlockSpec((None, tg, W), lambda b, j: (b, j, c))

    return pl.pallas_call(
        _gla_kernel,
        grid=(B, S // tg),
        in_specs=[colblk(0), colblk(1), colblk(2), colblk(3),
                  pl.BlockSpec((None, tg, LANES), lambda b, j: (b, j, 4 * W // LANES)),
                  pl.BlockSpec((LANES, W), const),
                  pl.BlockSpec((1, W), const), pl.BlockSpec((1, W), const),
                  pl.BlockSpec((GLA_CHUNK, GLA_CHUNK), const),
                  pl.BlockSpec((W, W), const)],
        out_specs=pl.BlockSpec((None, tg, W), lambda b, j: (b, j, 0)),
        out_shape=jax.ShapeDtypeStruct((B, S, W), F32),
        scratch_shapes=[pltpu.VMEM((GLA_HEADS // 2, LANES, LANES), F32)],
        compiler_params=pltpu.CompilerParams(
            dimension_semantics=("parallel", "arbitrary"), vmem_limit_bytes=VMEM_LIMIT_BYTES),
        name="gla",
    )(p_gla, p_gla, p_gla, p_gla, p_gla, w2, bg, ng, tri, mean_blk)


def _relbias_kernel(rb_ref, tb_ref):
    r = lax.broadcasted_iota(I32, (LANES, LANES), 0)
    c = lax.broadcasted_iota(I32, (LANES, LANES), 1)
    max_exact = REL_BUCKETS // 2
    for delta in range(REL_SLABS):
        n = jnp.maximum(delta * LANES + r - c, 0)
        nf = jnp.maximum(n, 1).astype(F32)
        large = max_exact + (jnp.log(nf / max_exact) / math.log(REL_MAX_DIST / max_exact)
                             * (REL_BUCKETS - max_exact)).astype(I32)
        large = jnp.minimum(large, REL_BUCKETS - 1)
        bucket = jnp.where(n < max_exact, n, large)
        for h in range(DSA_HEADS):
            acc = jnp.zeros((LANES, LANES), F32)
            for b in range(REL_BUCKETS):
                acc = jnp.where(bucket == b, rb_ref[b, h], acc)
            tb_ref[h, delta] = (acc - rb_ref[REL_BUCKETS - 1, h]) * LOG2E


def _relbias_tiles(rel_bias):
    return pl.pallas_call(
        _relbias_kernel,
        in_specs=[pl.BlockSpec(memory_space=pltpu.SMEM)],
        out_specs=pl.BlockSpec(memory_space=pltpu.VMEM),
        out_shape=jax.ShapeDtypeStruct((DSA_HEADS, REL_SLABS, LANES, LANES), F32),
        name="rel_bias_tiles",
    )(rel_bias)


def _dsa_kernel(iq_ref, iw_ref, q_ref, kit_ref, kt_ref, v_ref, tb_ref, uo_ref, y_ref,
                key_ref, kmx_ref, m_ref, acc_ref, *, top_k):
    i = pl.program_id(1)
    TQ, CK, L = Q_BLOCK, KEY_CHUNK, LANES
    S = key_ref.shape[1]
    first_blk = i * Q_SUB
    n_chunks = (first_blk + Q_SUB - 1) // (CK // L) + 1
    lane = lax.broadcasted_iota(I32, (TQ, L), 1)

    @pl.when(i == 0)
    def _():
        nb = 2 * CK
        for h in range(DSA_HEADS):
            def norm_chunk(c, mx):
                kk = kt_ref[h * DSA_HEAD_DIM:(h + 1) * DSA_HEAD_DIM, pl.ds(pl.multiple_of(c * nb, nb), nb)]
                kk = kk.astype(F32)
                return jnp.maximum(mx, jnp.sum(kk * kk, axis=0, keepdims=True))
            mx = lax.fori_loop(0, S // nb, norm_chunk, jnp.zeros((1, nb), F32))
            kmx_ref[h:h + 1, :] = jnp.broadcast_to(jnp.max(mx, axis=1, keepdims=True), (1, L))

    iq = iq_ref[...]
    iq_h = [iq[:, h * IDX_DIM:(h + 1) * IDX_DIM] for h in range(IDX_HEADS)]
    scale = (IDX_DIM ** -0.5) * (IDX_HEADS ** -0.5)
    w_h = [jnp.broadcast_to(iw_ref[:, h:h + 1] * scale, (TQ, CK)) for h in range(IDX_HEADS)]

    def score_chunk(c, masked):
        off = pl.multiple_of(c * CK, CK)
        kc = kit_ref[:, pl.ds(off, CK)]
        sc = jnp.zeros((TQ, CK), F32)
        for h in range(IDX_HEADS):
            z = jnp.dot(iq_h[h], kc, preferred_element_type=F32)
            sc = sc + jnp.maximum(z, 0.0) * w_h[h]
        sc = sc + 0.0
        if masked:
            row_t = i * TQ + lax.broadcasted_iota(I32, (TQ, CK), 0)
            pos = off + lax.broadcasted_iota(I32, (TQ, CK), 1)
            sc = jnp.where(pos <= row_t, sc, -jnp.inf)
        bits = pltpu.bitcast(sc, I32)
        key_ref[:, pl.ds(off, CK)] = bits ^ ((bits >> 31) & I32(0x7FFFFFFF))

    def score_body(c, carry):
        score_chunk(c, False)
        return carry

    lax.fori_loop(0, n_chunks - 1, score_body, 0)
    score_chunk(n_chunks - 1, True)

    @pl.when((n_chunks & 1) == 1)
    def _():
        key_ref[:, pl.ds(pl.multiple_of(n_chunks * CK, CK), CK)] = jnp.full((TQ, CK), NEG_INF_KEY, I32)

    def count_ge(cand):
        cand_all = jnp.broadcast_to(cand, (TQ, L))
        accs = []
        for r in range(Q_SUB):
            rows = slice(r * L, (r + 1) * L)
            cand_b = cand_all[rows]

            def body(c, acc, rows=rows, cand_b=cand_b):
                kk = key_ref[rows, pl.ds(pl.multiple_of(c * (2 * CK), 2 * CK), 2 * CK)]
                for s in range(2 * CK // L):
                    acc = acc + (kk[:, s * L:(s + 1) * L] >= cand_b).astype(I32)
                return acc
            accs.append(lax.fori_loop(0, (n_chunks + 1) >> 1, body, jnp.zeros((L, L), I32)))
        acc = accs[0] if Q_SUB == 1 else jnp.concatenate(accs, axis=0)
        return jnp.sum(acc, axis=1, keepdims=True)

    def bit_step(thr, cnt_thr, bit):
        cand = thr + bit
        cnt = count_ge(cand)
        ok = cnt >= top_k
        return jnp.where(ok, cand, thr), jnp.where(ok, cnt, cnt_thr)

    def high_step(b, st):
        return bit_step(st[0], st[1], lax.shift_left(I32(1), I32(31) - b))

    thr, cnt_thr = lax.fori_loop(
        0, 32 - HALF_BITS, high_step,
        (jnp.full((TQ, 1), INT_MIN, I32), jnp.full((TQ, 1), S, I32)))

    cnt_next = count_ge(thr + 1)
    exact = cnt_next < top_k

    def low_cond(st):
        b, _, _, active = st
        return (b < HALF_BITS) & (active > 0)

    def low_step(st):
        b, thr, cnt_thr, _ = st
        new_thr, new_cnt = bit_step(thr, cnt_thr, lax.shift_left(I32(1), I32(HALF_BITS - 1) - b))
        thr = jnp.where(exact, thr, new_thr)
        cnt_thr = jnp.where(exact, cnt_thr, new_cnt)
        active = jnp.max(jnp.where(exact | (cnt_thr == top_k), 0, 1))
        return b + 1, thr, cnt_thr, active

    active0 = jnp.max(jnp.where(exact | (cnt_thr == top_k), 0, 1))
    _, thr, cnt_thr, _ = lax.while_loop(low_cond, low_step, (I32(0), thr, cnt_thr, active0))

    t_sel = jnp.where(cnt_thr == top_k, thr - 1, thr)
    need = (top_k - count_ge(t_sel + 1)).astype(F32)
    t_sel_b = jnp.broadcast_to(t_sel, (TQ, L))
    need_b = jnp.broadcast_to(need, (TQ, L))
    uo = uo_ref[...]

    def mask_chunk(c, run):
        off = pl.multiple_of(c * CK, CK)
        kc = key_ref[:, pl.ds(off, CK)]
        out = []
        for s in range(CK // L):
            kk = kc[:, s * L:(s + 1) * L]
            eq = kk == t_sel_b
            pr = jnp.dot(jnp.where(eq, 1.0, 0.0).astype(BF16), uo, preferred_element_type=F32)
            sel = (kk > t_sel_b) | (eq & (run + pr[:, :L] <= need_b))
            sel = sel & (kk > NEG_INF_KEY)
            out.append(jnp.where(sel, 0.0, -jnp.inf).astype(F32))
            run = run + pr[:, L:]
        key_ref[:, pl.ds(off, CK)] = pltpu.bitcast(jnp.concatenate(out, axis=1), I32)
        return run

    lax.fori_loop(0, n_chunks, mask_chunk, jnp.zeros((TQ, L), F32))

    q2 = q_ref[...] * LOG2E
    q_ext = []
    shift_max = jnp.zeros((TQ, L), F32)
    for h in range(DSA_HEADS):
        blk = q2[:, (h // 2) * L:(h // 2 + 1) * L]
        if h % 2:
            blk = pltpu.roll(blk, DSA_HEAD_DIM, 1)
        qf = jnp.where(lane < DSA_HEAD_DIM, blk, 0.0).astype(BF16).astype(F32)
        bound = jnp.sqrt(jnp.sum(qf * qf, axis=1, keepdims=True) * kmx_ref[h:h + 1, :]) * (1.0 + 2.0 ** -7)
        shift_max = jnp.maximum(shift_max, bound)
        q_ext.append(jnp.where(lane == DSA_HEAD_DIM, bound, qf).astype(BF16))
    minus_one_row = jnp.where(lax.broadcasted_iota(I32, (DSA_HEAD_DIM, CK), 0) == 0, -1.0, 0.0).astype(BF16)
    ones_cols = jnp.ones((CK, L), BF16)

    acc_ref[...] = jnp.zeros(acc_ref.shape, F32)

    def attend(c, biased, fast):
        off = pl.multiple_of(c * CK, CK)
        width = CK
        mb = pltpu.bitcast(key_ref[:, pl.ds(off, width)], F32)
        for h in range(DSA_HEADS):
            k_ext = jnp.concatenate(
                [kt_ref[h * DSA_HEAD_DIM:(h + 1) * DSA_HEAD_DIM, pl.ds(off, width)], minus_one_row], axis=0)
            s = jnp.dot(q_ext[h], k_ext, preferred_element_type=F32) + mb
            if biased:
                def tile(r, j):
                    return tb_ref[h, jnp.clip(first_blk + r - (c * (CK // L) + j), 0, REL_SLABS - 1)]
                s = s + jnp.concatenate(
                    [jnp.concatenate([tile(r, j) for r in range(Q_SUB)], axis=0) for j in range(CK // L)], axis=1)
            pair = (h // 2) * L
            v_ext = jnp.concatenate([v_ref[pl.ds(off, width), pair:pair + L], ones_cols], axis=1)
            if fast:
                p = jnp.exp2(s).astype(BF16)
                acc_ref[h] += jnp.dot(p, v_ext, preferred_element_type=F32)
            else:
                m_old = m_ref[h]
                m_new = jnp.maximum(m_old, jnp.max(s, axis=1, keepdims=True))
                alpha = jnp.exp2(m_old - m_new)
                p = jnp.concatenate(
                    [jnp.exp2(s[:, j * L:(j + 1) * L] - m_new) for j in range(width // L)], axis=1)
                pv = jnp.dot(p.astype(BF16), v_ext, preferred_element_type=F32)
                acc_ref[h] = acc_ref[h] * jnp.concatenate([alpha, alpha], axis=1) + pv
                m_ref[h] = m_new

    def attend_all(fast):
        first_biased = jnp.maximum(first_blk - 1, 0) // (CK // L)

        def far_chunk(c, carry):
            attend(c, False, fast)
            return carry

        def near_chunk(c, carry):
            attend(c, True, fast)
            return carry

        def far_pair(c, carry):
            attend(2 * c, False, fast)
            attend(2 * c + 1, False, fast)
            return carry

        single_from = 0
        if fast:
            single_from = (first_biased >> 1) * 2
            lax.fori_loop(0, first_biased >> 1, far_pair, 0)
        lax.fori_loop(single_from, first_biased, far_chunk, 0)
        lax.fori_loop(first_biased, n_chunks, near_chunk, 0)

    use_fast = 2.0 * jnp.max(shift_max) + jnp.max(jnp.abs(tb_ref[...])) <= FAST_SOFTMAX_MAX_SHIFT

    @pl.when(use_fast)
    def _():
        attend_all(True)

    @pl.when(jnp.logical_not(use_fast))
    def _():
        m_ref[...] = jnp.full(m_ref.shape, NEG_BIG, F32)
        attend_all(False)

    for p in range(DSA_HEADS // 2):
        a0, a1 = acc_ref[2 * p], acc_ref[2 * p + 1]
        y_ref[:, p * L:(p + 1) * L] = jnp.where(lane < DSA_HEAD_DIM, a0[:, :L] / a0[:, L:], a1[:, :L] / a1[:, L:])


def _dsa(iq, iw, q, kit, kt, v, tb, *, top_k):
    B, S, W = q.shape
    upper = jnp.triu(jnp.ones((LANES, LANES), F32))
    uo = jnp.concatenate([upper, jnp.ones((LANES, LANES), F32)], axis=1).astype(BF16)
    qrow = lambda b, i: (b, i, 0)
    whole = lambda b, i: (b, 0, 0)
    one = pl.Buffered(1)
    return pl.pallas_call(
        functools.partial(_dsa_kernel, top_k=top_k),
        grid=(B, S // Q_BLOCK),
        in_specs=[pl.BlockSpec((None, Q_BLOCK, LANES), qrow),
                  pl.BlockSpec((None, Q_BLOCK, LANES), qrow),
                  pl.BlockSpec((None, Q_BLOCK, W), qrow),
                  pl.BlockSpec((None, IDX_DIM, S), whole, pipeline_mode=one),
                  pl.BlockSpec((None, W, S), whole, pipeline_mode=one),
                  pl.BlockSpec((None, S, W), whole, pipeline_mode=one),
                  pl.BlockSpec((DSA_HEADS, REL_SLABS, LANES, LANES), lambda b, i: (0, 0, 0, 0)),
                  pl.BlockSpec((LANES, 2 * LANES), lambda b, i: (0, 0))],
        out_specs=pl.BlockSpec((None, Q_BLOCK, W), qrow),
        out_shape=jax.ShapeDtypeStruct((B, S, W), F32),
        scratch_shapes=[pltpu.VMEM((Q_BLOCK, S), I32),
                        pltpu.VMEM((SUBLANES, LANES), F32),
                        pltpu.VMEM((DSA_HEADS, Q_BLOCK, LANES), F32),
                        pltpu.VMEM((DSA_HEADS, Q_BLOCK, 2 * LANES), F32)],
        compiler_params=pltpu.CompilerParams(
            dimension_semantics=("parallel", "arbitrary"), vmem_limit_bytes=VMEM_LIMIT_BYTES),
        name="dsa",
    )(iq, iw, q, kit, kt, v, tb, uo)


def _layer_norm(z, g, b):
    mu = jnp.mean(z, axis=-1, keepdims=True)
    zc = z - mu
    var = jnp.mean(zc * zc, axis=-1, keepdims=True)
    return zc * lax.rsqrt(var + LN_EPS) * g + b


def _mix_kernel(x_ref, ya_ref, yb_ref, yc_ref, wa_ref, wb_ref, wc_ref, g_ref, b_ref, o_ref):
    mix = jnp.dot(ya_ref[...].astype(BF16), wa_ref[...], preferred_element_type=F32)
    mix = mix + jnp.dot(yb_ref[...].astype(BF16), wb_ref[...], preferred_element_type=F32)
    mix = mix + jnp.dot(yc_ref[...].astype(BF16), wc_ref[...], preferred_element_type=F32)
    o_ref[...] = _layer_norm(DN_ALPHA * x_ref[...] + mix, g_ref[...], b_ref[...])


def _mix(x, ya, yb, yc, wa, wb, wc, g, b, *, tm):
    T, D = x.shape
    row = lambda m: (m, 0)
    const = lambda m: (0, 0)
    return pl.pallas_call(
        _mix_kernel,
        grid=(T // tm,),
        in_specs=[pl.BlockSpec((tm, D), row),
                  pl.BlockSpec((tm, ya.shape[1]), row), pl.BlockSpec((tm, yb.shape[1]), row),
                  pl.BlockSpec((tm, yc.shape[1]), row),
                  pl.BlockSpec(wa.shape, const), pl.BlockSpec(wb.shape, const), pl.BlockSpec(wc.shape, const),
                  pl.BlockSpec((1, D), const), pl.BlockSpec((1, D), const)],
        out_specs=pl.BlockSpec((tm, D), row),
        out_shape=jax.ShapeDtypeStruct((T, D), F32),
        compiler_params=pltpu.CompilerParams(
            dimension_semantics=("parallel",), vmem_limit_bytes=VMEM_LIMIT_BYTES),
        name="out_proj_ln",
    )(x, ya, yb, yc, wa, wb, wc, g, b)


def _ffn_kernel(x_ref, wg_ref, wu_ref, wd_ref, g_ref, b_ref, o_ref, xb_ref, acc_ref):
    f = pl.program_id(1)

    @pl.when(f == 0)
    def _():
        xb_ref[...] = x_ref[...].astype(BF16)
        acc_ref[...] = jnp.zeros_like(acc_ref)

    xb = xb_ref[...]
    gate = jnp.dot(xb, wg_ref[...], preferred_element_type=F32)
    up = jnp.dot(xb, wu_ref[...], preferred_element_type=F32)
    h = (gate * _sigmoid(gate)) * up
    acc_ref[...] += jnp.dot(h.astype(BF16), wd_ref[...], preferred_element_type=F32)

    @pl.when(f == pl.num_programs(1) - 1)
    def _():
        o_ref[...] = _layer_norm(DN_ALPHA * x_ref[...] + acc_ref[...], g_ref[...], b_ref[...])


def _ffn(x, wg, wu, wd, g, b, *, tm, tf):
    T, D = x.shape
    FF = wg.shape[1]
    row = lambda m, f: (m, 0)
    const = lambda m, f: (0, 0)
    return pl.pallas_call(
        _ffn_kernel,
        grid=(T // tm, FF // tf),
        in_specs=[pl.BlockSpec((tm, D), row),
                  pl.BlockSpec((D, tf), lambda m, f: (0, f)),
                  pl.BlockSpec((D, tf), lambda m, f: (0, f)),
                  pl.BlockSpec((tf, D), lambda m, f: (f, 0)),
                  pl.BlockSpec((1, D), const), pl.BlockSpec((1, D), const)],
        out_specs=pl.BlockSpec((tm, D), row),
        out_shape=jax.ShapeDtypeStruct((T, D), F32),
        scratch_shapes=[pltpu.VMEM((tm, D), BF16), pltpu.VMEM((tm, D), F32)],
        compiler_params=pltpu.CompilerParams(
            dimension_semantics=("parallel", "arbitrary"), vmem_limit_bytes=VMEM_LIMIT_BYTES),
        name="ffn_ln",
    )(x, wg, wu, wd, g, b)


def _split_w_in(w):
    parts = [w[:, IN_OFFSETS[j]:IN_OFFSETS[j + 1]] for j in range(len(IN_SIZES))]
    lru_x, lru_g, gq, gk, gv, gg, g_lr, dq, dk, dv, iq, ik, iw = parts
    D = w.shape[0]
    pad = lambda a, n: jnp.pad(a, ((0, 0), (0, n - a.shape[1])))
    wa = jnp.concatenate([lru_x, lru_g, gq, gk, gv, gg, pad(g_lr, LANES),
                          dq * (DSA_HEAD_DIM ** -0.5), dv, iq, pad(iw, LANES)], axis=1)
    wt = jnp.concatenate([dk, ik], axis=1).T
    assert wa.shape == (D, PA_END) and wt.shape == (PT_ROWS, D)
    return wa.astype(BF16), wt.astype(BF16)


def _block_diag(w):
    n, k, _ = w.shape
    eye = jnp.eye(n, dtype=w.dtype)
    return (eye[:, None, :, None] * w[:, :, None, :]).reshape(n * k, n * k)


def kernel(x, w_in, conv_w, conv_b, lru_wa, lru_ba, lru_wx, lru_bx, lru_lambda, gla_w_gate2, gla_b_gate,
           gla_norm_g, rel_bias, w_out, ln1_g, ln1_b, w_ffn_gate, w_ffn_up, w_ffn_down, ln2_g, ln2_b):
    B, S, D = x.shape
    T = B * S
    top_k = min(IDX_TOPK_MAX, S // 4)
    assert S % (2 * KEY_CHUNK) == 0 and D == D_MODEL
    tm = min(512, S)
    tb = _relbias_tiles(rel_bias)
    row = lambda a: a.reshape(1, -1)
    for l in range(w_in.shape[0]):
        wa, wt = _split_w_in(w_in[l])
        p_lru, p_gla, dq, dv, iq, iw, kt, kit = _proj(x, wa, wt, tm=tm)
        y_lru = _lru(p_lru, conv_w[l], row(conv_b[l]),
                     _block_diag(lru_wa[l]).astype(BF16), row(lru_ba[l]),
                     _block_diag(lru_wx[l]).astype(BF16), row(lru_bx[l]), row(lru_lambda[l]), tl=tm)
        w2 = jnp.pad(gla_w_gate2[l], ((0, LANES - GLA_GATE_RANK), (0, 0)))
        y_gla = _gla(p_gla, w2, row(gla_b_gate[l]), row(gla_norm_g[l]), tg=min(256, S))
        y_dsa = _dsa(iq, iw, dq, kit, kt, dv, tb, top_k=top_k)
        wo = w_out[l].astype(BF16)
        x1 = _mix(x.reshape(T, D), y_lru.reshape(T, -1), y_gla.reshape(T, -1), y_dsa.reshape(T, -1),
                  wo[:LRU_WIDTH], wo[LRU_WIDTH:LRU_WIDTH + GLA_WIDTH], wo[LRU_WIDTH + GLA_WIDTH:],
                  row(ln1_g[l]), row(ln1_b[l]), tm=tm)
        x2 = _ffn(x1, w_ffn_gate[l].astype(BF16), w_ffn_up[l].astype(BF16), w_ffn_down[l].astype(BF16),
                  row(ln2_g[l]), row(ln2_b[l]), tm=tm, tf=D_FF // 2)
        x = x2.reshape(B, S, D)
    return x
```

```python
import functools
import math

import jax
import jax.numpy as jnp
import numpy as np
from jax import lax
from jax.experimental import pallas as pl
from jax.experimental.pallas import tpu as pltpu

F32 = jnp.float32
BF16 = jnp.bfloat16
I32 = jnp.int32

D_MODEL = 1024
DEPTH = 2
LRU_WIDTH = 256
LRU_BLOCKS = 4
LRU_BLOCK = LRU_WIDTH // LRU_BLOCKS
CONV_WIDTH = 4
LRU_C = 8.0
GLA_HEADS = 6
GLA_DK = 64
GLA_DV = 64
GLA_GATE_RANK = 16
GLA_GATE_TAU = 16.0
GLA_CHUNK = 64
GLA_WIDTH = GLA_HEADS * GLA_DV
DSA_HEADS = 6
DSA_HEAD_DIM = 64
DSA_WIDTH = DSA_HEADS * DSA_HEAD_DIM
IDX_HEADS = 4
IDX_DIM = 32
IDX_TOPK_MAX = 256
REL_BUCKETS = 32
REL_MAX_DIST = 128
D_FF = 2816
DN_ALPHA = (2.0 * DEPTH) ** 0.25
LN_EPS = 1e-5

IN_SIZES = (LRU_WIDTH, LRU_WIDTH, GLA_WIDTH, GLA_WIDTH, GLA_WIDTH, GLA_WIDTH, GLA_GATE_RANK,
            DSA_WIDTH, DSA_WIDTH, DSA_WIDTH, IDX_HEADS * IDX_DIM, IDX_DIM, IDX_HEADS)
IN_OFFSETS = [0] + [int(o) for o in np.cumsum(IN_SIZES)]

LANES = 128
SUBLANES = 8
VMEM_LIMIT_BYTES = 56 * 1024 * 1024

LRU_COLS = 2 * LRU_WIDTH
GLA_COLS = 4 * GLA_WIDTH + LANES
PA_LRU = 0
PA_GLA = PA_LRU + LRU_COLS
PA_DQ = PA_GLA + GLA_COLS
PA_DV = PA_DQ + DSA_WIDTH
PA_IQ = PA_DV + DSA_WIDTH
PA_IW = PA_IQ + LANES
PA_END = PA_IW + LANES
PT_ROWS = DSA_WIDTH + IDX_DIM

Q_BLOCK = 256
Q_SUB = Q_BLOCK // LANES
KEY_CHUNK = 512
HALF_BITS = 16
INT_MIN = -2147483648
NEG_INF_KEY = -2139095041
NEG_BIG = -1e30
LOG2E = 1.4426950408889634
REL_SLABS = 3
FAST_SOFTMAX_MAX_SHIFT = 120.0


def _nt_dot(a, b):
    return lax.dot_general(a, b, (((1,), (1,)), ((), ())), preferred_element_type=F32)


def _tn_dot(a, b):
    return lax.dot_general(a, b, (((0,), (0,)), ((), ())), preferred_element_type=F32)


def _split3(x):
    x1 = x.astype(BF16)
    r1 = x - x1.astype(F32)
    x2 = r1.astype(BF16)
    return x1, x2, (r1 - x2.astype(F32)).astype(BF16)


def _softplus(z):
    return jnp.maximum(z, 0.0) + jnp.log(1.0 + jnp.exp(-jnp.abs(z)))


def _sigmoid(z):
    return 1.0 / (1.0 + jnp.exp(-z))


def _proj_kernel(x_ref, wa_ref, wt_ref, lru_ref, gla_ref, q_ref, v_ref, iq_ref, iw_ref, kt_ref, kit_ref):
    xb = x_ref[...].astype(BF16)

    def mm(c0, c1):
        return jnp.dot(xb, wa_ref[:, c0:c1], preferred_element_type=F32)

    lru_ref[...] = mm(PA_LRU, PA_GLA)
    gla_ref[...] = mm(PA_GLA, PA_DQ)
    q_ref[...] = mm(PA_DQ, PA_DV)
    v_ref[...] = mm(PA_DV, PA_IQ).astype(BF16)
    iq_ref[...] = mm(PA_IQ, PA_IW).astype(BF16)
    iw_ref[...] = mm(PA_IW, PA_END)
    t = _nt_dot(wt_ref[...], xb)
    kt_ref[...] = t[:DSA_WIDTH].astype(BF16)
    kit_ref[...] = t[DSA_WIDTH:PT_ROWS].astype(BF16)


def _proj(x, wa, wt, *, tm):
    B, S, D = x.shape
    grid = (B, S // tm)
    row = lambda b, m: (b, m, 0)
    col = lambda b, m: (b, 0, m)
    const = lambda b, m: (0, 0)
    out_shape = (
        jax.ShapeDtypeStruct((B, S, LRU_COLS), F32),
        jax.ShapeDtypeStruct((B, S, GLA_COLS), F32),
        jax.ShapeDtypeStruct((B, S, DSA_WIDTH), F32),
        jax.ShapeDtypeStruct((B, S, DSA_WIDTH), BF16),
        jax.ShapeDtypeStruct((B, S, LANES), BF16),
        jax.ShapeDtypeStruct((B, S, LANES), F32),
        jax.ShapeDtypeStruct((B, DSA_WIDTH, S), BF16),
        jax.ShapeDtypeStruct((B, IDX_DIM, S), BF16),
    )
    out_specs = (
        pl.BlockSpec((None, tm, LRU_COLS), row),
        pl.BlockSpec((None, tm, GLA_COLS), row),
        pl.BlockSpec((None, tm, DSA_WIDTH), row),
        pl.BlockSpec((None, tm, DSA_WIDTH), row),
        pl.BlockSpec((None, tm, LANES), row),
        pl.BlockSpec((None, tm, LANES), row),
        pl.BlockSpec((None, DSA_WIDTH, tm), col),
        pl.BlockSpec((None, IDX_DIM, tm), col),
    )
    return pl.pallas_call(
        _proj_kernel,
        grid=grid,
        in_specs=[pl.BlockSpec((None, tm, D), row),
                  pl.BlockSpec((D, PA_END), const),
                  pl.BlockSpec((PT_ROWS, D), const)],
        out_specs=out_specs,
        out_shape=out_shape,
        compiler_params=pltpu.CompilerParams(
            dimension_semantics=("parallel", "parallel"), vmem_limit_bytes=VMEM_LIMIT_BYTES),
        name="in_proj",
    )(x, wa, wt)


def _shift_rows(x, d, fill, row):
    return jnp.where(row >= d, pltpu.roll(x, d, 0), fill)


def _lru_kernel(p_ref, cw_ref, cb_ref, wa_ref, ba_ref, wx_ref, bx_ref, lam_ref, y_ref, xprev_ref, h_ref):
    tl = p_ref.shape[0]

    @pl.when(pl.program_id(1) == 0)
    def _():
        xprev_ref[...] = jnp.zeros_like(xprev_ref)
        h_ref[...] = jnp.zeros_like(h_ref)

    xb = p_ref[:, 0:LRU_WIDTH]
    gb = p_ref[:, LRU_WIDTH:2 * LRU_WIDTH]
    prev = xprev_ref[...]
    row8 = lax.broadcasted_iota(I32, (SUBLANES, LRU_WIDTH), 0)
    cw = cw_ref[...]
    xc = cb_ref[...] + cw[CONV_WIDTH - 1:CONV_WIDTH] * xb
    for d in range(1, CONV_WIDTH):
        r = pltpu.roll(xb, d, 0)
        top = jnp.where(row8 < d, pltpu.roll(prev, d, 0), r[0:SUBLANES])
        r = jnp.concatenate([top, r[SUBLANES:]], axis=0)
        xc = xc + cw[CONV_WIDTH - 1 - d:CONV_WIDTH - d] * r
    xprev_ref[...] = xb[tl - SUBLANES:tl]

    xcb = xc.astype(BF16)
    r_gate = _sigmoid(jnp.dot(xcb, wa_ref[...], preferred_element_type=F32) + ba_ref[...])
    i_gate = _sigmoid(jnp.dot(xcb, wx_ref[...], preferred_element_type=F32) + bx_ref[...])
    log_a = (-LRU_C) * r_gate * _softplus(-lam_ref[...])
    a = jnp.exp(log_a)
    u = jnp.sqrt(1.0 - a * a) * (i_gate * xc)

    row = lax.broadcasted_iota(I32, (tl, LRU_WIDTH), 0)
    d = 1
    while d < tl:
        a_s = _shift_rows(a, d, 1.0, row)
        u_s = _shift_rows(u, d, 0.0, row)
        u = u + a * u_s
        a = a * a_s
        d *= 2
    h = u + a * h_ref[SUBLANES - 1:SUBLANES, :]
    h_ref[...] = h[tl - SUBLANES:tl]
    y_ref[...] = h * jax.nn.gelu(gb)


def _lru(p_lru, cw, cb, wa, ba, wx, bx, lam, *, tl):
    B, S, _ = p_lru.shape
    const = lambda b, j: (0, 0)
    vec = pl.BlockSpec((1, LRU_WIDTH), const)
    mat = pl.BlockSpec((LRU_WIDTH, LRU_WIDTH), const)
    return pl.pallas_call(
        _lru_kernel,
        grid=(B, S // tl),
        in_specs=[pl.BlockSpec((None, tl, LRU_COLS), lambda b, j: (b, j, 0)),
                  pl.BlockSpec((CONV_WIDTH, LRU_WIDTH), const), vec, mat, vec, mat, vec, vec],
        out_specs=pl.BlockSpec((None, tl, LRU_WIDTH), lambda b, j: (b, j, 0)),
        out_shape=jax.ShapeDtypeStruct((B, S, LRU_WIDTH), F32),
        scratch_shapes=[pltpu.VMEM((SUBLANES, LRU_WIDTH), F32), pltpu.VMEM((SUBLANES, LRU_WIDTH), F32)],
        compiler_params=pltpu.CompilerParams(
            dimension_semantics=("parallel", "arbitrary"), vmem_limit_bytes=VMEM_LIMIT_BYTES),
        name="rg_lru",
    )(p_lru, cw, cb, wa, ba, wx, bx, lam)


def _gla_kernel(q_ref, k_ref, v_ref, g_ref, glr_ref, w2_ref, bg_ref, ng_ref, tri_ref, mean_ref,
                y_ref, st_ref):
    tg = q_ref.shape[0]
    C = GLA_CHUNK
    hi = lax.Precision.HIGHEST

    @pl.when(pl.program_id(1) == 0)
    def _():
        st_ref[...] = jnp.zeros_like(st_ref)

    z = jnp.dot(glr_ref[...], w2_ref[...], preferred_element_type=F32, precision=hi) + bg_ref[...]
    log_alpha = -_softplus(-z) * (1.0 / GLA_GATE_TAU)

    lane = lax.broadcasted_iota(I32, (C, LANES), 1)
    first = lane < GLA_DK
    rr = lax.broadcasted_iota(I32, (C, C), 0)
    cc = lax.broadcasted_iota(I32, (C, C), 1)
    causal = cc <= rr
    causal2 = jnp.concatenate([causal, causal], axis=0)
    r2 = lax.broadcasted_iota(I32, (LANES, LANES), 0)
    c2 = lax.broadcasted_iota(I32, (LANES, LANES), 1)
    same_head = (r2 < GLA_DV) == (c2 < GLA_DK)
    tri = tri_ref[...]

    for c in range(tg // C):
        rows = slice(c * C, (c + 1) * C)
        bcum = sum(jnp.dot(tri, piece, preferred_element_type=F32)
                   for piece in _split3(log_alpha[rows]))
        blast = bcum[C - 1:C]
        kf = k_ref[rows, :]
        q_dec = q_ref[rows, :] * (GLA_DK ** -0.5) * jnp.exp(bcum)
        k_inv = kf * jnp.exp(-bcum)
        k_end = kf * jnp.exp(blast - bcum)
        decay = jnp.exp(blast)
        vf = v_ref[rows, :]
        outs = []
        for p in range(GLA_HEADS // 2):
            cs = slice(p * LANES, (p + 1) * LANES)
            qd, ki, ke, vp = q_dec[:, cs], k_inv[:, cs], k_end[:, cs], vf[:, cs]
            kib = ki.astype(BF16)
            vpb = vp.astype(BF16)
            q_two = jnp.concatenate([jnp.where(first, qd, 0.0), jnp.where(first, 0.0, qd)], axis=0).astype(BF16)
            att = jnp.where(causal2, _nt_dot(q_two, kib), 0.0).astype(BF16)
            o_two = jnp.dot(att, vpb, preferred_element_type=F32)
            o_intra = jnp.where(first, o_two[:C], o_two[C:])
            st = st_ref[p]
            o_inter = _nt_dot(qd.astype(BF16), st.astype(BF16))
            u_t = _tn_dot(vpb, ke.astype(BF16))
            st_ref[p] = st * decay[:, cs] + jnp.where(same_head, u_t, 0.0)
            outs.append(o_intra + o_inter)
        o = jnp.concatenate(outs, axis=1)
        ms = sum(jnp.dot(piece, mean_ref[...], preferred_element_type=F32)
                 for piece in _split3(o * o))
        o = o * lax.rsqrt(ms + 1e-6) * ng_ref[...]
        gf = g_ref[rows, :]
        y_ref[rows, :] = o * (gf * _sigmoid(gf))


def _gla(p_gla, w2, bg, ng, *, tg):
    B, S, _ = p_gla.shape
    W = GLA_WIDTH
    const = lambda b, j: (0, 0)
    tri = jnp.tril(jnp.ones((GLA_CHUNK, GLA_CHUNK), F32)).astype(BF16)
    head = jnp.arange(W) // GLA_DV
    mean_blk = ((head[:, None] == head[None, :]).astype(F32) / GLA_DV).astype(BF16)

    def colblk(c):
        return pl.BlockSpec((None, tg, W), lambda b, j: (b, j, c))

    return pl.pallas_call(
        _gla_kernel,
        grid=(B, S // tg),
        in_specs=[colblk(0), colblk(1), colblk(2), colblk(3),
                  pl.BlockSpec((None, tg, LANES), lambda b, j: (b, j, 4 * W // LANES)),
                  pl.BlockSpec((LANES, W), const),
                  pl.BlockSpec((1, W), const), pl.BlockSpec((1, W), const),
                  pl.BlockSpec((GLA_CHUNK, GLA_CHUNK), const),
                  pl.BlockSpec((W, W), const)],
        out_specs=pl.BlockSpec((None, tg, W), lambda b, j: (b, j, 0)),
        out_shape=jax.ShapeDtypeStruct((B, S, W), F32),
        scratch_shapes=[pltpu.VMEM((GLA_HEADS // 2, LANES, LANES), F32)],
        compiler_params=pltpu.CompilerParams(
            dimension_semantics=("parallel", "arbitrary"), vmem_limit_bytes=VMEM_LIMIT_BYTES),
        name="gla",
    )(p_gla, p_gla, p_gla, p_gla, p_gla, w2, bg, ng, tri, mean_blk)


def _relbias_kernel(rb_ref, tb_ref):
    r = lax.broadcasted_iota(I32, (LANES, LANES), 0)
    c = lax.broadcasted_iota(I32, (LANES, LANES), 1)
    max_exact = REL_BUCKETS // 2
    for delta in range(REL_SLABS):
        n = jnp.maximum(delta * LANES + r - c, 0)
        nf = jnp.maximum(n, 1).astype(F32)
        large = max_exact + (jnp.log(nf / max_exact) / math.log(REL_MAX_DIST / max_exact)
                             * (REL_BUCKETS - max_exact)).astype(I32)
        large = jnp.minimum(large, REL_BUCKETS - 1)
        bucket = jnp.where(n < max_exact, n, large)
        for h in range(DSA_HEADS):
            acc = jnp.zeros((LANES, LANES), F32)
            for b in range(REL_BUCKETS):
                acc = jnp.where(bucket == b, rb_ref[b, h], acc)
            tb_ref[h, delta] = (acc - rb_ref[REL_BUCKETS - 1, h]) * LOG2E


def _relbias_tiles(rel_bias):
    return pl.pallas_call(
        _relbias_kernel,
        in_specs=[pl.BlockSpec(memory_space=pltpu.SMEM)],
        out_specs=pl.BlockSpec(memory_space=pltpu.VMEM),
        out_shape=jax.ShapeDtypeStruct((DSA_HEADS, REL_SLABS, LANES, LANES), F32),
        name="rel_bias_tiles",
    )(rel_bias)


def _dsa_kernel(iq_ref, iw_ref, q_ref, kit_ref, kt_ref, v_ref, tb_ref, uo_ref, y_ref,
                key_ref, kmx_ref, m_ref, acc_ref, *, top_k):
    i = pl.program_id(1)
    TQ, CK, L = Q_BLOCK, KEY_CHUNK, LANES
    S = key_ref.shape[1]
    first_blk = i * Q_SUB
    n_chunks = (first_blk + Q_SUB - 1) // (CK // L) + 1
    lane = lax.broadcasted_iota(I32, (TQ, L), 1)

    @pl.when(i == 0)
    def _():
        nb = 2 * CK
        for h in range(DSA_HEADS):
            def norm_chunk(c, mx):
                kk = kt_ref[h * DSA_HEAD_DIM:(h + 1) * DSA_HEAD_DIM, pl.ds(pl.multiple_of(c * nb, nb), nb)]
                kk = kk.astype(F32)
                return jnp.maximum(mx, jnp.sum(kk * kk, axis=0, keepdims=True))
            mx = lax.fori_loop(0, S // nb, norm_chunk, jnp.zeros((1, nb), F32))
            kmx_ref[h:h + 1, :] = jnp.broadcast_to(jnp.max(mx, axis=1, keepdims=True), (1, L))

    iq = iq_ref[...]
    iq_h = [iq[:, h * IDX_DIM:(h + 1) * IDX_DIM] for h in range(IDX_HEADS)]
    scale = (IDX_DIM ** -0.5) * (IDX_HEADS ** -0.5)
    w_h = [jnp.broadcast_to(iw_ref[:, h:h + 1] * scale, (TQ, CK)) for h in range(IDX_HEADS)]

    def score_chunk(c, masked):
        off = pl.multiple_of(c * CK, CK)
        kc = kit_ref[:, pl.ds(off, CK)]
        sc = jnp.zeros((TQ, CK), F32)
        for h in range(IDX_HEADS):
            z = jnp.dot(iq_h[h], kc, preferred_element_type=F32)
            sc = sc + jnp.maximum(z, 0.0) * w_h[h]
        sc = sc + 0.0
        if masked:
            row_t = i * TQ + lax.broadcasted_iota(I32, (TQ, CK), 0)
            pos = off + lax.broadcasted_iota(I32, (TQ, CK), 1)
            sc = jnp.where(pos <= row_t, sc, -jnp.inf)
        bits = pltpu.bitcast(sc, I32)
        key_ref[:, pl.ds(off, CK)] = bits ^ ((bits >> 31) & I32(0x7FFFFFFF))

    def score_body(c, carry):
        score_chunk(c, False)
        return carry

    def score_pair(c, carry):
        score_chunk(2 * c, False)
        score_chunk(2 * c + 1, False)
        return carry

    n_plain = n_chunks - 1
    lax.fori_loop(0, n_plain >> 1, score_pair, 0)
    lax.fori_loop((n_plain >> 1) * 2, n_plain, score_body, 0)
    score_chunk(n_chunks - 1, True)

    @pl.when((n_chunks & 1) == 1)
    def _():
        key_ref[:, pl.ds(pl.multiple_of(n_chunks * CK, CK), CK)] = jnp.full((TQ, CK), NEG_INF_KEY, I32)

    def count_ge(cand):
        cand_all = jnp.broadcast_to(cand, (TQ, L))
        accs = []
        for r in range(Q_SUB):
            rows = slice(r * L, (r + 1) * L)
            cand_b = cand_all[rows]

            def body(c, acc, rows=rows, cand_b=cand_b):
                kk = key_ref[rows, pl.ds(pl.multiple_of(c * (2 * CK), 2 * CK), 2 * CK)]
                for s in range(2 * CK // L):
                    acc = acc + (kk[:, s * L:(s + 1) * L] >= cand_b).astype(I32)
                return acc
            accs.append(lax.fori_loop(0, (n_chunks + 1) >> 1, body, jnp.zeros((L, L), I32)))
        acc = accs[0] if Q_SUB == 1 else jnp.concatenate(accs, axis=0)
        return jnp.sum(acc, axis=1, keepdims=True)

    def bit_step(thr, cnt_thr, bit):
        cand = thr + bit
        cnt = count_ge(cand)
        ok = cnt >= top_k
        return jnp.where(ok, cand, thr), jnp.where(ok, cnt, cnt_thr)

    def high_step(b, st):
        return bit_step(st[0], st[1], lax.shift_left(I32(1), I32(31) - b))

    thr, cnt_thr = lax.fori_loop(
        0, 32 - HALF_BITS, high_step,
        (jnp.full((TQ, 1), INT_MIN, I32), jnp.full((TQ, 1), S, I32)))

    cnt_next = count_ge(thr + 1)
    exact = cnt_next < top_k

    def low_cond(st):
        b, _, _, active = st
        return (b < HALF_BITS) & (active > 0)

    def low_step(st):
        b, thr, cnt_thr, _ = st
        new_thr, new_cnt = bit_step(thr, cnt_thr, lax.shift_left(I32(1), I32(HALF_BITS - 1) - b))
        thr = jnp.where(exact, thr, new_thr)
        cnt_thr = jnp.where(exact, cnt_thr, new_cnt)
        active = jnp.max(jnp.where(exact | (cnt_thr == top_k), 0, 1))
        return b + 1, thr, cnt_thr, active

    active0 = jnp.max(jnp.where(exact | (cnt_thr == top_k), 0, 1))
    _, thr, cnt_thr, _ = lax.while_loop(low_cond, low_step, (I32(0), thr, cnt_thr, active0))

    t_sel = jnp.where(cnt_thr == top_k, thr - 1, thr)
    need = (top_k - count_ge(t_sel + 1)).astype(F32)
    t_sel_b = jnp.broadcast_to(t_sel, (TQ, L))
    need_b = jnp.broadcast_to(need, (TQ, L))
    uo = uo_ref[...]

    def mask_chunk(c, run):
        off = pl.multiple_of(c * CK, CK)
        kc = key_ref[:, pl.ds(off, CK)]
        out = []
        for s in range(CK // L):
            kk = kc[:, s * L:(s + 1) * L]
            eq = kk == t_sel_b
            pr = jnp.dot(jnp.where(eq, 1.0, 0.0).astype(BF16), uo, preferred_element_type=F32)
            sel = (kk > t_sel_b) | (eq & (run + pr[:, :L] <= need_b))
            sel = sel & (kk > NEG_INF_KEY)
            out.append(jnp.where(sel, 0.0, -jnp.inf).astype(F32))
            run = run + pr[:, L:]
        key_ref[:, pl.ds(off, CK)] = pltpu.bitcast(jnp.concatenate(out, axis=1), I32)
        return run

    def mask_pair(c, run):
        return mask_chunk(2 * c + 1, mask_chunk(2 * c, run))

    run = lax.fori_loop(0, n_chunks >> 1, mask_pair, jnp.zeros((TQ, L), F32))
    lax.fori_loop((n_chunks >> 1) * 2, n_chunks, mask_chunk, run)

    q2 = q_ref[...] * LOG2E
    q_ext = []
    shift_max = jnp.zeros((TQ, L), F32)
    for h in range(DSA_HEADS):
        blk = q2[:, (h // 2) * L:(h // 2 + 1) * L]
        if h % 2:
            blk = pltpu.roll(blk, DSA_HEAD_DIM, 1)
        qf = jnp.where(lane < DSA_HEAD_DIM, blk, 0.0).astype(BF16).astype(F32)
        bound = jnp.sqrt(jnp.sum(qf * qf, axis=1, keepdims=True) * kmx_ref[h:h + 1, :]) * (1.0 + 2.0 ** -7)
        shift_max = jnp.maximum(shift_max, bound)
        q_ext.append(jnp.where(lane == DSA_HEAD_DIM, bound, qf).astype(BF16))
    minus_one_row = jnp.where(lax.broadcasted_iota(I32, (DSA_HEAD_DIM, CK), 0) == 0, -1.0, 0.0).astype(BF16)
    ones_cols = jnp.ones((CK, L), BF16)

    acc_ref[...] = jnp.zeros(acc_ref.shape, F32)

    def attend(c, biased, fast):
        off = pl.multiple_of(c * CK, CK)
        width = CK
        mb = pltpu.bitcast(key_ref[:, pl.ds(off, width)], F32)
        for h in range(DSA_HEADS):
            k_ext = jnp.concatenate(
                [kt_ref[h * DSA_HEAD_DIM:(h + 1) * DSA_HEAD_DIM, pl.ds(off, width)], minus_one_row], axis=0)
            s = jnp.dot(q_ext[h], k_ext, preferred_element_type=F32) + mb
            if biased:
                def tile(r, j):
                    return tb_ref[h, jnp.clip(first_blk + r - (c * (CK // L) + j), 0, REL_SLABS - 1)]
                s = s + jnp.concatenate(
                    [jnp.concatenate([tile(r, j) for r in range(Q_SUB)], axis=0) for j in range(CK // L)], axis=1)
            pair = (h // 2) * L
            v_ext = jnp.concatenate([v_ref[pl.ds(off, width), pair:pair + L], ones_cols], axis=1)
            if fast:
                p = jnp.exp2(s).astype(BF16)
                acc_ref[h] += jnp.dot(p, v_ext, preferred_element_type=F32)
            else:
                m_old = m_ref[h]
                m_new = jnp.maximum(m_old, jnp.max(s, axis=1, keepdims=True))
                alpha = jnp.exp2(m_old - m_new)
                p = jnp.concatenate(
                    [jnp.exp2(s[:, j * L:(j + 1) * L] - m_new) for j in range(width // L)], axis=1)
                pv = jnp.dot(p.astype(BF16), v_ext, preferred_element_type=F32)
                acc_ref[h] = acc_ref[h] * jnp.concatenate([alpha, alpha], axis=1) + pv
                m_ref[h] = m_new

    def attend_all(fast):
        first_biased = jnp.maximum(first_blk - 1, 0) // (CK // L)

        def far_chunk(c, carry):
            attend(c, False, fast)
            return carry

        def near_chunk(c, carry):
            attend(c, True, fast)
            return carry

        def far_pair(c, carry):
            attend(2 * c, False, fast)
            attend(2 * c + 1, False, fast)
            return carry

        single_from = 0
        if fast:
            single_from = (first_biased >> 1) * 2
            lax.fori_loop(0, first_biased >> 1, far_pair, 0)
        lax.fori_loop(single_from, first_biased, far_chunk, 0)
        lax.fori_loop(first_biased, n_chunks, near_chunk, 0)

    use_fast = 2.0 * jnp.max(shift_max) + jnp.max(jnp.abs(tb_ref[...])) <= FAST_SOFTMAX_MAX_SHIFT

    @pl.when(use_fast)
    def _():
        attend_all(True)

    @pl.when(jnp.logical_not(use_fast))
    def _():
        m_ref[...] = jnp.full(m_ref.shape, NEG_BIG, F32)
        attend_all(False)

    for p in range(DSA_HEADS // 2):
        a0, a1 = acc_ref[2 * p], acc_ref[2 * p + 1]
        y_ref[:, p * L:(p + 1) * L] = jnp.where(lane < DSA_HEAD_DIM, a0[:, :L] / a0[:, L:], a1[:, :L] / a1[:, L:])


def _dsa(iq, iw, q, kit, kt, v, tb, *, top_k):
    B, S, W = q.shape
    upper = jnp.triu(jnp.ones((LANES, LANES), F32))
    uo = jnp.concatenate([upper, jnp.ones((LANES, LANES), F32)], axis=1).astype(BF16)
    qrow = lambda b, i: (b, i, 0)
    whole = lambda b, i: (b, 0, 0)
    one = pl.Buffered(1)
    return pl.pallas_call(
        functools.partial(_dsa_kernel, top_k=top_k),
        grid=(B, S // Q_BLOCK),
        in_specs=[pl.BlockSpec((None, Q_BLOCK, LANES), qrow),
                  pl.BlockSpec((None, Q_BLOCK, LANES), qrow),
                  pl.BlockSpec((None, Q_BLOCK, W), qrow),
                  pl.BlockSpec((None, IDX_DIM, S), whole, pipeline_mode=one),
                  pl.BlockSpec((None, W, S), whole, pipeline_mode=one),
                  pl.BlockSpec((None, S, W), whole, pipeline_mode=one),
                  pl.BlockSpec((DSA_HEADS, REL_SLABS, LANES, LANES), lambda b, i: (0, 0, 0, 0)),
                  pl.BlockSpec((LANES, 2 * LANES), lambda b, i: (0, 0))],
        out_specs=pl.BlockSpec((None, Q_BLOCK, W), qrow),
        out_shape=jax.ShapeDtypeStruct((B, S, W), F32),
        scratch_shapes=[pltpu.VMEM((Q_BLOCK, S), I32),
                        pltpu.VMEM((SUBLANES, LANES), F32),
                        pltpu.VMEM((DSA_HEADS, Q_BLOCK, LANES), F32),
                        pltpu.VMEM((DSA_HEADS, Q_BLOCK, 2 * LANES), F32)],
        compiler_params=pltpu.CompilerParams(
            dimension_semantics=("parallel", "arbitrary"), vmem_limit_bytes=VMEM_LIMIT_BYTES),
        name="dsa",
    )(iq, iw, q, kit, kt, v, tb, uo)


def _layer_norm(z, g, b):
    mu = jnp.mean(z, axis=-1, keepdims=True)
    zc = z - mu
    var = jnp.mean(zc * zc, axis=-1, keepdims=True)
    return zc * lax.rsqrt(var + LN_EPS) * g + b


def _mix_kernel(x_ref, ya_ref, yb_ref, yc_ref, wa_ref, wb_ref, wc_ref, g_ref, b_ref, o_ref):
    mix = jnp.dot(ya_ref[...].astype(BF16), wa_ref[...], preferred_element_type=F32)
    mix = mix + jnp.dot(yb_ref[...].astype(BF16), wb_ref[...], preferred_element_type=F32)
    mix = mix + jnp.dot(yc_ref[...].astype(BF16), wc_ref[...], preferred_element_type=F32)
    o_ref[...] = _layer_norm(DN_ALPHA * x_ref[...] + mix, g_ref[...], b_ref[...])


def _mix(x, ya, yb, yc, wa, wb, wc, g, b, *, tm):
    T, D = x.shape
    row = lambda m: (m, 0)
    const = lambda m: (0, 0)
    return pl.pallas_call(
        _mix_kernel,
        grid=(T // tm,),
        in_specs=[pl.BlockSpec((tm, D), row),
                  pl.BlockSpec((tm, ya.shape[1]), row), pl.BlockSpec((tm, yb.shape[1]), row),
                  pl.BlockSpec((tm, yc.shape[1]), row),
                  pl.BlockSpec(wa.shape, const), pl.BlockSpec(wb.shape, const), pl.BlockSpec(wc.shape, const),
                  pl.BlockSpec((1, D), const), pl.BlockSpec((1, D), const)],
        out_specs=pl.BlockSpec((tm, D), row),
        out_shape=jax.ShapeDtypeStruct((T, D), F32),
        compiler_params=pltpu.CompilerParams(
            dimension_semantics=("parallel",), vmem_limit_bytes=VMEM_LIMIT_BYTES),
        name="out_proj_ln",
    )(x, ya, yb, yc, wa, wb, wc, g, b)


def _ffn_kernel(x_ref, wg_ref, wu_ref, wd_ref, g_ref, b_ref, o_ref, xb_ref, acc_ref):
    f = pl.program_id(1)

    @pl.when(f == 0)
    def _():
        xb_ref[...] = x_ref[...].astype(BF16)
        acc_ref[...] = jnp.zeros_like(acc_ref)

    xb = xb_ref[...]
    gate = jnp.dot(xb, wg_ref[...], preferred_element_type=F32)
    up = jnp.dot(xb, wu_ref[...], preferred_element_type=F32)
    h = (gate * _sigmoid(gate)) * up
    acc_ref[...] += jnp.dot(h.astype(BF16), wd_ref[...], preferred_element_type=F32)

    @pl.when(f == pl.num_programs(1) - 1)
    def _():
        o_ref[...] = _layer_norm(DN_ALPHA * x_ref[...] + acc_ref[...], g_ref[...], b_ref[...])


def _ffn(x, wg, wu, wd, g, b, *, tm, tf):
    T, D = x.shape
    FF = wg.shape[1]
    row = lambda m, f: (m, 0)
    const = lambda m, f: (0, 0)
    return pl.pallas_call(
        _ffn_kernel,
        grid=(T // tm, FF // tf),
        in_specs=[pl.BlockSpec((tm, D), row),
                  pl.BlockSpec((D, tf), lambda m, f: (0, f)),
                  pl.BlockSpec((D, tf), lambda m, f: (0, f)),
                  pl.BlockSpec((tf, D), lambda m, f: (f, 0)),
                  pl.BlockSpec((1, D), const), pl.BlockSpec((1, D), const)],
        out_specs=pl.BlockSpec((tm, D), row),
        out_shape=jax.ShapeDtypeStruct((T, D), F32),
        scratch_shapes=[pltpu.VMEM((tm, D), BF16), pltpu.VMEM((tm, D), F32)],
        compiler_params=pltpu.CompilerParams(
            dimension_semantics=("parallel", "arbitrary"), vmem_limit_bytes=VMEM_LIMIT_BYTES),
        name="ffn_ln",
    )(x, wg, wu, wd, g, b)


def _split_w_in(w):
    parts = [w[:, IN_OFFSETS[j]:IN_OFFSETS[j + 1]] for j in range(len(IN_SIZES))]
    lru_x, lru_g, gq, gk, gv, gg, g_lr, dq, dk, dv, iq, ik, iw = parts
    D = w.shape[0]
    pad = lambda a, n: jnp.pad(a, ((0, 0), (0, n - a.shape[1])))
    wa = jnp.concatenate([lru_x, lru_g, gq, gk, gv, gg, pad(g_lr, LANES),
                          dq * (DSA_HEAD_DIM ** -0.5), dv, iq, pad(iw, LANES)], axis=1)
    wt = jnp.concatenate([dk, ik], axis=1).T
    assert wa.shape == (D, PA_END) and wt.shape == (PT_ROWS, D)
    return wa.astype(BF16), wt.astype(BF16)


def _block_diag(w):
    n, k, _ = w.shape
    eye = jnp.eye(n, dtype=w.dtype)
    return (eye[:, None, :, None] * w[:, :, None, :]).reshape(n * k, n * k)


def kernel(x, w_in, conv_w, conv_b, lru_wa, lru_ba, lru_wx, lru_bx, lru_lambda, gla_w_gate2, gla_b_gate,
           gla_norm_g, rel_bias, w_out, ln1_g, ln1_b, w_ffn_gate, w_ffn_up, w_ffn_down, ln2_g, ln2_b):
    B, S, D = x.shape
    T = B * S
    top_k = min(IDX_TOPK_MAX, S // 4)
    assert S % (2 * KEY_CHUNK) == 0 and D == D_MODEL
    tm = min(512, S)
    tb = _relbias_tiles(rel_bias)
    row = lambda a: a.reshape(1, -1)
    for l in range(w_in.shape[0]):
        wa, wt = _split_w_in(w_in[l])
        p_lru, p_gla, dq, dv, iq, iw, kt, kit = _proj(x, wa, wt, tm=tm)
        y_lru = _lru(p_lru, conv_w[l], row(conv_b[l]),
                     _block_diag(lru_wa[l]).astype(BF16), row(lru_ba[l]),
                     _block_diag(lru_wx[l]).astype(BF16), row(lru_bx[l]), row(lru_lambda[l]), tl=tm)
        w2 = jnp.pad(gla_w_gate2[l], ((0, LANES - GLA_GATE_RANK), (0, 0)))
        y_gla = _gla(p_gla, w2, row(gla_b_gate[l]), row(gla_norm_g[l]), tg=min(256, S))
        y_dsa = _dsa(iq, iw, dq, kit, kt, dv, tb, top_k=top_k)
        wo = w_out[l].astype(BF16)
        x1 = _mix(x.reshape(T, D), y_lru.reshape(T, -1), y_gla.reshape(T, -1), y_dsa.reshape(T, -1),
                  wo[:LRU_WIDTH], wo[LRU_WIDTH:LRU_WIDTH + GLA_WIDTH], wo[LRU_WIDTH + GLA_WIDTH:],
                  row(ln1_g[l]), row(ln1_b[l]), tm=tm)
        x2 = _ffn(x1, w_ffn_gate[l].astype(BF16), w_ffn_up[l].astype(BF16), w_ffn_down[l].astype(BF16),
                  row(ln2_g[l]), row(ln2_b[l]), tm=tm, tf=D_FF // 2)
        x = x2.reshape(B, S, D)
    return x
```

```python
import functools
import math

import jax
import jax.numpy as jnp
import numpy as np
from jax import lax
from jax.experimental import pallas as pl
from jax.experimental.pallas import tpu as pltpu

F32 = jnp.float32
BF16 = jnp.bfloat16
I32 = jnp.int32

D_MODEL = 1024
DEPTH = 2
LRU_WIDTH = 256
LRU_BLOCKS = 4
LRU_BLOCK = LRU_WIDTH // LRU_BLOCKS
CONV_WIDTH = 4
LRU_C = 8.0
GLA_HEADS = 6
GLA_DK = 64
GLA_DV = 64
GLA_GATE_RANK = 16
GLA_GATE_TAU = 16.0
GLA_CHUNK = 64
GLA_WIDTH = GLA_HEADS * GLA_DV
DSA_HEADS = 6
DSA_HEAD_DIM = 64
DSA_WIDTH = DSA_HEADS * DSA_HEAD_DIM
IDX_HEADS = 4
IDX_DIM = 32
IDX_TOPK_MAX = 256
REL_BUCKETS = 32
REL_MAX_DIST = 128
D_FF = 2816
DN_ALPHA = (2.0 * DEPTH) ** 0.25
LN_EPS = 1e-5

IN_SIZES = (LRU_WIDTH, LRU_WIDTH, GLA_WIDTH, GLA_WIDTH, GLA_WIDTH, GLA_WIDTH, GLA_GATE_RANK,
            DSA_WIDTH, DSA_WIDTH, DSA_WIDTH, IDX_HEADS * IDX_DIM, IDX_DIM, IDX_HEADS)
IN_OFFSETS = [0] + [int(o) for o in np.cumsum(IN_SIZES)]

LANES = 128
SUBLANES = 8
VMEM_LIMIT_BYTES = 56 * 1024 * 1024

LRU_COLS = 2 * LRU_WIDTH
GLA_COLS = 4 * GLA_WIDTH + LANES
PA_LRU = 0
PA_GLA = PA_LRU + LRU_COLS
PA_DQ = PA_GLA + GLA_COLS
PA_DV = PA_DQ + DSA_WIDTH
PA_IQ = PA_DV + DSA_WIDTH
PA_IW = PA_IQ + LANES
PA_END = PA_IW + LANES
PT_ROWS = DSA_WIDTH + IDX_DIM

Q_BLOCK = 256
Q_SUB = Q_BLOCK // LANES
KEY_CHUNK = 512
HALF_BITS = 16
INT_MIN = -2147483648
NEG_INF_KEY = -2139095041
NEG_BIG = -1e30
LOG2E = 1.4426950408889634
REL_SLABS = 3
FAST_SOFTMAX_MAX_SHIFT = 120.0


def _nt_dot(a, b):
    return lax.dot_general(a, b, (((1,), (1,)), ((), ())), preferred_element_type=F32)


def _tn_dot(a, b):
    return lax.dot_general(a, b, (((0,), (0,)), ((), ())), preferred_element_type=F32)


def _split3(x):
    x1 = x.astype(BF16)
    r1 = x - x1.astype(F32)
    x2 = r1.astype(BF16)
    return x1, x2, (r1 - x2.astype(F32)).astype(BF16)


def _softplus(z):
    return jnp.maximum(z, 0.0) + jnp.log(1.0 + jnp.exp(-jnp.abs(z)))


def _sigmoid(z):
    return 1.0 / (1.0 + jnp.exp(-z))


def _proj_kernel(x_ref, wa_ref, wt_ref, lru_ref, gla_ref, q_ref, v_ref, iq_ref, iw_ref, kt_ref, kit_ref):
    xb = x_ref[...].astype(BF16)

    def mm(c0, c1):
        return jnp.dot(xb, wa_ref[:, c0:c1], preferred_element_type=F32)

    lru_ref[...] = mm(PA_LRU, PA_GLA)
    gla_ref[...] = mm(PA_GLA, PA_DQ)
    q_ref[...] = mm(PA_DQ, PA_DV)
    v_ref[...] = mm(PA_DV, PA_IQ).astype(BF16)
    iq_ref[...] = mm(PA_IQ, PA_IW).astype(BF16)
    iw_ref[...] = mm(PA_IW, PA_END)
    t = _nt_dot(wt_ref[...], xb)
    kt_ref[...] = t[:DSA_WIDTH].astype(BF16)
    kit_ref[...] = t[DSA_WIDTH:PT_ROWS].astype(BF16)


def _proj(x, wa, wt, *, tm):
    B, S, D = x.shape
    grid = (B, S // tm)
    row = lambda b, m: (b, m, 0)
    col = lambda b, m: (b, 0, m)
    const = lambda b, m: (0, 0)
    out_shape = (
        jax.ShapeDtypeStruct((B, S, LRU_COLS), F32),
        jax.ShapeDtypeStruct((B, S, GLA_COLS), F32),
        jax.ShapeDtypeStruct((B, S, DSA_WIDTH), F32),
        jax.ShapeDtypeStruct((B, S, DSA_WIDTH), BF16),
        jax.ShapeDtypeStruct((B, S, LANES), BF16),
        jax.ShapeDtypeStruct((B, S, LANES), F32),
        jax.ShapeDtypeStruct((B, DSA_WIDTH, S), BF16),
        jax.ShapeDtypeStruct((B, IDX_DIM, S), BF16),
    )
    out_specs = (
        pl.BlockSpec((None, tm, LRU_COLS), row),
        pl.BlockSpec((None, tm, GLA_COLS), row),
        pl.BlockSpec((None, tm, DSA_WIDTH), row),
        pl.BlockSpec((None, tm, DSA_WIDTH), row),
        pl.BlockSpec((None, tm, LANES), row),
        pl.BlockSpec((None, tm, LANES), row),
        pl.BlockSpec((None, DSA_WIDTH, tm), col),
        pl.BlockSpec((None, IDX_DIM, tm), col),
    )
    return pl.pallas_call(
        _proj_kernel,
        grid=grid,
        in_specs=[pl.BlockSpec((None, tm, D), row),
                  pl.BlockSpec((D, PA_END), const),
                  pl.BlockSpec((PT_ROWS, D), const)],
        out_specs=out_specs,
        out_shape=out_shape,
        compiler_params=pltpu.CompilerParams(
            dimension_semantics=("parallel", "parallel"), vmem_limit_bytes=VMEM_LIMIT_BYTES),
        name="in_proj",
    )(x, wa, wt)


def _shift_rows(x, d, fill, row):
    return jnp.where(row >= d, pltpu.roll(x, d, 0), fill)


def _lru_kernel(p_ref, cw_ref, cb_ref, wa_ref, ba_ref, wx_ref, bx_ref, lam_ref, y_ref, xprev_ref, h_ref):
    tl = p_ref.shape[0]

    @pl.when(pl.program_id(1) == 0)
    def _():
        xprev_ref[...] = jnp.zeros_like(xprev_ref)
        h_ref[...] = jnp.zeros_like(h_ref)

    xb = p_ref[:, 0:LRU_WIDTH]
    gb = p_ref[:, LRU_WIDTH:2 * LRU_WIDTH]
    prev = xprev_ref[...]
    row8 = lax.broadcasted_iota(I32, (SUBLANES, LRU_WIDTH), 0)
    cw = cw_ref[...]
    xc = cb_ref[...] + cw[CONV_WIDTH - 1:CONV_WIDTH] * xb
    for d in range(1, CONV_WIDTH):
        r = pltpu.roll(xb, d, 0)
        top = jnp.where(row8 < d, pltpu.roll(prev, d, 0), r[0:SUBLANES])
        r = jnp.concatenate([top, r[SUBLANES:]], axis=0)
        xc = xc + cw[CONV_WIDTH - 1 - d:CONV_WIDTH - d] * r
    xprev_ref[...] = xb[tl - SUBLANES:tl]

    xcb = xc.astype(BF16)
    r_gate = _sigmoid(jnp.dot(xcb, wa_ref[...], preferred_element_type=F32) + ba_ref[...])
    i_gate = _sigmoid(jnp.dot(xcb, wx_ref[...], preferred_element_type=F32) + bx_ref[...])
    log_a = (-LRU_C) * r_gate * _softplus(-lam_ref[...])
    a = jnp.exp(log_a)
    u = jnp.sqrt(1.0 - a * a) * (i_gate * xc)

    row = lax.broadcasted_iota(I32, (tl, LRU_WIDTH), 0)
    d = 1
    while d < tl:
        a_s = _shift_rows(a, d, 1.0, row)
        u_s = _shift_rows(u, d, 0.0, row)
        u = u + a * u_s
        a = a * a_s
        d *= 2
    h = u + a * h_ref[SUBLANES - 1:SUBLANES, :]
    h_ref[...] = h[tl - SUBLANES:tl]
    y_ref[...] = h * jax.nn.gelu(gb)


def _lru(p_lru, cw, cb, wa, ba, wx, bx, lam, *, tl):
    B, S, _ = p_lru.shape
    const = lambda b, j: (0, 0)
    vec = pl.BlockSpec((1, LRU_WIDTH), const)
    mat = pl.BlockSpec((LRU_WIDTH, LRU_WIDTH), const)
    return pl.pallas_call(
        _lru_kernel,
        grid=(B, S // tl),
        in_specs=[pl.BlockSpec((None, tl, LRU_COLS), lambda b, j: (b, j, 0)),
                  pl.BlockSpec((CONV_WIDTH, LRU_WIDTH), const), vec, mat, vec, mat, vec, vec],
        out_specs=pl.BlockSpec((None, tl, LRU_WIDTH), lambda b, j: (b, j, 0)),
        out_shape=jax.ShapeDtypeStruct((B, S, LRU_WIDTH), F32),
        scratch_shapes=[pltpu.VMEM((SUBLANES, LRU_WIDTH), F32), pltpu.VMEM((SUBLANES, LRU_WIDTH), F32)],
        compiler_params=pltpu.CompilerParams(
            dimension_semantics=("parallel", "arbitrary"), vmem_limit_bytes=VMEM_LIMIT_BYTES),
        name="rg_lru",
    )(p_lru, cw, cb, wa, ba, wx, bx, lam)


def _gla_kernel(q_ref, k_ref, v_ref, g_ref, glr_ref, w2_ref, bg_ref, ng_ref, tri_ref, mean_ref,
                y_ref, st_ref):
    tg = q_ref.shape[0]
    C = GLA_CHUNK
    hi = lax.Precision.HIGHEST

    @pl.when(pl.program_id(1) == 0)
    def _():
        st_ref[...] = jnp.zeros_like(st_ref)

    z = jnp.dot(glr_ref[...], w2_ref[...], preferred_element_type=F32, precision=hi) + bg_ref[...]
    log_alpha = -_softplus(-z) * (1.0 / GLA_GATE_TAU)

    lane = lax.broadcasted_iota(I32, (C, LANES), 1)
    first = lane < GLA_DK
    rr = lax.broadcasted_iota(I32, (C, C), 0)
    cc = lax.broadcasted_iota(I32, (C, C), 1)
    causal = cc <= rr
    causal2 = jnp.concatenate([causal, causal], axis=0)
    r2 = lax.broadcasted_iota(I32, (LANES, LANES), 0)
    c2 = lax.broadcasted_iota(I32, (LANES, LANES), 1)
    same_head = (r2 < GLA_DV) == (c2 < GLA_DK)
    tri = tri_ref[...]

    for c in range(tg // C):
        rows = slice(c * C, (c + 1) * C)
        bcum = sum(jnp.dot(tri, piece, preferred_element_type=F32)
                   for piece in _split3(log_alpha[rows]))
        blast = bcum[C - 1:C]
        kf = k_ref[rows, :]
        q_dec = q_ref[rows, :] * (GLA_DK ** -0.5) * jnp.exp(bcum)
        k_inv = kf * jnp.exp(-bcum)
        k_end = kf * jnp.exp(blast - bcum)
        decay = jnp.exp(blast)
        vf = v_ref[rows, :]
        outs = []
        for p in range(GLA_HEADS // 2):
            cs = slice(p * LANES, (p + 1) * LANES)
            qd, ki, ke, vp = q_dec[:, cs], k_inv[:, cs], k_end[:, cs], vf[:, cs]
            kib = ki.astype(BF16)
            vpb = vp.astype(BF16)
            q_two = jnp.concatenate([jnp.where(first, qd, 0.0), jnp.where(first, 0.0, qd)], axis=0).astype(BF16)
            att = jnp.where(causal2, _nt_dot(q_two, kib), 0.0).astype(BF16)
            o_two = jnp.dot(att, vpb, preferred_element_type=F32)
            o_intra = jnp.where(first, o_two[:C], o_two[C:])
            st = st_ref[p]
            o_inter = _nt_dot(qd.astype(BF16), st.astype(BF16))
            u_t = _tn_dot(vpb, ke.astype(BF16))
            st_ref[p] = st * decay[:, cs] + jnp.where(same_head, u_t, 0.0)
            outs.append(o_intra + o_inter)
        o = jnp.concatenate(outs, axis=1)
        ms = sum(jnp.dot(piece, mean_ref[...], preferred_element_type=F32)
                 for piece in _split3(o * o))
        o = o * lax.rsqrt(ms + 1e-6) * ng_ref[...]
        gf = g_ref[rows, :]
        y_ref[rows, :] = o * (gf * _sigmoid(gf))


def _gla(p_gla, w2, bg, ng, *, tg):
    B, S, _ = p_gla.shape
    W = GLA_WIDTH
    const = lambda b, j: (0, 0)
    tri = jnp.tril(jnp.ones((GLA_CHUNK, GLA_CHUNK), F32)).astype(BF16)
    head = jnp.arange(W) // GLA_DV
    mean_blk = ((head[:, None] == head[None, :]).astype(F32) / GLA_DV).astype(BF16)

    def colblk(c):
        return pl.BlockSpec((None, tg, W), lambda b, j: (b, j, c))

    return pl.pallas_call(
        _gla_kernel,
        grid=(B, S // tg),
        in_specs=[colblk(0), colblk(1), colblk(2), colblk(3),
                  pl.BlockSpec((None, tg, LANES), lambda b, j: (b, j, 4 * W // LANES)),
                  pl.BlockSpec((LANES, W), const),
                  pl.BlockSpec((1, W), const), pl.BlockSpec((1, W), const),
                  pl.BlockSpec((GLA_CHUNK, GLA_CHUNK), const),
                  pl.BlockSpec((W, W), const)],
        out_specs=pl.BlockSpec((None, tg, W), lambda b, j: (b, j, 0)),
        out_shape=jax.ShapeDtypeStruct((B, S, W), F32),
        scratch_shapes=[pltpu.VMEM((GLA_HEADS // 2, LANES, LANES), F32)],
        compiler_params=pltpu.CompilerParams(
            dimension_semantics=("parallel", "arbitrary"), vmem_limit_bytes=VMEM_LIMIT_BYTES),
        name="gla",
    )(p_gla, p_gla, p_gla, p_gla, p_gla, w2, bg, ng, tri, mean_blk)


def _relbias_kernel(rb_ref, tb_ref):
    r = lax.broadcasted_iota(I32, (LANES, LANES), 0)
    c = lax.broadcasted_iota(I32, (LANES, LANES), 1)
    max_exact = REL_BUCKETS // 2
    for delta in range(REL_SLABS):
        n = jnp.maximum(delta * LANES + r - c, 0)
        nf = jnp.maximum(n, 1).astype(F32)
        large = max_exact + (jnp.log(nf / max_exact) / math.log(REL_MAX_DIST / max_exact)
                             * (REL_BUCKETS - max_exact)).astype(I32)
        large = jnp.minimum(large, REL_BUCKETS - 1)
        bucket = jnp.where(n < max_exact, n, large)
        for h in range(DSA_HEADS):
            acc = jnp.zeros((LANES, LANES), F32)
            for b in range(REL_BUCKETS):
                acc = jnp.where(bucket == b, rb_ref[b, h], acc)
            tb_ref[h, delta] = (acc - rb_ref[REL_BUCKETS - 1, h]) * LOG2E


def _relbias_tiles(rel_bias):
    return pl.pallas_call(
        _relbias_kernel,
        in_specs=[pl.BlockSpec(memory_space=pltpu.SMEM)],
        out_specs=pl.BlockSpec(memory_space=pltpu.VMEM),
        out_shape=jax.ShapeDtypeStruct((DSA_HEADS, REL_SLABS, LANES, LANES), F32),
        name="rel_bias_tiles",
    )(rel_bias)


def _dsa_kernel(iq_ref, iw_ref, q_ref, kit_ref, kt_ref, v_ref, tb_ref, uo_ref, y_ref,
                key_ref, kmx_ref, m_ref, acc_ref, *, top_k):
    i = pl.program_id(1)
    TQ, CK, L = Q_BLOCK, KEY_CHUNK, LANES
    S = key_ref.shape[1]
    first_blk = i * Q_SUB
    n_chunks = (first_blk + Q_SUB - 1) // (CK // L) + 1
    lane = lax.broadcasted_iota(I32, (TQ, L), 1)

    @pl.when(i == 0)
    def _():
        nb = 2 * CK
        for h in range(DSA_HEADS):
            def norm_chunk(c, mx):
                kk = kt_ref[h * DSA_HEAD_DIM:(h + 1) * DSA_HEAD_DIM, pl.ds(pl.multiple_of(c * nb, nb), nb)]
                kk = kk.astype(F32)
                return jnp.maximum(mx, jnp.sum(kk * kk, axis=0, keepdims=True))
            mx = lax.fori_loop(0, S // nb, norm_chunk, jnp.zeros((1, nb), F32))
            kmx_ref[h:h + 1, :] = jnp.broadcast_to(jnp.max(mx, axis=1, keepdims=True), (1, L))

    iq = iq_ref[...]
    iq_h = [iq[:, h * IDX_DIM:(h + 1) * IDX_DIM] for h in range(IDX_HEADS)]
    scale = (IDX_DIM ** -0.5) * (IDX_HEADS ** -0.5)
    w_h = [jnp.broadcast_to(iw_ref[:, h:h + 1] * scale, (TQ, CK)) for h in range(IDX_HEADS)]

    def score_chunk(c, masked):
        off = pl.multiple_of(c * CK, CK)
        kc = kit_ref[:, pl.ds(off, CK)]
        sc = jnp.zeros((TQ, CK), F32)
        for h in range(IDX_HEADS):
            z = jnp.dot(iq_h[h], kc, preferred_element_type=F32)
            sc = sc + jnp.maximum(z, 0.0) * w_h[h]
        sc = sc + 0.0
        if masked:
            row_t = i * TQ + lax.broadcasted_iota(I32, (TQ, CK), 0)
            pos = off + lax.broadcasted_iota(I32, (TQ, CK), 1)
            sc = jnp.where(pos <= row_t, sc, -jnp.inf)
        bits = pltpu.bitcast(sc, I32)
        key_ref[:, pl.ds(off, CK)] = bits ^ ((bits >> 31) & I32(0x7FFFFFFF))

    def score_body(c, carry):
        score_chunk(c, False)
        return carry

    def score_pair(c, carry):
        score_chunk(2 * c, False)
        score_chunk(2 * c + 1, False)
        return carry

    n_plain = n_chunks - 1
    lax.fori_loop(0, n_plain >> 1, score_pair, 0)
    lax.fori_loop((n_plain >> 1) * 2, n_plain, score_body, 0)
    score_chunk(n_chunks - 1, True)

    @pl.when((n_chunks & 1) == 1)
    def _():
        key_ref[:, pl.ds(pl.multiple_of(n_chunks * CK, CK), CK)] = jnp.full((TQ, CK), NEG_INF_KEY, I32)

    def count_ge(cand):
        cand_all = jnp.broadcast_to(cand, (TQ, L))
        accs = []
        for r in range(Q_SUB):
            rows = slice(r * L, (r + 1) * L)
            cand_b = cand_all[rows]

            def body(c, acc, rows=rows, cand_b=cand_b):
                kk = key_ref[rows, pl.ds(pl.multiple_of(c * (2 * CK), 2 * CK), 2 * CK)]
                for s in range(2 * CK // L):
                    acc = acc + (kk[:, s * L:(s + 1) * L] >= cand_b).astype(I32)
                return acc
            accs.append(lax.fori_loop(0, (n_chunks + 1) >> 1, body, jnp.zeros((L, L), I32)))
        acc = accs[0] if Q_SUB == 1 else jnp.concatenate(accs, axis=0)
        return jnp.sum(acc, axis=1, keepdims=True)

    def bit_step(thr, cnt_thr, bit):
        cand = thr + bit
        cnt = count_ge(cand)
        ok = cnt >= top_k
        return jnp.where(ok, cand, thr), jnp.where(ok, cnt, cnt_thr)

    def high_step(b, st):
        return bit_step(st[0], st[1], lax.shift_left(I32(1), I32(31) - b))

    thr, cnt_thr = lax.fori_loop(
        0, 32 - HALF_BITS, high_step,
        (jnp.full((TQ, 1), INT_MIN, I32), jnp.full((TQ, 1), S, I32)))

    cnt_next = count_ge(thr + 1)
    exact = cnt_next < top_k

    def low_cond(st):
        b, _, _, active = st
        return (b < HALF_BITS) & (active > 0)

    def low_step(st):
        b, thr, cnt_thr, _ = st
        new_thr, new_cnt = bit_step(thr, cnt_thr, lax.shift_left(I32(1), I32(HALF_BITS - 1) - b))
        thr = jnp.where(exact, thr, new_thr)
        cnt_thr = jnp.where(exact, cnt_thr, new_cnt)
        active = jnp.max(jnp.where(exact | (cnt_thr == top_k), 0, 1))
        return b + 1, thr, cnt_thr, active

    active0 = jnp.max(jnp.where(exact | (cnt_thr == top_k), 0, 1))
    _, thr, cnt_thr, _ = lax.while_loop(low_cond, low_step, (I32(0), thr, cnt_thr, active0))

    t_sel = jnp.where(cnt_thr == top_k, thr - 1, thr)
    need = (top_k - count_ge(t_sel + 1)).astype(F32)
    t_sel_b = jnp.broadcast_to(t_sel, (TQ, L))
    need_b = jnp.broadcast_to(need, (TQ, L))
    uo = uo_ref[...]

    def mask_chunk(c, run):
        off = pl.multiple_of(c * CK, CK)
        kc = key_ref[:, pl.ds(off, CK)]
        out = []
        for s in range(CK // L):
            kk = kc[:, s * L:(s + 1) * L]
            eq = kk == t_sel_b
            pr = jnp.dot(jnp.where(eq, 1.0, 0.0).astype(BF16), uo, preferred_element_type=F32)
            sel = (kk > t_sel_b) | (eq & (run + pr[:, :L] <= need_b))
            sel = sel & (kk > NEG_INF_KEY)
            out.append(jnp.where(sel, 0.0, -jnp.inf).astype(F32))
            run = run + pr[:, L:]
        key_ref[:, pl.ds(off, CK)] = pltpu.bitcast(jnp.concatenate(out, axis=1), I32)
        return run

    def mask_pair(c, run):
        return mask_chunk(2 * c + 1, mask_chunk(2 * c, run))

    run = lax.fori_loop(0, n_chunks >> 1, mask_pair, jnp.zeros((TQ, L), F32))
    lax.fori_loop((n_chunks >> 1) * 2, n_chunks, mask_chunk, run)

    q2 = q_ref[...] * LOG2E
    q_ext = []
    shift_max = jnp.zeros((TQ, L), F32)
    for h in range(DSA_HEADS):
        blk = q2[:, (h // 2) * L:(h // 2 + 1) * L]
        if h % 2:
            blk = pltpu.roll(blk, DSA_HEAD_DIM, 1)
        qf = jnp.where(lane < DSA_HEAD_DIM, blk, 0.0).astype(BF16).astype(F32)
        bound = jnp.sqrt(jnp.sum(qf * qf, axis=1, keepdims=True) * kmx_ref[h:h + 1, :]) * (1.0 + 2.0 ** -7)
        shift_max = jnp.maximum(shift_max, bound)
        q_ext.append(jnp.where(lane == DSA_HEAD_DIM, bound, qf).astype(BF16))
    minus_one_row = jnp.where(lax.broadcasted_iota(I32, (DSA_HEAD_DIM, CK), 0) == 0, -1.0, 0.0).astype(BF16)
    ones_cols = jnp.ones((CK, L), BF16)

    acc_ref[...] = jnp.zeros(acc_ref.shape, F32)

    def attend(c, biased, fast):
        off = pl.multiple_of(c * CK, CK)
        width = CK
        mb = pltpu.bitcast(key_ref[:, pl.ds(off, width)], F32)
        for h in range(DSA_HEADS):
            k_ext = jnp.concatenate(
                [kt_ref[h * DSA_HEAD_DIM:(h + 1) * DSA_HEAD_DIM, pl.ds(off, width)], minus_one_row], axis=0)
            s = jnp.dot(q_ext[h], k_ext, preferred_element_type=F32) + mb
            if biased:
                def tile(r, j):
                    return tb_ref[h, jnp.clip(first_blk + r - (c * (CK // L) + j), 0, REL_SLABS - 1)]
                s = s + jnp.concatenate(
                    [jnp.concatenate([tile(r, j) for r in range(Q_SUB)], axis=0) for j in range(CK // L)], axis=1)
            pair = (h // 2) * L
            v_ext = jnp.concatenate([v_ref[pl.ds(off, width), pair:pair + L], ones_cols], axis=1)
            if fast:
                p = jnp.exp2(s).astype(BF16)
                acc_ref[h] += jnp.dot(p, v_ext, preferred_element_type=F32)
            else:
                m_old = m_ref[h]
                m_new = jnp.maximum(m_old, jnp.max(s, axis=1, keepdims=True))
                alpha = jnp.exp2(m_old - m_new)
                p = jnp.concatenate(
                    [jnp.exp2(s[:, j * L:(j + 1) * L] - m_new) for j in range(width // L)], axis=1)
                pv = jnp.dot(p.astype(BF16), v_ext, preferred_element_type=F32)
                acc_ref[h] = acc_ref[h] * jnp.concatenate([alpha, alpha], axis=1) + pv
                m_ref[h] = m_new

    def attend_all(fast):
        first_biased = jnp.maximum(first_blk - 1, 0) // (CK // L)

        def far_chunk(c, carry):
            attend(c, False, fast)
            return carry

        def near_chunk(c, carry):
            attend(c, True, fast)
            return carry

        def far_pair(c, carry):
            attend(2 * c, False, fast)
            attend(2 * c + 1, False, fast)
            return carry

        def far_quad(c, carry):
            far_pair(2 * c, carry)
            return far_pair(2 * c + 1, carry)

        single_from = 0
        if fast:
            single_from = (first_biased >> 1) * 2
            lax.fori_loop(0, first_biased >> 2, far_quad, 0)
            lax.fori_loop((first_biased >> 2) * 2, first_biased >> 1, far_pair, 0)
        lax.fori_loop(single_from, first_biased, far_chunk, 0)
        lax.fori_loop(first_biased, n_chunks, near_chunk, 0)

    use_fast = 2.0 * jnp.max(shift_max) + jnp.max(jnp.abs(tb_ref[...])) <= FAST_SOFTMAX_MAX_SHIFT

    @pl.when(use_fast)
    def _():
        attend_all(True)

    @pl.when(jnp.logical_not(use_fast))
    def _():
        m_ref[...] = jnp.full(m_ref.shape, NEG_BIG, F32)
        attend_all(False)

    for p in range(DSA_HEADS // 2):
        a0, a1 = acc_ref[2 * p], acc_ref[2 * p + 1]
        y_ref[:, p * L:(p + 1) * L] = jnp.where(lane < DSA_HEAD_DIM, a0[:, :L] / a0[:, L:], a1[:, :L] / a1[:, L:])


def _dsa(iq, iw, q, kit, kt, v, tb, *, top_k):
    B, S, W = q.shape
    upper = jnp.triu(jnp.ones((LANES, LANES), F32))
    uo = jnp.concatenate([upper, jnp.ones((LANES, LANES), F32)], axis=1).astype(BF16)
    qrow = lambda b, i: (b, i, 0)
    whole = lambda b, i: (b, 0, 0)
    one = pl.Buffered(1)
    return pl.pallas_call(
        functools.partial(_dsa_kernel, top_k=top_k),
        grid=(B, S // Q_BLOCK),
        in_specs=[pl.BlockSpec((None, Q_BLOCK, LANES), qrow),
                  pl.BlockSpec((None, Q_BLOCK, LANES), qrow),
                  pl.BlockSpec((None, Q_BLOCK, W), qrow),
                  pl.BlockSpec((None, IDX_DIM, S), whole, pipeline_mode=one),
                  pl.BlockSpec((None, W, S), whole, pipeline_mode=one),
                  pl.BlockSpec((None, S, W), whole, pipeline_mode=one),
                  pl.BlockSpec((DSA_HEADS, REL_SLABS, LANES, LANES), lambda b, i: (0, 0, 0, 0)),
                  pl.BlockSpec((LANES, 2 * LANES), lambda b, i: (0, 0))],
        out_specs=pl.BlockSpec((None, Q_BLOCK, W), qrow),
        out_shape=jax.ShapeDtypeStruct((B, S, W), F32),
        scratch_shapes=[pltpu.VMEM((Q_BLOCK, S), I32),
                        pltpu.VMEM((SUBLANES, LANES), F32),
                        pltpu.VMEM((DSA_HEADS, Q_BLOCK, LANES), F32),
                        pltpu.VMEM((DSA_HEADS, Q_BLOCK, 2 * LANES), F32)],
        compiler_params=pltpu.CompilerParams(
            dimension_semantics=("parallel", "arbitrary"), vmem_limit_bytes=VMEM_LIMIT_BYTES),
        name="dsa",
    )(iq, iw, q, kit, kt, v, tb, uo)


def _layer_norm(z, g, b):
    mu = jnp.mean(z, axis=-1, keepdims=True)
    zc = z - mu
    var = jnp.mean(zc * zc, axis=-1, keepdims=True)
    return zc * lax.rsqrt(var + LN_EPS) * g + b


def _mix_kernel(x_ref, ya_ref, yb_ref, yc_ref, wa_ref, wb_ref, wc_ref, g_ref, b_ref, o_ref):
    mix = jnp.dot(ya_ref[...].astype(BF16), wa_ref[...], preferred_element_type=F32)
    mix = mix + jnp.dot(yb_ref[...].astype(BF16), wb_ref[...], preferred_element_type=F32)
    mix = mix + jnp.dot(yc_ref[...].astype(BF16), wc_ref[...], preferred_element_type=F32)
    o_ref[...] = _layer_norm(DN_ALPHA * x_ref[...] + mix, g_ref[...], b_ref[...])


def _mix(x, ya, yb, yc, wa, wb, wc, g, b, *, tm):
    T, D = x.shape
    row = lambda m: (m, 0)
    const = lambda m: (0, 0)
    return pl.pallas_call(
        _mix_kernel,
        grid=(T // tm,),
        in_specs=[pl.BlockSpec((tm, D), row),
                  pl.BlockSpec((tm, ya.shape[1]), row), pl.BlockSpec((tm, yb.shape[1]), row),
                  pl.BlockSpec((tm, yc.shape[1]), row),
                  pl.BlockSpec(wa.shape, const), pl.BlockSpec(wb.shape, const), pl.BlockSpec(wc.shape, const),
                  pl.BlockSpec((1, D), const), pl.BlockSpec((1, D), const)],
        out_specs=pl.BlockSpec((tm, D), row),
        out_shape=jax.ShapeDtypeStruct((T, D), F32),
        compiler_params=pltpu.CompilerParams(
            dimension_semantics=("parallel",), vmem_limit_bytes=VMEM_LIMIT_BYTES),
        name="out_proj_ln",
    )(x, ya, yb, yc, wa, wb, wc, g, b)


def _ffn_kernel(x_ref, wg_ref, wu_ref, wd_ref, g_ref, b_ref, o_ref, xb_ref, acc_ref):
    f = pl.program_id(1)

    @pl.when(f == 0)
    def _():
        xb_ref[...] = x_ref[...].astype(BF16)
        acc_ref[...] = jnp.zeros_like(acc_ref)

    xb = xb_ref[...]
    gate = jnp.dot(xb, wg_ref[...], preferred_element_type=F32)
    up = jnp.dot(xb, wu_ref[...], preferred_element_type=F32)
    h = (gate * _sigmoid(gate)) * up
    acc_ref[...] += jnp.dot(h.astype(BF16), wd_ref[...], preferred_element_type=F32)

    @pl.when(f == pl.num_programs(1) - 1)
    def _():
        o_ref[...] = _layer_norm(DN_ALPHA * x_ref[...] + acc_ref[...], g_ref[...], b_ref[...])


def _ffn(x, wg, wu, wd, g, b, *, tm, tf):
    T, D = x.shape
    FF = wg.shape[1]
    row = lambda m, f: (m, 0)
    const = lambda m, f: (0, 0)
    return pl.pallas_call(
        _ffn_kernel,
        grid=(T // tm, FF // tf),
        in_specs=[pl.BlockSpec((tm, D), row),
                  pl.BlockSpec((D, tf), lambda m, f: (0, f)),
                  pl.BlockSpec((D, tf), lambda m, f: (0, f)),
                  pl.BlockSpec((tf, D), lambda m, f: (f, 0)),
                  pl.BlockSpec((1, D), const), pl.BlockSpec((1, D), const)],
        out_specs=pl.BlockSpec((tm, D), row),
        out_shape=jax.ShapeDtypeStruct((T, D), F32),
        scratch_shapes=[pltpu.VMEM((tm, D), BF16), pltpu.VMEM((tm, D), F32)],
        compiler_params=pltpu.CompilerParams(
            dimension_semantics=("parallel", "arbitrary"), vmem_limit_bytes=VMEM_LIMIT_BYTES),
        name="ffn_ln",
    )(x, wg, wu, wd, g, b)


def _split_w_in(w):
    parts = [w[:, IN_OFFSETS[j]:IN_OFFSETS[j + 1]] for j in range(len(IN_SIZES))]
    lru_x, lru_g, gq, gk, gv, gg, g_lr, dq, dk, dv, iq, ik, iw = parts
    D = w.shape[0]
    pad = lambda a, n: jnp.pad(a, ((0, 0), (0, n - a.shape[1])))
    wa = jnp.concatenate([lru_x, lru_g, gq, gk, gv, gg, pad(g_lr, LANES),
                          dq * (DSA_HEAD_DIM ** -0.5), dv, iq, pad(iw, LANES)], axis=1)
    wt = jnp.concatenate([dk, ik], axis=1).T
    assert wa.shape == (D, PA_END) and wt.shape == (PT_ROWS, D)
    return wa.astype(BF16), wt.astype(BF16)


def _block_diag(w):
    n, k, _ = w.shape
    eye = jnp.eye(n, dtype=w.dtype)
    return (eye[:, None, :, None] * w[:, :, None, :]).reshape(n * k, n * k)


def kernel(x, w_in, conv_w, conv_b, lru_wa, lru_ba, lru_wx, lru_bx, lru_lambda, gla_w_gate2, gla_b_gate,
           gla_norm_g, rel_bias, w_out, ln1_g, ln1_b, w_ffn_gate, w_ffn_up, w_ffn_down, ln2_g, ln2_b):
    B, S, D = x.shape
    T = B * S
    top_k = min(IDX_TOPK_MAX, S // 4)
    assert S % (2 * KEY_CHUNK) == 0 and D == D_MODEL
    tm = min(512, S)
    tb = _relbias_tiles(rel_bias)
    row = lambda a: a.reshape(1, -1)
    for l in range(w_in.shape[0]):
        wa, wt = _split_w_in(w_in[l])
        p_lru, p_gla, dq, dv, iq, iw, kt, kit = _proj(x, wa, wt, tm=tm)
        y_lru = _lru(p_lru, conv_w[l], row(conv_b[l]),
                     _block_diag(lru_wa[l]).astype(BF16), row(lru_ba[l]),
                     _block_diag(lru_wx[l]).astype(BF16), row(lru_bx[l]), row(lru_lambda[l]), tl=tm)
        w2 = jnp.pad(gla_w_gate2[l], ((0, LANES - GLA_GATE_RANK), (0, 0)))
        y_gla = _gla(p_gla, w2, row(gla_b_gate[l]), row(gla_norm_g[l]), tg=min(256, S))
        y_dsa = _dsa(iq, iw, dq, kit, kt, dv, tb, top_k=top_k)
        wo = w_out[l].astype(BF16)
        x1 = _mix(x.reshape(T, D), y_lru.reshape(T, -1), y_gla.reshape(T, -1), y_dsa.reshape(T, -1),
                  wo[:LRU_WIDTH], wo[LRU_WIDTH:LRU_WIDTH + GLA_WIDTH], wo[LRU_WIDTH + GLA_WIDTH:],
                  row(ln1_g[l]), row(ln1_b[l]), tm=tm)
        x2 = _ffn(x1, w_ffn_gate[l].astype(BF16), w_ffn_up[l].astype(BF16), w_ffn_down[l].astype(BF16),
                  row(ln2_g[l]), row(ln2_b[l]), tm=tm, tf=D_FF // 2)
        x = x2.reshape(B, S, D)
    return x
```

```python
import functools
import math

import jax
import jax.numpy as jnp
import numpy as np
from jax import lax
from jax.experimental import pallas as pl
from jax.experimental.pallas import tpu as pltpu

F32 = jnp.float32
BF16 = jnp.bfloat16
I32 = jnp.int32

D_MODEL = 1024
DEPTH = 2
LRU_WIDTH = 256
LRU_BLOCKS = 4
LRU_BLOCK = LRU_WIDTH // LRU_BLOCKS
CONV_WIDTH = 4
LRU_C = 8.0
GLA_HEADS = 6
GLA_DK = 64
GLA_DV = 64
GLA_GATE_RANK = 16
GLA_GATE_TAU = 16.0
GLA_CHUNK = 64
GLA_WIDTH = GLA_HEADS * GLA_DV
DSA_HEADS = 6
DSA_HEAD_DIM = 64
DSA_WIDTH = DSA_HEADS * DSA_HEAD_DIM
IDX_HEADS = 4
IDX_DIM = 32
IDX_TOPK_MAX = 256
REL_BUCKETS = 32
REL_MAX_DIST = 128
D_FF = 2816
DN_ALPHA = (2.0 * DEPTH) ** 0.25
LN_EPS = 1e-5

IN_SIZES = (LRU_WIDTH, LRU_WIDTH, GLA_WIDTH, GLA_WIDTH, GLA_WIDTH, GLA_WIDTH, GLA_GATE_RANK,
            DSA_WIDTH, DSA_WIDTH, DSA_WIDTH, IDX_HEADS * IDX_DIM, IDX_DIM, IDX_HEADS)
IN_OFFSETS = [0] + [int(o) for o in np.cumsum(IN_SIZES)]

LANES = 128
SUBLANES = 8
VMEM_LIMIT_BYTES = 56 * 1024 * 1024

LRU_COLS = 2 * LRU_WIDTH
GLA_COLS = 4 * GLA_WIDTH + LANES
PA_LRU = 0
PA_GLA = PA_LRU + LRU_COLS
PA_DQ = PA_GLA + GLA_COLS
PA_DV = PA_DQ + DSA_WIDTH
PA_IQ = PA_DV + DSA_WIDTH
PA_IW = PA_IQ + LANES
PA_END = PA_IW + LANES
PT_ROWS = DSA_WIDTH + IDX_DIM

Q_BLOCK = 256
Q_SUB = Q_BLOCK // LANES
KEY_CHUNK = 512
HALF_BITS = 16
INT_MIN = -2147483648
NEG_INF_KEY = -2139095041
NEG_BIG = -1e30
LOG2E = 1.4426950408889634
REL_SLABS = 3
FAST_SOFTMAX_MAX_SHIFT = 120.0


def _nt_dot(a, b):
    return lax.dot_general(a, b, (((1,), (1,)), ((), ())), preferred_element_type=F32)


def _tn_dot(a, b):
    return lax.dot_general(a, b, (((0,), (0,)), ((), ())), preferred_element_type=F32)


def _split3(x):
    x1 = x.astype(BF16)
    r1 = x - x1.astype(F32)
    x2 = r1.astype(BF16)
    return x1, x2, (r1 - x2.astype(F32)).astype(BF16)


def _softplus(z):
    return jnp.maximum(z, 0.0) + jnp.log(1.0 + jnp.exp(-jnp.abs(z)))


def _sigmoid(z):
    return 1.0 / (1.0 + jnp.exp(-z))


def _proj_kernel(x_ref, wa_ref, wt_ref, lru_ref, gla_ref, q_ref, v_ref, iq_ref, iw_ref, kt_ref, kit_ref):
    xb = x_ref[...].astype(BF16)

    def mm(c0, c1):
        return jnp.dot(xb, wa_ref[:, c0:c1], preferred_element_type=F32)

    lru_ref[...] = mm(PA_LRU, PA_GLA)
    gla_ref[...] = mm(PA_GLA, PA_DQ)
    q_ref[...] = mm(PA_DQ, PA_DV)
    v_ref[...] = mm(PA_DV, PA_IQ).astype(BF16)
    iq_ref[...] = mm(PA_IQ, PA_IW).astype(BF16)
    iw_ref[...] = mm(PA_IW, PA_END)
    t = _nt_dot(wt_ref[...], xb)
    kt_ref[...] = t[:DSA_WIDTH].astype(BF16)
    kit_ref[...] = t[DSA_WIDTH:PT_ROWS].astype(BF16)


def _proj(x, wa, wt, *, tm):
    B, S, D = x.shape
    grid = (B, S // tm)
    row = lambda b, m: (b, m, 0)
    col = lambda b, m: (b, 0, m)
    const = lambda b, m: (0, 0)
    out_shape = (
        jax.ShapeDtypeStruct((B, S, LRU_COLS), F32),
        jax.ShapeDtypeStruct((B, S, GLA_COLS), F32),
        jax.ShapeDtypeStruct((B, S, DSA_WIDTH), F32),
        jax.ShapeDtypeStruct((B, S, DSA_WIDTH), BF16),
        jax.ShapeDtypeStruct((B, S, LANES), BF16),
        jax.ShapeDtypeStruct((B, S, LANES), F32),
        jax.ShapeDtypeStruct((B, DSA_WIDTH, S), BF16),
        jax.ShapeDtypeStruct((B, IDX_DIM, S), BF16),
    )
    out_specs = (
        pl.BlockSpec((None, tm, LRU_COLS), row),
        pl.BlockSpec((None, tm, GLA_COLS), row),
        pl.BlockSpec((None, tm, DSA_WIDTH), row),
        pl.BlockSpec((None, tm, DSA_WIDTH), row),
        pl.BlockSpec((None, tm, LANES), row),
        pl.BlockSpec((None, tm, LANES), row),
        pl.BlockSpec((None, DSA_WIDTH, tm), col),
        pl.BlockSpec((None, IDX_DIM, tm), col),
    )
    return pl.pallas_call(
        _proj_kernel,
        grid=grid,
        in_specs=[pl.BlockSpec((None, tm, D), row),
                  pl.BlockSpec((D, PA_END), const),
                  pl.BlockSpec((PT_ROWS, D), const)],
        out_specs=out_specs,
        out_shape=out_shape,
        compiler_params=pltpu.CompilerParams(
            dimension_semantics=("parallel", "parallel"), vmem_limit_bytes=VMEM_LIMIT_BYTES),
        name="in_proj",
    )(x, wa, wt)


def _shift_rows(x, d, fill, row):
    return jnp.where(row >= d, pltpu.roll(x, d, 0), fill)


def _lru_kernel(p_ref, cw_ref, cb_ref, wa_ref, ba_ref, wx_ref, bx_ref, lam_ref, y_ref, xprev_ref, h_ref):
    tl = p_ref.shape[0]

    @pl.when(pl.program_id(1) == 0)
    def _():
        xprev_ref[...] = jnp.zeros_like(xprev_ref)
        h_ref[...] = jnp.zeros_like(h_ref)

    xb = p_ref[:, 0:LRU_WIDTH]
    gb = p_ref[:, LRU_WIDTH:2 * LRU_WIDTH]
    prev = xprev_ref[...]
    row8 = lax.broadcasted_iota(I32, (SUBLANES, LRU_WIDTH), 0)
    cw = cw_ref[...]
    xc = cb_ref[...] + cw[CONV_WIDTH - 1:CONV_WIDTH] * xb
    for d in range(1, CONV_WIDTH):
        r = pltpu.roll(xb, d, 0)
        top = jnp.where(row8 < d, pltpu.roll(prev, d, 0), r[0:SUBLANES])
        r = jnp.concatenate([top, r[SUBLANES:]], axis=0)
        xc = xc + cw[CONV_WIDTH - 1 - d:CONV_WIDTH - d] * r
    xprev_ref[...] = xb[tl - SUBLANES:tl]

    xcb = xc.astype(BF16)
    r_gate = _sigmoid(jnp.dot(xcb, wa_ref[...], preferred_element_type=F32) + ba_ref[...])
    i_gate = _sigmoid(jnp.dot(xcb, wx_ref[...], preferred_element_type=F32) + bx_ref[...])
    log_a = (-LRU_C) * r_gate * _softplus(-lam_ref[...])
    a = jnp.exp(log_a)
    u = jnp.sqrt(1.0 - a * a) * (i_gate * xc)

    row = lax.broadcasted_iota(I32, (tl, LRU_WIDTH), 0)
    d = 1
    while d < tl:
        a_s = _shift_rows(a, d, 1.0, row)
        u_s = _shift_rows(u, d, 0.0, row)
        u = u + a * u_s
        a = a * a_s
        d *= 2
    h = u + a * h_ref[SUBLANES - 1:SUBLANES, :]
    h_ref[...] = h[tl - SUBLANES:tl]
    y_ref[...] = h * jax.nn.gelu(gb)


def _lru(p_lru, cw, cb, wa, ba, wx, bx, lam, *, tl):
    B, S, _ = p_lru.shape
    const = lambda b, j: (0, 0)
    vec = pl.BlockSpec((1, LRU_WIDTH), const)
    mat = pl.BlockSpec((LRU_WIDTH, LRU_WIDTH), const)
    return pl.pallas_call(
        _lru_kernel,
        grid=(B, S // tl),
        in_specs=[pl.BlockSpec((None, tl, LRU_COLS), lambda b, j: (b, j, 0)),
                  pl.BlockSpec((CONV_WIDTH, LRU_WIDTH), const), vec, mat, vec, mat, vec, vec],
        out_specs=pl.BlockSpec((None, tl, LRU_WIDTH), lambda b, j: (b, j, 0)),
        out_shape=jax.ShapeDtypeStruct((B, S, LRU_WIDTH), F32),
        scratch_shapes=[pltpu.VMEM((SUBLANES, LRU_WIDTH), F32), pltpu.VMEM((SUBLANES, LRU_WIDTH), F32)],
        compiler_params=pltpu.CompilerParams(
            dimension_semantics=("parallel", "arbitrary"), vmem_limit_bytes=VMEM_LIMIT_BYTES),
        name="rg_lru",
    )(p_lru, cw, cb, wa, ba, wx, bx, lam)


def _gla_kernel(q_ref, k_ref, v_ref, g_ref, glr_ref, w2_ref, bg_ref, ng_ref, tri_ref, mean_ref,
                y_ref, st_ref):
    tg = q_ref.shape[0]
    C = GLA_CHUNK
    hi = lax.Precision.HIGHEST

    @pl.when(pl.program_id(1) == 0)
    def _():
        st_ref[...] = jnp.zeros_like(st_ref)

    z = jnp.dot(glr_ref[...], w2_ref[...], preferred_element_type=F32, precision=hi) + bg_ref[...]
    log_alpha = -_softplus(-z) * (1.0 / GLA_GATE_TAU)

    lane = lax.broadcasted_iota(I32, (C, LANES), 1)
    first = lane < GLA_DK
    rr = lax.broadcasted_iota(I32, (C, C), 0)
    cc = lax.broadcasted_iota(I32, (C, C), 1)
    causal = cc <= rr
    causal2 = jnp.concatenate([causal, causal], axis=0)
    r2 = lax.broadcasted_iota(I32, (LANES, LANES), 0)
    c2 = lax.broadcasted_iota(I32, (LANES, LANES), 1)
    same_head = (r2 < GLA_DV) == (c2 < GLA_DK)
    tri = tri_ref[...]

    for c in range(tg // C):
        rows = slice(c * C, (c + 1) * C)
        bcum = sum(jnp.dot(tri, piece, preferred_element_type=F32)
                   for piece in _split3(log_alpha[rows]))
        blast = bcum[C - 1:C]
        kf = k_ref[rows, :]
        q_dec = q_ref[rows, :] * (GLA_DK ** -0.5) * jnp.exp(bcum)
        k_inv = kf * jnp.exp(-bcum)
        k_end = kf * jnp.exp(blast - bcum)
        decay = jnp.exp(blast)
        vf = v_ref[rows, :]
        outs = []
        for p in range(GLA_HEADS // 2):
            cs = slice(p * LANES, (p + 1) * LANES)
            qd, ki, ke, vp = q_dec[:, cs], k_inv[:, cs], k_end[:, cs], vf[:, cs]
            kib = ki.astype(BF16)
            vpb = vp.astype(BF16)
            q_two = jnp.concatenate([jnp.where(first, qd, 0.0), jnp.where(first, 0.0, qd)], axis=0).astype(BF16)
            att = jnp.where(causal2, _nt_dot(q_two, kib), 0.0).astype(BF16)
            o_two = jnp.dot(att, vpb, preferred_element_type=F32)
            o_intra = jnp.where(first, o_two[:C], o_two[C:])
            st = st_ref[p]
            o_inter = _nt_dot(qd.astype(BF16), st.astype(BF16))
            u_t = _tn_dot(vpb, ke.astype(BF16))
            st_ref[p] = st * decay[:, cs] + jnp.where(same_head, u_t, 0.0)
            outs.append(o_intra + o_inter)
        o = jnp.concatenate(outs, axis=1)
        ms = sum(jnp.dot(piece, mean_ref[...], preferred_element_type=F32)
                 for piece in _split3(o * o))
        o = o * lax.rsqrt(ms + 1e-6) * ng_ref[...]
        gf = g_ref[rows, :]
        y_ref[rows, :] = o * (gf * _sigmoid(gf))


def _gla(p_gla, w2, bg, ng, *, tg):
    B, S, _ = p_gla.shape
    W = GLA_WIDTH
    const = lambda b, j: (0, 0)
    tri = jnp.tril(jnp.ones((GLA_CHUNK, GLA_CHUNK), F32)).astype(BF16)
    head = jnp.arange(W) // GLA_DV
    mean_blk = ((head[:, None] == head[None, :]).astype(F32) / GLA_DV).astype(BF16)

    def colblk(c):
        return pl.BlockSpec((None, tg, W), lambda b, j: (b, j, c))

    return pl.pallas_call(
        _gla_kernel,
        grid=(B, S // tg),
        in_specs=[colblk(0), colblk(1), colblk(2), colblk(3),
                  pl.BlockSpec((None, tg, LANES), lambda b, j: (b, j, 4 * W // LANES)),
                  pl.BlockSpec((LANES, W), const),
                  pl.BlockSpec((1, W), const), pl.BlockSpec((1, W), const),
                  pl.BlockSpec((GLA_CHUNK, GLA_CHUNK), const),
                  pl.BlockSpec((W, W), const)],
        out_specs=pl.BlockSpec((None, tg, W), lambda b, j: (b, j, 0)),
        out_shape=jax.ShapeDtypeStruct((B, S, W), F32),
        scratch_shapes=[pltpu.VMEM((GLA_HEADS // 2, LANES, LANES), F32)],
        compiler_params=pltpu.CompilerParams(
            dimension_semantics=("parallel", "arbitrary"), vmem_limit_bytes=VMEM_LIMIT_BYTES),
        name="gla",
    )(p_gla, p_gla, p_gla, p_gla, p_gla, w2, bg, ng, tri, mean_blk)


def _relbias_kernel(rb_ref, tb_ref):
    r = lax.broadcasted_iota(I32, (LANES, LANES), 0)
    c = lax.broadcasted_iota(I32, (LANES, LANES), 1)
    max_exact = REL_BUCKETS // 2
    for delta in range(REL_SLABS):
        n = jnp.maximum(delta * LANES + r - c, 0)
        nf = jnp.maximum(n, 1).astype(F32)
        large = max_exact + (jnp.log(nf / max_exact) / math.log(REL_MAX_DIST / max_exact)
                             * (REL_BUCKETS - max_exact)).astype(I32)
        large = jnp.minimum(large, REL_BUCKETS - 1)
        bucket = jnp.where(n < max_exact, n, large)
        for h in range(DSA_HEADS):
            acc = jnp.zeros((LANES, LANES), F32)
            for b in range(REL_BUCKETS):
                acc = jnp.where(bucket == b, rb_ref[b, h], acc)
            tb_ref[h, delta] = (acc - rb_ref[REL_BUCKETS - 1, h]) * LOG2E


def _relbias_tiles(rel_bias):
    return pl.pallas_call(
        _relbias_kernel,
        in_specs=[pl.BlockSpec(memory_space=pltpu.SMEM)],
        out_specs=pl.BlockSpec(memory_space=pltpu.VMEM),
        out_shape=jax.ShapeDtypeStruct((DSA_HEADS, REL_SLABS, LANES, LANES), F32),
        name="rel_bias_tiles",
    )(rel_bias)


def _dsa_kernel(iq_ref, iw_ref, q_ref, kit_ref, kt_ref, v_ref, tb_ref, uo_ref, y_ref,
                key_ref, kmx_ref, m_ref, acc_ref, *, top_k):
    i = pl.program_id(1)
    TQ, CK, L = Q_BLOCK, KEY_CHUNK, LANES
    S = key_ref.shape[1]
    first_blk = i * Q_SUB
    n_chunks = (first_blk + Q_SUB - 1) // (CK // L) + 1
    lane = lax.broadcasted_iota(I32, (TQ, L), 1)

    @pl.when(i == 0)
    def _():
        nb = 2 * CK
        for h in range(DSA_HEADS):
            def norm_chunk(c, mx):
                kk = kt_ref[h * DSA_HEAD_DIM:(h + 1) * DSA_HEAD_DIM, pl.ds(pl.multiple_of(c * nb, nb), nb)]
                kk = kk.astype(F32)
                return jnp.maximum(mx, jnp.sum(kk * kk, axis=0, keepdims=True))
            mx = lax.fori_loop(0, S // nb, norm_chunk, jnp.zeros((1, nb), F32))
            kmx_ref[h:h + 1, :] = jnp.broadcast_to(jnp.max(mx, axis=1, keepdims=True), (1, L))

    iq = iq_ref[...]
    iq_h = [iq[:, h * IDX_DIM:(h + 1) * IDX_DIM] for h in range(IDX_HEADS)]
    scale = (IDX_DIM ** -0.5) * (IDX_HEADS ** -0.5)
    w_h = [jnp.broadcast_to(iw_ref[:, h:h + 1] * scale, (TQ, CK)) for h in range(IDX_HEADS)]

    def score_chunk(c, masked):
        off = pl.multiple_of(c * CK, CK)
        kc = kit_ref[:, pl.ds(off, CK)]
        sc = jnp.zeros((TQ, CK), F32)
        for h in range(IDX_HEADS):
            z = jnp.dot(iq_h[h], kc, preferred_element_type=F32)
            sc = sc + jnp.maximum(z, 0.0) * w_h[h]
        sc = sc + 0.0
        if masked:
            row_t = i * TQ + lax.broadcasted_iota(I32, (TQ, CK), 0)
            pos = off + lax.broadcasted_iota(I32, (TQ, CK), 1)
            sc = jnp.where(pos <= row_t, sc, -jnp.inf)
        bits = pltpu.bitcast(sc, I32)
        key_ref[:, pl.ds(off, CK)] = bits ^ ((bits >> 31) & I32(0x7FFFFFFF))

    def score_body(c, carry):
        score_chunk(c, False)
        return carry

    def score_pair(c, carry):
        score_chunk(2 * c, False)
        score_chunk(2 * c + 1, False)
        return carry

    def score_quad(c, carry):
        score_pair(2 * c, carry)
        return score_pair(2 * c + 1, carry)

    n_plain = n_chunks - 1
    lax.fori_loop(0, n_plain >> 2, score_quad, 0)
    lax.fori_loop((n_plain >> 2) * 2, n_plain >> 1, score_pair, 0)
    lax.fori_loop((n_plain >> 1) * 2, n_plain, score_body, 0)
    score_chunk(n_chunks - 1, True)

    @pl.when((n_chunks & 1) == 1)
    def _():
        key_ref[:, pl.ds(pl.multiple_of(n_chunks * CK, CK), CK)] = jnp.full((TQ, CK), NEG_INF_KEY, I32)

    def count_ge(cand):
        cand_all = jnp.broadcast_to(cand, (TQ, L))
        accs = []
        for r in range(Q_SUB):
            rows = slice(r * L, (r + 1) * L)
            cand_b = cand_all[rows]

            def body(c, acc, rows=rows, cand_b=cand_b):
                kk = key_ref[rows, pl.ds(pl.multiple_of(c * (2 * CK), 2 * CK), 2 * CK)]
                for s in range(2 * CK // L):
                    acc = acc + (kk[:, s * L:(s + 1) * L] >= cand_b).astype(I32)
                return acc
            accs.append(lax.fori_loop(0, (n_chunks + 1) >> 1, body, jnp.zeros((L, L), I32)))
        acc = accs[0] if Q_SUB == 1 else jnp.concatenate(accs, axis=0)
        return jnp.sum(acc, axis=1, keepdims=True)

    def bit_step(thr, cnt_thr, bit):
        cand = thr + bit
        cnt = count_ge(cand)
        ok = cnt >= top_k
        return jnp.where(ok, cand, thr), jnp.where(ok, cnt, cnt_thr)

    def high_step(b, st):
        return bit_step(st[0], st[1], lax.shift_left(I32(1), I32(31) - b))

    thr, cnt_thr = lax.fori_loop(
        0, 32 - HALF_BITS, high_step,
        (jnp.full((TQ, 1), INT_MIN, I32), jnp.full((TQ, 1), S, I32)))

    cnt_next = count_ge(thr + 1)
    exact = cnt_next < top_k

    def low_cond(st):
        b, _, _, active = st
        return (b < HALF_BITS) & (active > 0)

    def low_step(st):
        b, thr, cnt_thr, _ = st
        new_thr, new_cnt = bit_step(thr, cnt_thr, lax.shift_left(I32(1), I32(HALF_BITS - 1) - b))
        thr = jnp.where(exact, thr, new_thr)
        cnt_thr = jnp.where(exact, cnt_thr, new_cnt)
        active = jnp.max(jnp.where(exact | (cnt_thr == top_k), 0, 1))
        return b + 1, thr, cnt_thr, active

    active0 = jnp.max(jnp.where(exact | (cnt_thr == top_k), 0, 1))
    _, thr, cnt_thr, _ = lax.while_loop(low_cond, low_step, (I32(0), thr, cnt_thr, active0))

    t_sel = jnp.where(cnt_thr == top_k, thr - 1, thr)
    need = (top_k - count_ge(t_sel + 1)).astype(F32)
    t_sel_b = jnp.broadcast_to(t_sel, (TQ, L))
    need_b = jnp.broadcast_to(need, (TQ, L))
    uo = uo_ref[...]

    def mask_chunk(c, run):
        off = pl.multiple_of(c * CK, CK)
        kc = key_ref[:, pl.ds(off, CK)]
        out = []
        for s in range(CK // L):
            kk = kc[:, s * L:(s + 1) * L]
            eq = kk == t_sel_b
            pr = jnp.dot(jnp.where(eq, 1.0, 0.0).astype(BF16), uo, preferred_element_type=F32)
            sel = (kk > t_sel_b) | (eq & (run + pr[:, :L] <= need_b))
            sel = sel & (kk > NEG_INF_KEY)
            out.append(jnp.where(sel, 0.0, -jnp.inf).astype(F32))
            run = run + pr[:, L:]
        key_ref[:, pl.ds(off, CK)] = pltpu.bitcast(jnp.concatenate(out, axis=1), I32)
        return run

    def mask_pair(c, run):
        return mask_chunk(2 * c + 1, mask_chunk(2 * c, run))

    run = lax.fori_loop(0, n_chunks >> 1, mask_pair, jnp.zeros((TQ, L), F32))
    lax.fori_loop((n_chunks >> 1) * 2, n_chunks, mask_chunk, run)

    q2 = q_ref[...] * LOG2E
    q_ext = []
    shift_max = jnp.zeros((TQ, L), F32)
    for h in range(DSA_HEADS):
        blk = q2[:, (h // 2) * L:(h // 2 + 1) * L]
        if h % 2:
            blk = pltpu.roll(blk, DSA_HEAD_DIM, 1)
        qf = jnp.where(lane < DSA_HEAD_DIM, blk, 0.0).astype(BF16).astype(F32)
        bound = jnp.sqrt(jnp.sum(qf * qf, axis=1, keepdims=True) * kmx_ref[h:h + 1, :]) * (1.0 + 2.0 ** -7)
        shift_max = jnp.maximum(shift_max, bound)
        q_ext.append(jnp.where(lane == DSA_HEAD_DIM, bound, qf).astype(BF16))
    minus_one_row = jnp.where(lax.broadcasted_iota(I32, (DSA_HEAD_DIM, CK), 0) == 0, -1.0, 0.0).astype(BF16)
    ones_cols = jnp.ones((CK, L), BF16)

    acc_ref[...] = jnp.zeros(acc_ref.shape, F32)

    def attend(c, biased, fast):
        off = pl.multiple_of(c * CK, CK)
        width = CK
        mb = pltpu.bitcast(key_ref[:, pl.ds(off, width)], F32)
        for h in range(DSA_HEADS):
            k_ext = jnp.concatenate(
                [kt_ref[h * DSA_HEAD_DIM:(h + 1) * DSA_HEAD_DIM, pl.ds(off, width)], minus_one_row], axis=0)
            s = jnp.dot(q_ext[h], k_ext, preferred_element_type=F32) + mb
            if biased:
                def tile(r, j):
                    return tb_ref[h, jnp.clip(first_blk + r - (c * (CK // L) + j), 0, REL_SLABS - 1)]
                s = s + jnp.concatenate(
                    [jnp.concatenate([tile(r, j) for r in range(Q_SUB)], axis=0) for j in range(CK // L)], axis=1)
            pair = (h // 2) * L
            v_ext = jnp.concatenate([v_ref[pl.ds(off, width), pair:pair + L], ones_cols], axis=1)
            if fast:
                p = jnp.exp2(s).astype(BF16)
                acc_ref[h] += jnp.dot(p, v_ext, preferred_element_type=F32)
            else:
                m_old = m_ref[h]
                m_new = jnp.maximum(m_old, jnp.max(s, axis=1, keepdims=True))
                alpha = jnp.exp2(m_old - m_new)
                p = jnp.concatenate(
                    [jnp.exp2(s[:, j * L:(j + 1) * L] - m_new) for j in range(width // L)], axis=1)
                pv = jnp.dot(p.astype(BF16), v_ext, preferred_element_type=F32)
                acc_ref[h] = acc_ref[h] * jnp.concatenate([alpha, alpha], axis=1) + pv
                m_ref[h] = m_new

    def attend_all(fast):
        first_biased = jnp.maximum(first_blk - 1, 0) // (CK // L)

        def far_chunk(c, carry):
            attend(c, False, fast)
            return carry

        def near_chunk(c, carry):
            attend(c, True, fast)
            return carry

        def far_pair(c, carry):
            attend(2 * c, False, fast)
            attend(2 * c + 1, False, fast)
            return carry

        def far_quad(c, carry):
            far_pair(2 * c, carry)
            return far_pair(2 * c + 1, carry)

        single_from = 0
        if fast:
            single_from = (first_biased >> 1) * 2
            lax.fori_loop(0, first_biased >> 2, far_quad, 0)
            lax.fori_loop((first_biased >> 2) * 2, first_biased >> 1, far_pair, 0)
        lax.fori_loop(single_from, first_biased, far_chunk, 0)
        lax.fori_loop(first_biased, n_chunks, near_chunk, 0)

    use_fast = 2.0 * jnp.max(shift_max) + jnp.max(jnp.abs(tb_ref[...])) <= FAST_SOFTMAX_MAX_SHIFT

    @pl.when(use_fast)
    def _():
        attend_all(True)

    @pl.when(jnp.logical_not(use_fast))
    def _():
        m_ref[...] = jnp.full(m_ref.shape, NEG_BIG, F32)
        attend_all(False)

    for p in range(DSA_HEADS // 2):
        a0, a1 = acc_ref[2 * p], acc_ref[2 * p + 1]
        y_ref[:, p * L:(p + 1) * L] = jnp.where(lane < DSA_HEAD_DIM, a0[:, :L] / a0[:, L:], a1[:, :L] / a1[:, L:])


def _dsa(iq, iw, q, kit, kt, v, tb, *, top_k):
    B, S, W = q.shape
    upper = jnp.triu(jnp.ones((LANES, LANES), F32))
    uo = jnp.concatenate([upper, jnp.ones((LANES, LANES), F32)], axis=1).astype(BF16)
    qrow = lambda b, i: (b, i, 0)
    whole = lambda b, i: (b, 0, 0)
    one = pl.Buffered(1)
    return pl.pallas_call(
        functools.partial(_dsa_kernel, top_k=top_k),
        grid=(B, S // Q_BLOCK),
        in_specs=[pl.BlockSpec((None, Q_BLOCK, LANES), qrow),
                  pl.BlockSpec((None, Q_BLOCK, LANES), qrow),
                  pl.BlockSpec((None, Q_BLOCK, W), qrow),
                  pl.BlockSpec((None, IDX_DIM, S), whole, pipeline_mode=one),
                  pl.BlockSpec((None, W, S), whole, pipeline_mode=one),
                  pl.BlockSpec((None, S, W), whole, pipeline_mode=one),
                  pl.BlockSpec((DSA_HEADS, REL_SLABS, LANES, LANES), lambda b, i: (0, 0, 0, 0)),
                  pl.BlockSpec((LANES, 2 * LANES), lambda b, i: (0, 0))],
        out_specs=pl.BlockSpec((None, Q_BLOCK, W), qrow),
        out_shape=jax.ShapeDtypeStruct((B, S, W), F32),
        scratch_shapes=[pltpu.VMEM((Q_BLOCK, S), I32),
                        pltpu.VMEM((SUBLANES, LANES), F32),
                        pltpu.VMEM((DSA_HEADS, Q_BLOCK, LANES), F32),
                        pltpu.VMEM((DSA_HEADS, Q_BLOCK, 2 * LANES), F32)],
        compiler_params=pltpu.CompilerParams(
            dimension_semantics=("parallel", "arbitrary"), vmem_limit_bytes=VMEM_LIMIT_BYTES),
        name="dsa",
    )(iq, iw, q, kit, kt, v, tb, uo)


def _layer_norm(z, g, b):
    mu = jnp.mean(z, axis=-1, keepdims=True)
    zc = z - mu
    var = jnp.mean(zc * zc, axis=-1, keepdims=True)
    return zc * lax.rsqrt(var + LN_EPS) * g + b


def _mix_kernel(x_ref, ya_ref, yb_ref, yc_ref, wa_ref, wb_ref, wc_ref, g_ref, b_ref, o_ref):
    mix = jnp.dot(ya_ref[...].astype(BF16), wa_ref[...], preferred_element_type=F32)
    mix = mix + jnp.dot(yb_ref[...].astype(BF16), wb_ref[...], preferred_element_type=F32)
    mix = mix + jnp.dot(yc_ref[...].astype(BF16), wc_ref[...], preferred_element_type=F32)
    o_ref[...] = _layer_norm(DN_ALPHA * x_ref[...] + mix, g_ref[...], b_ref[...])


def _mix(x, ya, yb, yc, wa, wb, wc, g, b, *, tm):
    T, D = x.shape
    row = lambda m: (m, 0)
    const = lambda m: (0, 0)
    return pl.pallas_call(
        _mix_kernel,
        grid=(T // tm,),
        in_specs=[pl.BlockSpec((tm, D), row),
                  pl.BlockSpec((tm, ya.shape[1]), row), pl.BlockSpec((tm, yb.shape[1]), row),
                  pl.BlockSpec((tm, yc.shape[1]), row),
                  pl.BlockSpec(wa.shape, const), pl.BlockSpec(wb.shape, const), pl.BlockSpec(wc.shape, const),
                  pl.BlockSpec((1, D), const), pl.BlockSpec((1, D), const)],
        out_specs=pl.BlockSpec((tm, D), row),
        out_shape=jax.ShapeDtypeStruct((T, D), F32),
        compiler_params=pltpu.CompilerParams(
            dimension_semantics=("parallel",), vmem_limit_bytes=VMEM_LIMIT_BYTES),
        name="out_proj_ln",
    )(x, ya, yb, yc, wa, wb, wc, g, b)


def _ffn_kernel(x_ref, wg_ref, wu_ref, wd_ref, g_ref, b_ref, o_ref, xb_ref, acc_ref):
    f = pl.program_id(1)

    @pl.when(f == 0)
    def _():
        xb_ref[...] = x_ref[...].astype(BF16)
        acc_ref[...] = jnp.zeros_like(acc_ref)

    xb = xb_ref[...]
    gate = jnp.dot(xb, wg_ref[...], preferred_element_type=F32)
    up = jnp.dot(xb, wu_ref[...], preferred_element_type=F32)
    h = (gate * _sigmoid(gate)) * up
    acc_ref[...] += jnp.dot(h.astype(BF16), wd_ref[...], preferred_element_type=F32)

    @pl.when(f == pl.num_programs(1) - 1)
    def _():
        o_ref[...] = _layer_norm(DN_ALPHA * x_ref[...] + acc_ref[...], g_ref[...], b_ref[...])


def _ffn(x, wg, wu, wd, g, b, *, tm, tf):
    T, D = x.shape
    FF = wg.shape[1]
    row = lambda m, f: (m, 0)
    const = lambda m, f: (0, 0)
    return pl.pallas_call(
        _ffn_kernel,
        grid=(T // tm, FF // tf),
        in_specs=[pl.BlockSpec((tm, D), row),
                  pl.BlockSpec((D, tf), lambda m, f: (0, f)),
                  pl.BlockSpec((D, tf), lambda m, f: (0, f)),
                  pl.BlockSpec((tf, D), lambda m, f: (f, 0)),
                  pl.BlockSpec((1, D), const), pl.BlockSpec((1, D), const)],
        out_specs=pl.BlockSpec((tm, D), row),
        out_shape=jax.ShapeDtypeStruct((T, D), F32),
        scratch_shapes=[pltpu.VMEM((tm, D), BF16), pltpu.VMEM((tm, D), F32)],
        compiler_params=pltpu.CompilerParams(
            dimension_semantics=("parallel", "arbitrary"), vmem_limit_bytes=VMEM_LIMIT_BYTES),
        name="ffn_ln",
    )(x, wg, wu, wd, g, b)


def _split_w_in(w):
    parts = [w[:, IN_OFFSETS[j]:IN_OFFSETS[j + 1]] for j in range(len(IN_SIZES))]
    lru_x, lru_g, gq, gk, gv, gg, g_lr, dq, dk, dv, iq, ik, iw = parts
    D = w.shape[0]
    pad = lambda a, n: jnp.pad(a, ((0, 0), (0, n - a.shape[1])))
    wa = jnp.concatenate([lru_x, lru_g, gq, gk, gv, gg, pad(g_lr, LANES),
                          dq * (DSA_HEAD_DIM ** -0.5), dv, iq, pad(iw, LANES)], axis=1)
    wt = jnp.concatenate([dk, ik], axis=1).T
    assert wa.shape == (D, PA_END) and wt.shape == (PT_ROWS, D)
    return wa.astype(BF16), wt.astype(BF16)


def _block_diag(w):
    n, k, _ = w.shape
    eye = jnp.eye(n, dtype=w.dtype)
    return (eye[:, None, :, None] * w[:, :, None, :]).reshape(n * k, n * k)


def kernel(x, w_in, conv_w, conv_b, lru_wa, lru_ba, lru_wx, lru_bx, lru_lambda, gla_w_gate2, gla_b_gate,
           gla_norm_g, rel_bias, w_out, ln1_g, ln1_b, w_ffn_gate, w_ffn_up, w_ffn_down, ln2_g, ln2_b):
    B, S, D = x.shape
    T = B * S
    top_k = min(IDX_TOPK_MAX, S // 4)
    assert S % (2 * KEY_CHUNK) == 0 and D == D_MODEL
    tm = min(512, S)
    tb = _relbias_tiles(rel_bias)
    row = lambda a: a.reshape(1, -1)
    for l in range(w_in.shape[0]):
        wa, wt = _split_w_in(w_in[l])
        p_lru, p_gla, dq, dv, iq, iw, kt, kit = _proj(x, wa, wt, tm=tm)
        y_lru = _lru(p_lru, conv_w[l], row(conv_b[l]),
                     _block_diag(lru_wa[l]).astype(BF16), row(lru_ba[l]),
                     _block_diag(lru_wx[l]).astype(BF16), row(lru_bx[l]), row(lru_lambda[l]), tl=tm)
        w2 = jnp.pad(gla_w_gate2[l], ((0, LANES - GLA_GATE_RANK), (0, 0)))
        y_gla = _gla(p_gla, w2, row(gla_b_gate[l]), row(gla_norm_g[l]), tg=min(256, S))
        y_dsa = _dsa(iq, iw, dq, kit, kt, dv, tb, top_k=top_k)
        wo = w_out[l].astype(BF16)
        x1 = _mix(x.reshape(T, D), y_lru.reshape(T, -1), y_gla.reshape(T, -1), y_dsa.reshape(T, -1),
                  wo[:LRU_WIDTH], wo[LRU_WIDTH:LRU_WIDTH + GLA_WIDTH], wo[LRU_WIDTH + GLA_WIDTH:],
                  row(ln1_g[l]), row(ln1_b[l]), tm=tm)
        x2 = _ffn(x1, w_ffn_gate[l].astype(BF16), w_ffn_up[l].astype(BF16), w_ffn_down[l].astype(BF16),
                  row(ln2_g[l]), row(ln2_b[l]), tm=tm, tf=D_FF // 2)
        x = x2.reshape(B, S, D)
    return x
```

```python
import functools
import math

import jax
import jax.numpy as jnp
import numpy as np
from jax import lax
from jax.experimental import pallas as pl
from jax.experimental.pallas import tpu as pltpu

F32 = jnp.float32
BF16 = jnp.bfloat16
I32 = jnp.int32

D_MODEL = 1024
DEPTH = 2
LRU_WIDTH = 256
LRU_BLOCKS = 4
LRU_BLOCK = LRU_WIDTH // LRU_BLOCKS
CONV_WIDTH = 4
LRU_C = 8.0
GLA_HEADS = 6
GLA_DK = 64
GLA_DV = 64
GLA_GATE_RANK = 16
GLA_GATE_TAU = 16.0
GLA_CHUNK = 64
GLA_WIDTH = GLA_HEADS * GLA_DV
DSA_HEADS = 6
DSA_HEAD_DIM = 64
DSA_WIDTH = DSA_HEADS * DSA_HEAD_DIM
IDX_HEADS = 4
IDX_DIM = 32
IDX_TOPK_MAX = 256
REL_BUCKETS = 32
REL_MAX_DIST = 128
D_FF = 2816
DN_ALPHA = (2.0 * DEPTH) ** 0.25
LN_EPS = 1e-5

IN_SIZES = (LRU_WIDTH, LRU_WIDTH, GLA_WIDTH, GLA_WIDTH, GLA_WIDTH, GLA_WIDTH, GLA_GATE_RANK,
            DSA_WIDTH, DSA_WIDTH, DSA_WIDTH, IDX_HEADS * IDX_DIM, IDX_DIM, IDX_HEADS)
IN_OFFSETS = [0] + [int(o) for o in np.cumsum(IN_SIZES)]

LANES = 128
SUBLANES = 8
VMEM_LIMIT_BYTES = 56 * 1024 * 1024

LRU_COLS = 2 * LRU_WIDTH
GLA_COLS = 4 * GLA_WIDTH + LANES
PA_LRU = 0
PA_GLA = PA_LRU + LRU_COLS
PA_DQ = PA_GLA + GLA_COLS
PA_DV = PA_DQ + DSA_WIDTH
PA_IQ = PA_DV + DSA_WIDTH
PA_IW = PA_IQ + LANES
PA_END = PA_IW + LANES
PT_ROWS = DSA_WIDTH + IDX_DIM

Q_BLOCK = 256
Q_SUB = Q_BLOCK // LANES
KEY_CHUNK = 512
HALF_BITS = 16
INT_MIN = -2147483648
NEG_INF_KEY = -2139095041
NEG_BIG = -1e30
LOG2E = 1.4426950408889634
REL_SLABS = 3
FAST_SOFTMAX_MAX_SHIFT = 120.0


def _nt_dot(a, b):
    return lax.dot_general(a, b, (((1,), (1,)), ((), ())), preferred_element_type=F32)


def _tn_dot(a, b):
    return lax.dot_general(a, b, (((0,), (0,)), ((), ())), preferred_element_type=F32)


def _split3(x):
    x1 = x.astype(BF16)
    r1 = x - x1.astype(F32)
    x2 = r1.astype(BF16)
    return x1, x2, (r1 - x2.astype(F32)).astype(BF16)


def _softplus(z):
    return jnp.maximum(z, 0.0) + jnp.log(1.0 + jnp.exp(-jnp.abs(z)))


def _sigmoid(z):
    return 1.0 / (1.0 + jnp.exp(-z))


def _proj_kernel(x_ref, wa_ref, wt_ref, lru_ref, gla_ref, q_ref, v_ref, iq_ref, iw_ref, kt_ref, kit_ref):
    xb = x_ref[...].astype(BF16)

    def mm(c0, c1):
        return jnp.dot(xb, wa_ref[:, c0:c1], preferred_element_type=F32)

    lru_ref[...] = mm(PA_LRU, PA_GLA)
    gla_ref[...] = mm(PA_GLA, PA_DQ)
    q_ref[...] = mm(PA_DQ, PA_DV)
    v_ref[...] = mm(PA_DV, PA_IQ).astype(BF16)
    iq_ref[...] = mm(PA_IQ, PA_IW).astype(BF16)
    iw_ref[...] = mm(PA_IW, PA_END)
    t = _nt_dot(wt_ref[...], xb)
    kt_ref[...] = t[:DSA_WIDTH].astype(BF16)
    kit_ref[...] = t[DSA_WIDTH:PT_ROWS].astype(BF16)


def _proj(x, wa, wt, *, tm):
    B, S, D = x.shape
    grid = (B, S // tm)
    row = lambda b, m: (b, m, 0)
    col = lambda b, m: (b, 0, m)
    const = lambda b, m: (0, 0)
    out_shape = (
        jax.ShapeDtypeStruct((B, S, LRU_COLS), F32),
        jax.ShapeDtypeStruct((B, S, GLA_COLS), F32),
        jax.ShapeDtypeStruct((B, S, DSA_WIDTH), F32),
        jax.ShapeDtypeStruct((B, S, DSA_WIDTH), BF16),
        jax.ShapeDtypeStruct((B, S, LANES), BF16),
        jax.ShapeDtypeStruct((B, S, LANES), F32),
        jax.ShapeDtypeStruct((B, DSA_WIDTH, S), BF16),
        jax.ShapeDtypeStruct((B, IDX_DIM, S), BF16),
    )
    out_specs = (
        pl.BlockSpec((None, tm, LRU_COLS), row),
        pl.BlockSpec((None, tm, GLA_COLS), row),
        pl.BlockSpec((None, tm, DSA_WIDTH), row),
        pl.BlockSpec((None, tm, DSA_WIDTH), row),
        pl.BlockSpec((None, tm, LANES), row),
        pl.BlockSpec((None, tm, LANES), row),
        pl.BlockSpec((None, DSA_WIDTH, tm), col),
        pl.BlockSpec((None, IDX_DIM, tm), col),
    )
    return pl.pallas_call(
        _proj_kernel,
        grid=grid,
        in_specs=[pl.BlockSpec((None, tm, D), row),
                  pl.BlockSpec((D, PA_END), const),
                  pl.BlockSpec((PT_ROWS, D), const)],
        out_specs=out_specs,
        out_shape=out_shape,
        compiler_params=pltpu.CompilerParams(
            dimension_semantics=("parallel", "parallel"), vmem_limit_bytes=VMEM_LIMIT_BYTES),
        name="in_proj",
    )(x, wa, wt)


def _shift_rows(x, d, fill, row):
    return jnp.where(row >= d, pltpu.roll(x, d, 0), fill)


def _lru_kernel(p_ref, cw_ref, cb_ref, wa_ref, ba_ref, wx_ref, bx_ref, lam_ref, y_ref, xprev_ref, h_ref):
    tl = p_ref.shape[0]

    @pl.when(pl.program_id(1) == 0)
    def _():
        xprev_ref[...] = jnp.zeros_like(xprev_ref)
        h_ref[...] = jnp.zeros_like(h_ref)

    xb = p_ref[:, 0:LRU_WIDTH]
    gb = p_ref[:, LRU_WIDTH:2 * LRU_WIDTH]
    prev = xprev_ref[...]
    row8 = lax.broadcasted_iota(I32, (SUBLANES, LRU_WIDTH), 0)
    cw = cw_ref[...]
    xc = cb_ref[...] + cw[CONV_WIDTH - 1:CONV_WIDTH] * xb
    for d in range(1, CONV_WIDTH):
        r = pltpu.roll(xb, d, 0)
        top = jnp.where(row8 < d, pltpu.roll(prev, d, 0), r[0:SUBLANES])
        r = jnp.concatenate([top, r[SUBLANES:]], axis=0)
        xc = xc + cw[CONV_WIDTH - 1 - d:CONV_WIDTH - d] * r
    xprev_ref[...] = xb[tl - SUBLANES:tl]

    xcb = xc.astype(BF16)
    r_gate = _sigmoid(jnp.dot(xcb, wa_ref[...], preferred_element_type=F32) + ba_ref[...])
    i_gate = _sigmoid(jnp.dot(xcb, wx_ref[...], preferred_element_type=F32) + bx_ref[...])
    log_a = (-LRU_C) * r_gate * _softplus(-lam_ref[...])
    a = jnp.exp(log_a)
    u = jnp.sqrt(1.0 - a * a) * (i_gate * xc)

    row = lax.broadcasted_iota(I32, (tl, LRU_WIDTH), 0)
    d = 1
    while d < tl:
        a_s = _shift_rows(a, d, 1.0, row)
        u_s = _shift_rows(u, d, 0.0, row)
        u = u + a * u_s
        a = a * a_s
        d *= 2
    h = u + a * h_ref[SUBLANES - 1:SUBLANES, :]
    h_ref[...] = h[tl - SUBLANES:tl]
    y_ref[...] = h * jax.nn.gelu(gb)


def _lru(p_lru, cw, cb, wa, ba, wx, bx, lam, *, tl):
    B, S, _ = p_lru.shape
    const = lambda b, j: (0, 0)
    vec = pl.BlockSpec((1, LRU_WIDTH), const)
    mat = pl.BlockSpec((LRU_WIDTH, LRU_WIDTH), const)
    return pl.pallas_call(
        _lru_kernel,
        grid=(B, S // tl),
        in_specs=[pl.BlockSpec((None, tl, LRU_COLS), lambda b, j: (b, j, 0)),
                  pl.BlockSpec((CONV_WIDTH, LRU_WIDTH), const), vec, mat, vec, mat, vec, vec],
        out_specs=pl.BlockSpec((None, tl, LRU_WIDTH), lambda b, j: (b, j, 0)),
        out_shape=jax.ShapeDtypeStruct((B, S, LRU_WIDTH), F32),
        scratch_shapes=[pltpu.VMEM((SUBLANES, LRU_WIDTH), F32), pltpu.VMEM((SUBLANES, LRU_WIDTH), F32)],
        compiler_params=pltpu.CompilerParams(
            dimension_semantics=("parallel", "arbitrary"), vmem_limit_bytes=VMEM_LIMIT_BYTES),
        name="rg_lru",
    )(p_lru, cw, cb, wa, ba, wx, bx, lam)


def _gla_kernel(q_ref, k_ref, v_ref, g_ref, glr_ref, w2_ref, bg_ref, ng_ref, tri_ref, mean_ref,
                y_ref, st_ref):
    tg = q_ref.shape[0]
    C = GLA_CHUNK
    hi = lax.Precision.HIGHEST

    @pl.when(pl.program_id(1) == 0)
    def _():
        st_ref[...] = jnp.zeros_like(st_ref)

    z = jnp.dot(glr_ref[...], w2_ref[...], preferred_element_type=F32, precision=hi) + bg_ref[...]
    log_alpha = -_softplus(-z) * (1.0 / GLA_GATE_TAU)

    lane = lax.broadcasted_iota(I32, (C, LANES), 1)
    first = lane < GLA_DK
    rr = lax.broadcasted_iota(I32, (C, C), 0)
    cc = lax.broadcasted_iota(I32, (C, C), 1)
    causal = cc <= rr
    causal2 = jnp.concatenate([causal, causal], axis=0)
    r2 = lax.broadcasted_iota(I32, (LANES, LANES), 0)
    c2 = lax.broadcasted_iota(I32, (LANES, LANES), 1)
    same_head = (r2 < GLA_DV) == (c2 < GLA_DK)
    tri = tri_ref[...]

    for c in range(tg // C):
        rows = slice(c * C, (c + 1) * C)
        bcum = sum(jnp.dot(tri, piece, preferred_element_type=F32)
                   for piece in _split3(log_alpha[rows]))
        blast = bcum[C - 1:C]
        kf = k_ref[rows, :]
        q_dec = q_ref[rows, :] * (GLA_DK ** -0.5) * jnp.exp(bcum)
        k_inv = kf * jnp.exp(-bcum)
        k_end = kf * jnp.exp(blast - bcum)
        decay = jnp.exp(blast)
        vf = v_ref[rows, :]
        outs = []
        for p in range(GLA_HEADS // 2):
            cs = slice(p * LANES, (p + 1) * LANES)
            qd, ki, ke, vp = q_dec[:, cs], k_inv[:, cs], k_end[:, cs], vf[:, cs]
            kib = ki.astype(BF16)
            vpb = vp.astype(BF16)
            q_two = jnp.concatenate([jnp.where(first, qd, 0.0), jnp.where(first, 0.0, qd)], axis=0).astype(BF16)
            att = jnp.where(causal2, _nt_dot(q_two, kib), 0.0).astype(BF16)
            o_two = jnp.dot(att, vpb, preferred_element_type=F32)
            o_intra = jnp.where(first, o_two[:C], o_two[C:])
            st = st_ref[p]
            o_inter = _nt_dot(qd.astype(BF16), st.astype(BF16))
            u_t = _tn_dot(vpb, ke.astype(BF16))
            st_ref[p] = st * decay[:, cs] + jnp.where(same_head, u_t, 0.0)
            outs.append(o_intra + o_inter)
        o = jnp.concatenate(outs, axis=1)
        ms = sum(jnp.dot(piece, mean_ref[...], preferred_element_type=F32)
                 for piece in _split3(o * o))
        o = o * lax.rsqrt(ms + 1e-6) * ng_ref[...]
        gf = g_ref[rows, :]
        y_ref[rows, :] = o * (gf * _sigmoid(gf))


def _gla(p_gla, w2, bg, ng, *, tg):
    B, S, _ = p_gla.shape
    W = GLA_WIDTH
    const = lambda b, j: (0, 0)
    tri = jnp.tril(jnp.ones((GLA_CHUNK, GLA_CHUNK), F32)).astype(BF16)
    head = jnp.arange(W) // GLA_DV
    mean_blk = ((head[:, None] == head[None, :]).astype(F32) / GLA_DV).astype(BF16)

    def colblk(c):
        return pl.BlockSpec((None, tg, W), lambda b, j: (b, j, c))

    return pl.pallas_call(
        _gla_kernel,
        grid=(B, S // tg),
        in_specs=[colblk(0), colblk(1), colblk(2), colblk(3),
                  pl.BlockSpec((None, tg, LANES), lambda b, j: (b, j, 4 * W // LANES)),
                  pl.BlockSpec((LANES, W), const),
                  pl.BlockSpec((1, W), const), pl.BlockSpec((1, W), const),
                  pl.BlockSpec((GLA_CHUNK, GLA_CHUNK), const),
                  pl.BlockSpec((W, W), const)],
        out_specs=pl.BlockSpec((None, tg, W), lambda b, j: (b, j, 0)),
        out_shape=jax.ShapeDtypeStruct((B, S, W), F32),
        scratch_shapes=[pltpu.VMEM((GLA_HEADS // 2, LANES, LANES), F32)],
        compiler_params=pltpu.CompilerParams(
            dimension_semantics=("parallel", "arbitrary"), vmem_limit_bytes=VMEM_LIMIT_BYTES),
        name="gla",
    )(p_gla, p_gla, p_gla, p_gla, p_gla, w2, bg, ng, tri, mean_blk)


def _relbias_kernel(rb_ref, tb_ref):
    r = lax.broadcasted_iota(I32, (LANES, LANES), 0)
    c = lax.broadcasted_iota(I32, (LANES, LANES), 1)
    max_exact = REL_BUCKETS // 2
    for delta in range(REL_SLABS):
        n = jnp.maximum(delta * LANES + r - c, 0)
        nf = jnp.maximum(n, 1).astype(F32)
        large = max_exact + (jnp.log(nf / max_exact) / math.log(REL_MAX_DIST / max_exact)
                             * (REL_BUCKETS - max_exact)).astype(I32)
        large = jnp.minimum(large, REL_BUCKETS - 1)
        bucket = jnp.where(n < max_exact, n, large)
        for h in range(DSA_HEADS):
            acc = jnp.zeros((LANES, LANES), F32)
            for b in range(REL_BUCKETS):
                acc = jnp.where(bucket == b, rb_ref[b, h], acc)
            tb_ref[h, delta] = (acc - rb_ref[REL_BUCKETS - 1, h]) * LOG2E


def _relbias_tiles(rel_bias):
    return pl.pallas_call(
        _relbias_kernel,
        in_specs=[pl.BlockSpec(memory_space=pltpu.SMEM)],
        out_specs=pl.BlockSpec(memory_space=pltpu.VMEM),
        out_shape=jax.ShapeDtypeStruct((DSA_HEADS, REL_SLABS, LANES, LANES), F32),
        name="rel_bias_tiles",
    )(rel_bias)


def _dsa_kernel(iq_ref, iw_ref, q_ref, kit_ref, kt_ref, v_ref, tb_ref, uo_ref, y_ref,
                key_ref, kmx_ref, m_ref, acc_ref, *, top_k):
    i = pl.program_id(1)
    TQ, CK, L = Q_BLOCK, KEY_CHUNK, LANES
    S = key_ref.shape[1]
    first_blk = i * Q_SUB
    n_chunks = (first_blk + Q_SUB - 1) // (CK // L) + 1
    lane = lax.broadcasted_iota(I32, (TQ, L), 1)

    @pl.when(i == 0)
    def _():
        nb = 2 * CK
        for h in range(DSA_HEADS):
            def norm_chunk(c, mx):
                kk = kt_ref[h * DSA_HEAD_DIM:(h + 1) * DSA_HEAD_DIM, pl.ds(pl.multiple_of(c * nb, nb), nb)]
                kk = kk.astype(F32)
                return jnp.maximum(mx, jnp.sum(kk * kk, axis=0, keepdims=True))
            mx = lax.fori_loop(0, S // nb, norm_chunk, jnp.zeros((1, nb), F32))
            kmx_ref[h:h + 1, :] = jnp.broadcast_to(jnp.max(mx, axis=1, keepdims=True), (1, L))

    iq = iq_ref[...]
    iq_h = [iq[:, h * IDX_DIM:(h + 1) * IDX_DIM] for h in range(IDX_HEADS)]
    scale = (IDX_DIM ** -0.5) * (IDX_HEADS ** -0.5)
    w_h = [jnp.broadcast_to(iw_ref[:, h:h + 1] * scale, (TQ, CK)) for h in range(IDX_HEADS)]

    def score_chunk(c, masked):
        off = pl.multiple_of(c * CK, CK)
        kc = kit_ref[:, pl.ds(off, CK)]
        sc = jnp.zeros((TQ, CK), F32)
        for h in range(IDX_HEADS):
            z = jnp.dot(iq_h[h], kc, preferred_element_type=F32)
            sc = sc + jnp.maximum(z, 0.0) * w_h[h]
        sc = sc + 0.0
        if masked:
            row_t = i * TQ + lax.broadcasted_iota(I32, (TQ, CK), 0)
            pos = off + lax.broadcasted_iota(I32, (TQ, CK), 1)
            sc = jnp.where(pos <= row_t, sc, -jnp.inf)
        bits = pltpu.bitcast(sc, I32)
        key_ref[:, pl.ds(off, CK)] = bits ^ ((bits >> 31) & I32(0x7FFFFFFF))

    def score_body(c, carry):
        score_chunk(c, False)
        return carry

    def score_pair(c, carry):
        score_chunk(2 * c, False)
        score_chunk(2 * c + 1, False)
        return carry

    def score_quad(c, carry):
        score_pair(2 * c, carry)
        return score_pair(2 * c + 1, carry)

    n_plain = n_chunks - 1
    lax.fori_loop(0, n_plain >> 2, score_quad, 0)
    lax.fori_loop((n_plain >> 2) * 2, n_plain >> 1, score_pair, 0)
    lax.fori_loop((n_plain >> 1) * 2, n_plain, score_body, 0)
    score_chunk(n_chunks - 1, True)

    @pl.when((n_chunks & 1) == 1)
    def _():
        key_ref[:, pl.ds(pl.multiple_of(n_chunks * CK, CK), CK)] = jnp.full((TQ, CK), NEG_INF_KEY, I32)

    def count_ge(cand):
        cand_all = jnp.broadcast_to(cand, (TQ, L))
        accs = []
        for r in range(Q_SUB):
            rows = slice(r * L, (r + 1) * L)
            cand_b = cand_all[rows]

            def body(c, acc, rows=rows, cand_b=cand_b):
                kk = key_ref[rows, pl.ds(pl.multiple_of(c * (2 * CK), 2 * CK), 2 * CK)]
                for s in range(2 * CK // L):
                    acc = acc + (kk[:, s * L:(s + 1) * L] >= cand_b).astype(I32)
                return acc
            def body2(c, acc, body=body):
                return body(2 * c + 1, body(2 * c, acc))
            n_pairs = (n_chunks + 1) >> 1
            acc = lax.fori_loop(0, n_pairs >> 1, body2, jnp.zeros((L, L), I32))
            accs.append(lax.fori_loop((n_pairs >> 1) * 2, n_pairs, body, acc))
        acc = accs[0] if Q_SUB == 1 else jnp.concatenate(accs, axis=0)
        return jnp.sum(acc, axis=1, keepdims=True)

    def bit_step(thr, cnt_thr, bit):
        cand = thr + bit
        cnt = count_ge(cand)
        ok = cnt >= top_k
        return jnp.where(ok, cand, thr), jnp.where(ok, cnt, cnt_thr)

    def high_step(b, st):
        return bit_step(st[0], st[1], lax.shift_left(I32(1), I32(31) - b))

    thr, cnt_thr = lax.fori_loop(
        0, 32 - HALF_BITS, high_step,
        (jnp.full((TQ, 1), INT_MIN, I32), jnp.full((TQ, 1), S, I32)))

    cnt_next = count_ge(thr + 1)
    exact = cnt_next < top_k

    def low_cond(st):
        b, _, _, active = st
        return (b < HALF_BITS) & (active > 0)

    def low_step(st):
        b, thr, cnt_thr, _ = st
        new_thr, new_cnt = bit_step(thr, cnt_thr, lax.shift_left(I32(1), I32(HALF_BITS - 1) - b))
        thr = jnp.where(exact, thr, new_thr)
        cnt_thr = jnp.where(exact, cnt_thr, new_cnt)
        active = jnp.max(jnp.where(exact | (cnt_thr == top_k), 0, 1))
        return b + 1, thr, cnt_thr, active

    active0 = jnp.max(jnp.where(exact | (cnt_thr == top_k), 0, 1))
    _, thr, cnt_thr, _ = lax.while_loop(low_cond, low_step, (I32(0), thr, cnt_thr, active0))

    t_sel = jnp.where(cnt_thr == top_k, thr - 1, thr)
    need = (top_k - count_ge(t_sel + 1)).astype(F32)
    t_sel_b = jnp.broadcast_to(t_sel, (TQ, L))
    need_b = jnp.broadcast_to(need, (TQ, L))
    uo = uo_ref[...]

    def mask_chunk(c, run):
        off = pl.multiple_of(c * CK, CK)
        kc = key_ref[:, pl.ds(off, CK)]
        out = []
        for s in range(CK // L):
            kk = kc[:, s * L:(s + 1) * L]
            eq = kk == t_sel_b
            pr = jnp.dot(jnp.where(eq, 1.0, 0.0).astype(BF16), uo, preferred_element_type=F32)
            sel = (kk > t_sel_b) | (eq & (run + pr[:, :L] <= need_b))
            sel = sel & (kk > NEG_INF_KEY)
            out.append(jnp.where(sel, 0.0, -jnp.inf).astype(F32))
            run = run + pr[:, L:]
        key_ref[:, pl.ds(off, CK)] = pltpu.bitcast(jnp.concatenate(out, axis=1), I32)
        return run

    def mask_pair(c, run):
        return mask_chunk(2 * c + 1, mask_chunk(2 * c, run))

    run = lax.fori_loop(0, n_chunks >> 1, mask_pair, jnp.zeros((TQ, L), F32))
    lax.fori_loop((n_chunks >> 1) * 2, n_chunks, mask_chunk, run)

    q2 = q_ref[...] * LOG2E
    q_ext = []
    shift_max = jnp.zeros((TQ, L), F32)
    for h in range(DSA_HEADS):
        blk = q2[:, (h // 2) * L:(h // 2 + 1) * L]
        if h % 2:
            blk = pltpu.roll(blk, DSA_HEAD_DIM, 1)
        qf = jnp.where(lane < DSA_HEAD_DIM, blk, 0.0).astype(BF16).astype(F32)
        bound = jnp.sqrt(jnp.sum(qf * qf, axis=1, keepdims=True) * kmx_ref[h:h + 1, :]) * (1.0 + 2.0 ** -7)
        shift_max = jnp.maximum(shift_max, bound)
        q_ext.append(jnp.where(lane == DSA_HEAD_DIM, bound, qf).astype(BF16))
    minus_one_row = jnp.where(lax.broadcasted_iota(I32, (DSA_HEAD_DIM, CK), 0) == 0, -1.0, 0.0).astype(BF16)
    ones_cols = jnp.ones((CK, L), BF16)

    acc_ref[...] = jnp.zeros(acc_ref.shape, F32)

    def attend(c, biased, fast):
        off = pl.multiple_of(c * CK, CK)
        width = CK
        mb = pltpu.bitcast(key_ref[:, pl.ds(off, width)], F32)
        for h in range(DSA_HEADS):
            k_ext = jnp.concatenate(
                [kt_ref[h * DSA_HEAD_DIM:(h + 1) * DSA_HEAD_DIM, pl.ds(off, width)], minus_one_row], axis=0)
            s = jnp.dot(q_ext[h], k_ext, preferred_element_type=F32) + mb
            if biased:
                def tile(r, j):
                    return tb_ref[h, jnp.clip(first_blk + r - (c * (CK // L) + j), 0, REL_SLABS - 1)]
                s = s + jnp.concatenate(
                    [jnp.concatenate([tile(r, j) for r in range(Q_SUB)], axis=0) for j in range(CK // L)], axis=1)
            pair = (h // 2) * L
            v_ext = jnp.concatenate([v_ref[pl.ds(off, width), pair:pair + L], ones_cols], axis=1)
            if fast:
                p = jnp.exp2(s).astype(BF16)
                acc_ref[h] += jnp.dot(p, v_ext, preferred_element_type=F32)
            else:
                m_old = m_ref[h]
                m_new = jnp.maximum(m_old, jnp.max(s, axis=1, keepdims=True))
                alpha = jnp.exp2(m_old - m_new)
                p = jnp.concatenate(
                    [jnp.exp2(s[:, j * L:(j + 1) * L] - m_new) for j in range(width // L)], axis=1)
                pv = jnp.dot(p.astype(BF16), v_ext, preferred_element_type=F32)
                acc_ref[h] = acc_ref[h] * jnp.concatenate([alpha, alpha], axis=1) + pv
                m_ref[h] = m_new

    def attend_all(fast):
        first_biased = jnp.maximum(first_blk - 1, 0) // (CK // L)

        def far_chunk(c, carry):
            attend(c, False, fast)
            return carry

        def near_chunk(c, carry):
            attend(c, True, fast)
            return carry

        def far_pair(c, carry):
            attend(2 * c, False, fast)
            attend(2 * c + 1, False, fast)
            return carry

        def far_quad(c, carry):
            far_pair(2 * c, carry)
            return far_pair(2 * c + 1, carry)

        single_from = 0
        if fast:
            single_from = (first_biased >> 1) * 2
            lax.fori_loop(0, first_biased >> 2, far_quad, 0)
            lax.fori_loop((first_biased >> 2) * 2, first_biased >> 1, far_pair, 0)
        lax.fori_loop(single_from, first_biased, far_chunk, 0)
        lax.fori_loop(first_biased, n_chunks, near_chunk, 0)

    use_fast = 2.0 * jnp.max(shift_max) + jnp.max(jnp.abs(tb_ref[...])) <= FAST_SOFTMAX_MAX_SHIFT

    @pl.when(use_fast)
    def _():
        attend_all(True)

    @pl.when(jnp.logical_not(use_fast))
    def _():
        m_ref[...] = jnp.full(m_ref.shape, NEG_BIG, F32)
        attend_all(False)

    for p in range(DSA_HEADS // 2):
        a0, a1 = acc_ref[2 * p], acc_ref[2 * p + 1]
        y_ref[:, p * L:(p + 1) * L] = jnp.where(lane < DSA_HEAD_DIM, a0[:, :L] / a0[:, L:], a1[:, :L] / a1[:, L:])


def _dsa(iq, iw, q, kit, kt, v, tb, *, top_k):
    B, S, W = q.shape
    upper = jnp.triu(jnp.ones((LANES, LANES), F32))
    uo = jnp.concatenate([upper, jnp.ones((LANES, LANES), F32)], axis=1).astype(BF16)
    qrow = lambda b, i: (b, i, 0)
    whole = lambda b, i: (b, 0, 0)
    one = pl.Buffered(1)
    return pl.pallas_call(
        functools.partial(_dsa_kernel, top_k=top_k),
        grid=(B, S // Q_BLOCK),
        in_specs=[pl.BlockSpec((None, Q_BLOCK, LANES), qrow),
                  pl.BlockSpec((None, Q_BLOCK, LANES), qrow),
                  pl.BlockSpec((None, Q_BLOCK, W), qrow),
                  pl.BlockSpec((None, IDX_DIM, S), whole, pipeline_mode=one),
                  pl.BlockSpec((None, W, S), whole, pipeline_mode=one),
                  pl.BlockSpec((None, S, W), whole, pipeline_mode=one),
                  pl.BlockSpec((DSA_HEADS, REL_SLABS, LANES, LANES), lambda b, i: (0, 0, 0, 0)),
                  pl.BlockSpec((LANES, 2 * LANES), lambda b, i: (0, 0))],
        out_specs=pl.BlockSpec((None, Q_BLOCK, W), qrow),
        out_shape=jax.ShapeDtypeStruct((B, S, W), F32),
        scratch_shapes=[pltpu.VMEM((Q_BLOCK, S), I32),
                        pltpu.VMEM((SUBLANES, LANES), F32),
                        pltpu.VMEM((DSA_HEADS, Q_BLOCK, LANES), F32),
                        pltpu.VMEM((DSA_HEADS, Q_BLOCK, 2 * LANES), F32)],
        compiler_params=pltpu.CompilerParams(
            dimension_semantics=("parallel", "arbitrary"), vmem_limit_bytes=VMEM_LIMIT_BYTES),
        name="dsa",
    )(iq, iw, q, kit, kt, v, tb, uo)


def _layer_norm(z, g, b):
    mu = jnp.mean(z, axis=-1, keepdims=True)
    zc = z - mu
    var = jnp.mean(zc * zc, axis=-1, keepdims=True)
    return zc * lax.rsqrt(var + LN_EPS) * g + b


def _mix_kernel(x_ref, ya_ref, yb_ref, yc_ref, wa_ref, wb_ref, wc_ref, g_ref, b_ref, o_ref):
    mix = jnp.dot(ya_ref[...].astype(BF16), wa_ref[...], preferred_element_type=F32)
    mix = mix + jnp.dot(yb_ref[...].astype(BF16), wb_ref[...], preferred_element_type=F32)
    mix = mix + jnp.dot(yc_ref[...].astype(BF16), wc_ref[...], preferred_element_type=F32)
    o_ref[...] = _layer_norm(DN_ALPHA * x_ref[...] + mix, g_ref[...], b_ref[...])


def _mix(x, ya, yb, yc, wa, wb, wc, g, b, *, tm):
    T, D = x.shape
    row = lambda m: (m, 0)
    const = lambda m: (0, 0)
    return pl.pallas_call(
        _mix_kernel,
        grid=(T // tm,),
        in_specs=[pl.BlockSpec((tm, D), row),
                  pl.BlockSpec((tm, ya.shape[1]), row), pl.BlockSpec((tm, yb.shape[1]), row),
                  pl.BlockSpec((tm, yc.shape[1]), row),
                  pl.BlockSpec(wa.shape, const), pl.BlockSpec(wb.shape, const), pl.BlockSpec(wc.shape, const),
                  pl.BlockSpec((1, D), const), pl.BlockSpec((1, D), const)],
        out_specs=pl.BlockSpec((tm, D), row),
        out_shape=jax.ShapeDtypeStruct((T, D), F32),
        compiler_params=pltpu.CompilerParams(
            dimension_semantics=("parallel",), vmem_limit_bytes=VMEM_LIMIT_BYTES),
        name="out_proj_ln",
    )(x, ya, yb, yc, wa, wb, wc, g, b)


def _ffn_kernel(x_ref, wg_ref, wu_ref, wd_ref, g_ref, b_ref, o_ref, xb_ref, acc_ref):
    f = pl.program_id(1)

    @pl.when(f == 0)
    def _():
        xb_ref[...] = x_ref[...].astype(BF16)
        acc_ref[...] = jnp.zeros_like(acc_ref)

    xb = xb_ref[...]
    gate = jnp.dot(xb, wg_ref[...], preferred_element_type=F32)
    up = jnp.dot(xb, wu_ref[...], preferred_element_type=F32)
    h = (gate * _sigmoid(gate)) * up
    acc_ref[...] += jnp.dot(h.astype(BF16), wd_ref[...], preferred_element_type=F32)

    @pl.when(f == pl.num_programs(1) - 1)
    def _():
        o_ref[...] = _layer_norm(DN_ALPHA * x_ref[...] + acc_ref[...], g_ref[...], b_ref[...])


def _ffn(x, wg, wu, wd, g, b, *, tm, tf):
    T, D = x.shape
    FF = wg.shape[1]
    row = lambda m, f: (m, 0)
    const = lambda m, f: (0, 0)
    return pl.pallas_call(
        _ffn_kernel,
        grid=(T // tm, FF // tf),
        in_specs=[pl.BlockSpec((tm, D), row),
                  pl.BlockSpec((D, tf), lambda m, f: (0, f)),
                  pl.BlockSpec((D, tf), lambda m, f: (0, f)),
                  pl.BlockSpec((tf, D), lambda m, f: (f, 0)),
                  pl.BlockSpec((1, D), const), pl.BlockSpec((1, D), const)],
        out_specs=pl.BlockSpec((tm, D), row),
        out_shape=jax.ShapeDtypeStruct((T, D), F32),
        scratch_shapes=[pltpu.VMEM((tm, D), BF16), pltpu.VMEM((tm, D), F32)],
        compiler_params=pltpu.CompilerParams(
            dimension_semantics=("parallel", "arbitrary"), vmem_limit_bytes=VMEM_LIMIT_BYTES),
        name="ffn_ln",
    )(x, wg, wu, wd, g, b)


def _split_w_in(w):
    parts = [w[:, IN_OFFSETS[j]:IN_OFFSETS[j + 1]] for j in range(len(IN_SIZES))]
    lru_x, lru_g, gq, gk, gv, gg, g_lr, dq, dk, dv, iq, ik, iw = parts
    D = w.shape[0]
    pad = lambda a, n: jnp.pad(a, ((0, 0), (0, n - a.shape[1])))
    wa = jnp.concatenate([lru_x, lru_g, gq, gk, gv, gg, pad(g_lr, LANES),
                          dq * (DSA_HEAD_DIM ** -0.5), dv, iq, pad(iw, LANES)], axis=1)
    wt = jnp.concatenate([dk, ik], axis=1).T
    assert wa.shape == (D, PA_END) and wt.shape == (PT_ROWS, D)
    return wa.astype(BF16), wt.astype(BF16)


def _block_diag(w):
    n, k, _ = w.shape
    eye = jnp.eye(n, dtype=w.dtype)
    return (eye[:, None, :, None] * w[:, :, None, :]).reshape(n * k, n * k)


def kernel(x, w_in, conv_w, conv_b, lru_wa, lru_ba, lru_wx, lru_bx, lru_lambda, gla_w_gate2, gla_b_gate,
           gla_norm_g, rel_bias, w_out, ln1_g, ln1_b, w_ffn_gate, w_ffn_up, w_ffn_down, ln2_g, ln2_b):
    B, S, D = x.shape
    T = B * S
    top_k = min(IDX_TOPK_MAX, S // 4)
    assert S % (2 * KEY_CHUNK) == 0 and D == D_MODEL
    tm = min(512, S)
    tb = _relbias_tiles(rel_bias)
    row = lambda a: a.reshape(1, -1)
    for l in range(w_in.shape[0]):
        wa, wt = _split_w_in(w_in[l])
        p_lru, p_gla, dq, dv, iq, iw, kt, kit = _proj(x, wa, wt, tm=tm)
        y_lru = _lru(p_lru, conv_w[l], row(conv_b[l]),
                     _block_diag(lru_wa[l]).astype(BF16), row(lru_ba[l]),
                     _block_diag(lru_wx[l]).astype(BF16), row(lru_bx[l]), row(lru_lambda[l]), tl=tm)
        w2 = jnp.pad(gla_w_gate2[l], ((0, LANES - GLA_GATE_RANK), (0, 0)))
        y_gla = _gla(p_gla, w2, row(gla_b_gate[l]), row(gla_norm_g[l]), tg=min(256, S))
        y_dsa = _dsa(iq, iw, dq, kit, kt, dv, tb, top_k=top_k)
        wo = w_out[l].astype(BF16)
        x1 = _mix(x.reshape(T, D), y_lru.reshape(T, -1), y_gla.reshape(T, -1), y_dsa.reshape(T, -1),
                  wo[:LRU_WIDTH], wo[LRU_WIDTH:LRU_WIDTH + GLA_WIDTH], wo[LRU_WIDTH + GLA_WIDTH:],
                  row(ln1_g[l]), row(ln1_b[l]), tm=tm)
        x2 = _ffn(x1, w_ffn_gate[l].astype(BF16), w_ffn_up[l].astype(BF16), w_ffn_down[l].astype(BF16),
                  row(ln2_g[l]), row(ln2_b[l]), tm=tm, tf=D_FF // 2)
        x = x2.reshape(B, S, D)
    return x
```
